```python
import jax, jax.numpy as jnp
from jax import lax
import numpy as np

D_MODEL = 1024
BATCH = 8
SEQ = 4096
DEPTH = 1
DEC_BATCH = 2
DEC_SEQ = 8192
PAST_LEN = 128

N_META = 16
N_FOURIER_GROUPS = 4
FOURIER_GROUP_DIM = D_MODEL // 8
FOURIER_DIM = N_FOURIER_GROUPS * FOURIER_GROUP_DIM
N_HEADS = 8
QK_NOPE_DIM = 128
QK_ROPE_DIM = 64
QK_HEAD_DIM = QK_NOPE_DIM + QK_ROPE_DIM
V_HEAD_DIM = 128
Q_LORA_RANK = D_MODEL // 2
KV_LORA_RANK = D_MODEL // 4
ATTN_DIM = N_HEADS * V_HEAD_DIM
D_FF = ((-(-8 * D_MODEL // 3) + 255) // 256) * 256
ROPE_THETA = 10000.0
NORM_EPS = 1e-6
Q_BLOCK = 128
ATTN_SCALE = QK_HEAD_DIM ** -0.5
IN_SPLITS = (
    FOURIER_DIM,
    FOURIER_DIM + Q_LORA_RANK,
    FOURIER_DIM + Q_LORA_RANK + KV_LORA_RANK,
    FOURIER_DIM + Q_LORA_RANK + KV_LORA_RANK + QK_ROPE_DIM,
    FOURIER_DIM + Q_LORA_RANK + KV_LORA_RANK + QK_ROPE_DIM + D_MODEL,
)
IN_PROJ_DIM = IN_SPLITS[-1] + D_MODEL

kernel_name = "fnet_mla_gated_hybrid_encoder"


def _rmsnorm(x, g):
    xf = x.astype(jnp.float32)
    y = xf * lax.rsqrt(jnp.mean(xf * xf, axis=-1, keepdims=True) + NORM_EPS)
    return (y * g.astype(jnp.float32)).astype(x.dtype)


def _rope_tables(length):
    inv = 1.0 / (ROPE_THETA ** (jnp.arange(0, QK_ROPE_DIM, 2, dtype=jnp.float32) / QK_ROPE_DIM))
    ang = jnp.arange(length, dtype=jnp.float32)[:, None] * inv[None, :]
    return jnp.cos(ang), jnp.sin(ang)


def _apply_rope(x, cos, sin):
    xf = x.astype(jnp.float32)
    x1, x2 = jnp.split(xf, 2, axis=-1)
    c = cos[None, :, None, :]
    s = sin[None, :, None, :]
    return jnp.concatenate([x1 * c - x2 * s, x2 * c + x1 * s], axis=-1).astype(x.dtype)


def _attend(q, k, v):
    s = jnp.einsum("bthd,blhd->bhtl", q, k).astype(jnp.float32) * ATTN_SCALE
    p = jax.nn.softmax(s, axis=-1)
    return jnp.einsum("bhtl,blhd->bthd", p.astype(v.dtype), v)


def _fourier_mixer(u):
    b, l, _ = u.shape
    ug = u.astype(jnp.float32).reshape(b, l, N_FOURIER_GROUPS, FOURIER_GROUP_DIM)
    yf = jnp.fft.fft2(ug, axes=(1, 3), norm="ortho").real
    return yf.reshape(b, l, FOURIER_DIM).astype(u.dtype)


def _mla_mixer(c_q, c_kv, k_r, cos, sin, q_norm_g, kv_norm_g, w_uq, w_ukv):
    b, l, _ = c_q.shape
    q = (_rmsnorm(c_q, q_norm_g) @ w_uq).reshape(b, l, N_HEADS, QK_HEAD_DIM)
    q_nope, q_rope = jnp.split(q, [QK_NOPE_DIM], axis=-1)
    kv = (_rmsnorm(c_kv, kv_norm_g) @ w_ukv).reshape(b, l, N_HEADS, QK_NOPE_DIM + V_HEAD_DIM)
    k_nope, v = jnp.split(kv, [QK_NOPE_DIM], axis=-1)
    k_rope = _apply_rope(k_r[:, :, None, :], cos, sin)
    q = jnp.concatenate([q_nope, _apply_rope(q_rope, cos, sin)], axis=-1)
    k = jnp.concatenate([k_nope, jnp.broadcast_to(k_rope, (b, l, N_HEADS, QK_ROPE_DIM))], axis=-1)
    meta_out = _attend(q[:, :N_META], k, v)
    s_real = l - N_META
    nb = s_real // Q_BLOCK
    qb = q[:, N_META:].reshape(b, nb, Q_BLOCK, N_HEADS, QK_HEAD_DIM).transpose(1, 0, 2, 3, 4)
    ob = lax.map(lambda qq: _attend(qq, k, v), qb)
    real_out = ob.transpose(1, 0, 2, 3, 4).reshape(b, s_real, ATTN_DIM)
    return jnp.concatenate([meta_out.reshape(b, N_META, ATTN_DIM), real_out], axis=1)


def _layer(x, cos, sin, norm1_g, w_in, q_norm_g, kv_norm_g, w_uq, w_ukv, w_fourier_out,
           w_attn_out, w_o, norm2_g, w_ffn_gate, w_ffn_up, w_ffn_down):
    h = _rmsnorm(x, norm1_g)
    z = h @ w_in
    u_f, c_q, c_kv, k_r, g_a, g_b = jnp.split(z, list(IN_SPLITS), axis=-1)
    y_a = _fourier_mixer(u_f) @ w_fourier_out
    y_b = _mla_mixer(c_q, c_kv, k_r, cos, sin, q_norm_g, kv_norm_g, w_uq, w_ukv) @ w_attn_out
    merged = jax.nn.sigmoid(g_a) * y_a + jax.nn.sigmoid(g_b) * y_b
    x = x + merged @ w_o
    h2 = _rmsnorm(x, norm2_g)
    return x + (jax.nn.silu(h2 @ w_ffn_gate) * (h2 @ w_ffn_up)) @ w_ffn_down


def _trunk(x, meta_tokens, norm1_g, w_in, q_norm_g, kv_norm_g, w_uq, w_ukv, w_fourier_out,
           w_attn_out, w_o, norm2_g, w_ffn_gate, w_ffn_up, w_ffn_down, final_norm_g):
    b = x.shape[0]
    meta = jnp.broadcast_to(meta_tokens.astype(x.dtype)[None], (b, N_META, D_MODEL))
    h = jnp.concatenate([meta, x], axis=1)
    cos, sin = _rope_tables(h.shape[1])
    for i in range(DEPTH):
        h = _layer(h, cos, sin, norm1_g[i], w_in[i], q_norm_g[i], kv_norm_g[i], w_uq[i], w_ukv[i],
                   w_fourier_out[i], w_attn_out[i], w_o[i], norm2_g[i], w_ffn_gate[i],
                   w_ffn_up[i], w_ffn_down[i])
    h = _rmsnorm(h, final_norm_g)
    return h[:, N_META:]


def setup_inputs(seed: int = 0) -> dict:
    key = jax.random.key(seed)
    ks = jax.random.split(key, 20)

    def w(k, shape, fan_in):
        return jax.random.normal(k, shape, jnp.float32) * (fan_in ** -0.5)

    def g(k, shape):
        return 1.0 + 0.02 * jax.random.normal(k, shape, jnp.float32)

    return {
        "x_prompt": jax.random.normal(ks[0], (BATCH, SEQ, D_MODEL), jnp.float32),
        "x_sample": jax.random.normal(ks[1], (DEC_BATCH, DEC_SEQ, D_MODEL), jnp.float32),
        "meta_tokens": jax.random.normal(ks[2], (N_META, D_MODEL), jnp.float32),
        "norm1_g": g(ks[3], (DEPTH, D_MODEL)),
        "w_in": w(ks[4], (DEPTH, D_MODEL, IN_PROJ_DIM), D_MODEL),
        "q_norm_g": g(ks[5], (DEPTH, Q_LORA_RANK)),
        "kv_norm_g": g(ks[6], (DEPTH, KV_LORA_RANK)),
        "w_uq": w(ks[7], (DEPTH, Q_LORA_RANK, N_HEADS * QK_HEAD_DIM), Q_LORA_RANK),
        "w_ukv": w(ks[8], (DEPTH, KV_LORA_RANK, N_HEADS * (QK_NOPE_DIM + V_HEAD_DIM)), KV_LORA_RANK),
        "w_fourier_out": w(ks[9], (DEPTH, FOURIER_DIM, D_MODEL), FOURIER_DIM),
        "w_attn_out": w(ks[10], (DEPTH, ATTN_DIM, D_MODEL), ATTN_DIM),
        "w_o": w(ks[11], (DEPTH, D_MODEL, D_MODEL), D_MODEL),
        "norm2_g": g(ks[12], (DEPTH, D_MODEL)),
        "w_ffn_gate": w(ks[13], (DEPTH, D_MODEL, D_FF), D_MODEL),
        "w_ffn_up": w(ks[14], (DEPTH, D_MODEL, D_FF), D_MODEL),
        "w_ffn_down": w(ks[15], (DEPTH, D_FF, D_MODEL), D_FF),
        "final_norm_g": g(ks[16], (D_MODEL,)),
    }


def reference(x_prompt, x_sample, meta_tokens, norm1_g, w_in, q_norm_g, kv_norm_g, w_uq, w_ukv,
              w_fourier_out, w_attn_out, w_o, norm2_g, w_ffn_gate, w_ffn_up, w_ffn_down,
              final_norm_g):
    y_prompt = _trunk(x_prompt, meta_tokens, norm1_g, w_in, q_norm_g, kv_norm_g, w_uq, w_ukv,
                      w_fourier_out, w_attn_out, w_o, norm2_g, w_ffn_gate, w_ffn_up, w_ffn_down,
                      final_norm_g)
    y_sample = _trunk(x_sample, meta_tokens, norm1_g, w_in, q_norm_g, kv_norm_g, w_uq, w_ukv,
                      w_fourier_out, w_attn_out, w_o, norm2_g, w_ffn_gate, w_ffn_up, w_ffn_down,
                      final_norm_g)
    return (y_prompt, y_sample)
```

```python
import functools
import math

import jax
import jax.numpy as jnp
import numpy as np
from jax import lax
from jax.experimental import pallas as pl
from jax.experimental.pallas import tpu as pltpu

D_MODEL = 1024
N_META = 16
N_GROUPS = 4
GROUP_DIM = 128
FOURIER_DIM = N_GROUPS * GROUP_DIM
N_HEADS = 8
NOPE_DIM = 128
ROPE_DIM = 64
QK_DIM = NOPE_DIM + ROPE_DIM
V_DIM = 128
Q_RANK = 512
KV_RANK = 256
ATTN_DIM = N_HEADS * V_DIM
D_FF = 2816
ROPE_THETA = 10000.0
NORM_EPS = 1e-6
ATTN_SCALE = QK_DIM ** -0.5
LOG2_E = math.log2(math.e)
DFT_RADIX = 16

LANES = 128
VMEM_LIMIT_BYTES = 56 * 1024 * 1024
PAIR_UNROLL = 4

_C_UF = 0
_C_CQ = _C_UF + FOURIER_DIM
_C_CKV = _C_CQ + Q_RANK
_C_GA = _C_CKV + KV_RANK
_C_GB = _C_GA + D_MODEL
_C_KR = _C_GB + D_MODEL
IN_COLS = _C_KR + LANES


def _rms(x, g):
    return x * lax.rsqrt(jnp.mean(x * x, axis=-1, keepdims=True) + NORM_EPS) * g


def _rope128(x, cos2, sin2):
    lane = lax.broadcasted_iota(jnp.int32, x.shape, 1)
    first_half = (lane % ROPE_DIM) < (ROPE_DIM // 2)
    partner = jnp.where(first_half, pltpu.roll(x, LANES - ROPE_DIM // 2, 1), pltpu.roll(x, ROPE_DIM // 2, 1))
    return x * cos2 + partner * sin2


def _inproj_kernel(x_ref, g1_ref, win_ref, qg_ref, kvg_ref, wuq_ref, wkt_ref, wv_ref, cos_ref, sin_ref,
                   uf_ref, q_ref, kt_ref, v_ref, sa_ref, sb_ref):
    bf = jnp.bfloat16
    f32 = jnp.float32
    h = _rms(x_ref[0], g1_ref[...]).astype(bf)

    def proj(c0, width):
        return jnp.dot(h, win_ref[:, c0:c0 + width], preferred_element_type=f32)

    uf_ref[0] = proj(_C_UF, FOURIER_DIM)
    sa_ref[0] = jax.nn.sigmoid(proj(_C_GA, D_MODEL)).astype(bf)
    sb_ref[0] = jax.nn.sigmoid(proj(_C_GB, D_MODEL)).astype(bf)

    cos2 = cos_ref[...]
    sin2 = sin_ref[...]

    cq = _rms(proj(_C_CQ, Q_RANK), qg_ref[...]).astype(bf)
    q = jnp.dot(cq, wuq_ref[...], preferred_element_type=f32) * (ATTN_SCALE * LOG2_E)
    rope0 = N_HEADS * NOPE_DIM
    for pair in range(N_HEADS // 2):
        qr = _rope128(q[:, rope0 + pair * LANES:rope0 + (pair + 1) * LANES], cos2, sin2).astype(bf)
        for sub in range(2):
            hd = 2 * pair + sub
            q_ref[0, hd, :, :NOPE_DIM] = q[:, hd * NOPE_DIM:(hd + 1) * NOPE_DIM].astype(bf)
            q_ref[0, hd, :, NOPE_DIM:] = qr[:, sub * ROPE_DIM:(sub + 1) * ROPE_DIM]

    ckv = _rms(proj(_C_CKV, KV_RANK), kvg_ref[...])
    vals = jnp.dot(ckv.astype(bf), wv_ref[...], preferred_element_type=f32)
    ckv_t = ckv.T.astype(bf)
    k_t = jnp.dot(wkt_ref[...], ckv_t, preferred_element_type=f32)
    kr_t = _rope128(proj(_C_KR, LANES), cos2, sin2).T[:ROPE_DIM].astype(bf)
    ones = jnp.ones((vals.shape[0], V_DIM), bf)
    for hd in range(N_HEADS):
        kt_ref[0, hd, 0, :NOPE_DIM, :] = k_t[hd * NOPE_DIM:(hd + 1) * NOPE_DIM].astype(bf)
        kt_ref[0, hd, 0, NOPE_DIM:, :] = kr_t
        v_ref[0, hd, :, :V_DIM] = vals[:, hd * V_DIM:(hd + 1) * V_DIM].astype(bf)
        v_ref[0, hd, :, V_DIM:] = ones


def _const_spec(shape):
    return pl.BlockSpec(shape, lambda *_: (0,) * len(shape), pipeline_mode=pl.Buffered(1))


def _inproj(x, g1, win, qg, kvg, wuq, wkt, wv, cos2, sin2, tm):
    b, s, _ = x.shape
    tok = lambda bi, si: (bi, si, 0)
    head = lambda bi, si: (bi, 0, si, 0)
    bf = jnp.bfloat16
    return pl.pallas_call(
        _inproj_kernel,
        grid=(b, s // tm),
        in_specs=[
            pl.BlockSpec((1, tm, D_MODEL), tok),
            _const_spec((1, D_MODEL)),
            _const_spec((D_MODEL, IN_COLS)),
            _const_spec((1, Q_RANK)),
            _const_spec((1, KV_RANK)),
            _const_spec((Q_RANK, N_HEADS * QK_DIM)),
            _const_spec((N_HEADS * NOPE_DIM, KV_RANK)),
            _const_spec((KV_RANK, N_HEADS * V_DIM)),
            pl.BlockSpec((tm, LANES), lambda bi, si: (si, 0)),
            pl.BlockSpec((tm, LANES), lambda bi, si: (si, 0)),
        ],
        out_specs=[
            pl.BlockSpec((1, tm, FOURIER_DIM), tok),
            pl.BlockSpec((1, N_HEADS, tm, QK_DIM), head),
            pl.BlockSpec((1, N_HEADS, 1, QK_DIM, tm), lambda bi, si: (bi, 0, si, 0, 0)),
            pl.BlockSpec((1, N_HEADS, tm, 2 * V_DIM), head),
            pl.BlockSpec((1, tm, D_MODEL), tok),
            pl.BlockSpec((1, tm, D_MODEL), tok),
        ],
        out_shape=[
            jax.ShapeDtypeStruct((b, s, FOURIER_DIM), jnp.float32),
            jax.ShapeDtypeStruct((b, N_HEADS, s, QK_DIM), bf),
            jax.ShapeDtypeStruct((b, N_HEADS, s // tm, QK_DIM, tm), bf),
            jax.ShapeDtypeStruct((b, N_HEADS, s, 2 * V_DIM), bf),
            jax.ShapeDtypeStruct((b, s, D_MODEL), bf),
            jax.ShapeDtypeStruct((b, s, D_MODEL), bf),
        ],
        compiler_params=pltpu.CompilerParams(
            dimension_semantics=("parallel", "parallel"), vmem_limit_bytes=VMEM_LIMIT_BYTES),
        name="inproj",
    )(x, g1, win, qg, kvg, wuq, wkt, wv, cos2, sin2)


def _cneg(v):
    return None if v is None else -v


def _cadd(a, b):
    if a is None:
        return b
    if b is None:
        return a
    return a + b


def _csub(a, b):
    if b is None:
        return a
    if a is None:
        return -b
    return a - b


def _fft(xs):
    n = len(xs)
    if n == 1:
        return xs
    ev = _fft(xs[0::2])
    od = _fft(xs[1::2])
    out = [None] * n
    for k in range(n // 2):
        re, im = od[k]
        if k == 0:
            tr, ti = re, im
        elif 4 * k == n:
            tr, ti = im, _cneg(re)
        else:
            c = math.cos(2 * math.pi * k / n)
            s = math.sin(2 * math.pi * k / n)
            tr = _cadd(None if re is None else re * c, None if im is None else im * s)
            ti = _csub(None if im is None else im * c, None if re is None else re * s)
        out[k] = (_cadd(ev[k][0], tr), _cadd(ev[k][1], ti))
        out[k + n // 2] = (_csub(ev[k][0], tr), _csub(ev[k][1], ti))
    return out


def _dft16_real(rows):
    out = _fft([(r, None) for r in rows])
    zero = jnp.zeros_like(rows[0])
    out = [(zero if re is None else re, zero if im is None else im) for re, im in out]
    for k in range(DFT_RADIX // 2 + 1, DFT_RADIX):
        re, im = out[DFT_RADIX - k]
        out[k] = (re, -im)
    return out


def _fourier_kernel(n2, xr_ref, xm_ref, twc_ref, tws_ref, dmat_ref, cs_ref, y_ref, x0_ref, z_ref, z0_ref):
    m = n2 - 1
    bf = jnp.bfloat16
    x0_ref[0:N_META, :] = xm_ref[...]
    x0_ref[N_META:N_META + m, :] = xr_ref[0, 0:m, :]

    def run_start(j1):
        return j1 * n2 - N_META

    dc = _dft16_real([x0_ref[0:1, :]] + [xr_ref[0, run_start(j1):run_start(j1) + 1, :] for j1 in range(1, DFT_RADIX)])
    for k1 in range(DFT_RADIX):
        z0_ref[k1:k1 + 1, 0:GROUP_DIM] = dc[k1][0]
        z0_ref[k1:k1 + 1, GROUP_DIM:] = dc[k1][1]

    def chunk(c, carry):
        r0 = pl.multiple_of(c * 8, 8)
        rows = [x0_ref[pl.ds(r0 + 1, 8), :]]
        rows += [xr_ref[0, pl.ds(r0 + (run_start(j1) + 1), 8), :] for j1 in range(1, DFT_RADIX)]
        a = _dft16_real(rows)
        for k1 in range(DFT_RADIX):
            ar, ai = a[k1]
            if k1 == 0:
                zr, zi = ar, ai
            else:
                tc = twc_ref[k1, pl.ds(r0, 8), :]
                ts = tws_ref[k1, pl.ds(r0, 8), :]
                zr = tc * ar + ts * ai
                zi = tc * ai - ts * ar
            z_ref[k1, pl.ds(r0, 8), 0:GROUP_DIM] = zr
            z_ref[k1, pl.ds(r0, 8), GROUP_DIM:] = zi
            z_ref[k1, pl.ds(r0 + m, 8), 0:GROUP_DIM] = zi
            z_ref[k1, pl.ds(r0 + m, 8), GROUP_DIM:] = -zr
        return carry

    lax.fori_loop(0, m // 8, chunk, 0)

    scale = 1.0 / math.sqrt(GROUP_DIM * DFT_RADIX * n2)
    for k1 in range(DFT_RADIX):
        p = jnp.dot(dmat_ref[...], z_ref[k1].astype(bf), preferred_element_type=jnp.float32)
        p = p + z0_ref[k1:k1 + 1, :]
        y = jnp.dot(p.astype(bf), cs_ref[...], preferred_element_type=jnp.float32) * scale
        y_ref[0, 0, :, k1 * GROUP_DIM:(k1 + 1) * GROUP_DIM] = y.astype(bf)


def _fourier(uf, uf_meta, twc, tws, dmat, cs):
    b, s, _ = uf.shape
    n2 = (s + N_META) // DFT_RADIX
    m = n2 - 1
    out = pl.pallas_call(
        functools.partial(_fourier_kernel, n2),
        grid=(b, N_GROUPS),
        in_specs=[
            pl.BlockSpec((1, s, GROUP_DIM), lambda bi, g: (bi, 0, g)),
            pl.BlockSpec((N_META, GROUP_DIM), lambda bi, g: (0, g)),
            pl.BlockSpec((DFT_RADIX, m, GROUP_DIM), lambda bi, g: (0, 0, 0)),
            pl.BlockSpec((DFT_RADIX, m, GROUP_DIM), lambda bi, g: (0, 0, 0)),
            pl.BlockSpec((m, 2 * m), lambda bi, g: (0, 0)),
            pl.BlockSpec((2 * GROUP_DIM, GROUP_DIM), lambda bi, g: (0, 0)),
        ],
        out_specs=pl.BlockSpec((1, 1, m, DFT_RADIX * GROUP_DIM), lambda bi, g: (bi, g, 0, 0)),
        out_shape=jax.ShapeDtypeStruct((b, N_GROUPS, m, DFT_RADIX * GROUP_DIM), jnp.bfloat16),
        scratch_shapes=[
            pltpu.VMEM((N_META + m, GROUP_DIM), jnp.float32),
            pltpu.VMEM((DFT_RADIX, 2 * m, 2 * GROUP_DIM), jnp.float32),
            pltpu.VMEM((DFT_RADIX, 2 * GROUP_DIM), jnp.float32),
        ],
        compiler_params=pltpu.CompilerParams(
            dimension_semantics=("parallel", "parallel"), vmem_limit_bytes=VMEM_LIMIT_BYTES),
        name="fourier",
    )(uf, uf_meta, twc, tws, dmat, cs)
    return out.reshape(b, N_GROUPS, s, GROUP_DIM)


def _fourier_tables(s):
    n2 = (s + N_META) // DFT_RADIX
    length = DFT_RADIX * n2
    j2 = np.arange(1, n2, dtype=np.int64)
    k1 = np.arange(DFT_RADIX, dtype=np.int64)
    phi = 2.0 * np.pi * ((k1[:, None] * j2[None, :]) % length) / length
    twc = jnp.broadcast_to(jnp.asarray(np.cos(phi), jnp.float32)[:, :, None], (DFT_RADIX, n2 - 1, GROUP_DIM))
    tws = jnp.broadcast_to(jnp.asarray(np.sin(phi), jnp.float32)[:, :, None], (DFT_RADIX, n2 - 1, GROUP_DIM))
    theta = 2.0 * np.pi * ((j2[:, None] * j2[None, :]) % n2) / n2
    dmat = jnp.asarray(np.concatenate([np.cos(theta), np.sin(theta)], axis=1), jnp.float32)
    c = np.arange(GROUP_DIM, dtype=np.int64)
    psi = 2.0 * np.pi * ((c[:, None] * c[None, :]) % GROUP_DIM) / GROUP_DIM
    cs = jnp.asarray(np.concatenate([np.cos(psi), np.sin(psi)], axis=0), jnp.float32)
    return twc, tws, dmat.astype(jnp.bfloat16), cs.astype(jnp.bfloat16)


def _attn_kernel(q_ref, kt_ref, v_ref, kmt_ref, vm_ref, o_ref, s_ref, pm_ref):
    bf = jnp.bfloat16
    f32 = jnp.float32
    tq = o_ref.shape[1]
    n, _, tk = kt_ref.shape[2:]
    qi = pl.program_id(2)
    row_cur = pl.multiple_of(qi * tq, tq)
    row_nxt = pl.multiple_of(jnp.minimum(qi + 1, pl.num_programs(2) - 1) * tq, tq)

    def produce(slot, row0, c):
        s = jnp.dot(q_ref[0, 0, pl.ds(row0, tq), :], kt_ref[0, 0, c], preferred_element_type=f32)
        s_ref[slot] = s
        pm_ref[slot] = functools.reduce(jnp.maximum, [s[:, j * LANES:(j + 1) * LANES] for j in range(tk // LANES)])

    def absorb(slot, c, m_i, acc):
        m_new = jnp.maximum(m_i, jnp.max(pm_ref[slot], axis=1, keepdims=True))
        p = jnp.exp2((s_ref[slot] - m_new).astype(bf))
        vals = v_ref[0, 0, pl.ds(pl.multiple_of(c * tk, tk), tk), :]
        return m_new, jnp.exp2(m_i - m_new) * acc + jnp.dot(p, vals, preferred_element_type=f32)

    @pl.when(qi == 0)
    def _():
        produce(0, row_cur, 0)

    lane = lax.broadcasted_iota(jnp.int32, (tq, LANES), 1)
    s_meta = jnp.dot(q_ref[0, 0, pl.ds(row_cur, tq), :], kmt_ref[0, 0], preferred_element_type=f32)
    s_meta = jnp.where(lane < N_META, s_meta, -jnp.inf)
    m = jnp.max(s_meta, axis=1, keepdims=True)
    acc = jnp.dot(jnp.exp2((s_meta - m).astype(bf)), vm_ref[0, 0], preferred_element_type=f32)

    def pair(i, carry):
        m_i, acc_i = carry
        c = 2 * i
        produce(1, row_cur, c + 1)
        m_i, acc_i = absorb(0, c, m_i, acc_i)
        last = i == n // 2 - 1
        produce(0, jnp.where(last, row_nxt, row_cur), jnp.where(last, 0, c + 2))
        return absorb(1, c + 1, m_i, acc_i)

    m, acc = lax.fori_loop(0, n // 2, pair, (m, acc), unroll=PAIR_UNROLL)
    o_ref[0] = (acc[:, :V_DIM] / acc[:, V_DIM:]).astype(bf)


def _attention(q, kt, v, kmt, vm, tq):
    b, nh, s, _ = q.shape
    n, _, tk = kt.shape[2:]
    assert n % 2 == 0 and n * tk == s
    return pl.pallas_call(
        _attn_kernel,
        grid=(b, nh, s // tq),
        in_specs=[
            pl.BlockSpec((1, 1, s, QK_DIM), lambda bi, h, qi: (bi, h, 0, 0)),
            pl.BlockSpec((1, 1, n, QK_DIM, tk), lambda bi, h, qi: (bi, h, 0, 0, 0)),
            pl.BlockSpec((1, 1, s, 2 * V_DIM), lambda bi, h, qi: (bi, h, 0, 0)),
            pl.BlockSpec((1, 1, QK_DIM, LANES), lambda bi, h, qi: (0, h, 0, 0)),
            pl.BlockSpec((1, 1, LANES, 2 * V_DIM), lambda bi, h, qi: (0, h, 0, 0)),
        ],
        out_specs=pl.BlockSpec((1, tq, V_DIM), lambda bi, h, qi: (bi, qi, h)),
        out_shape=jax.ShapeDtypeStruct((b, s, ATTN_DIM), jnp.bfloat16),
        scratch_shapes=[pltpu.VMEM((2, tq, tk), jnp.float32), pltpu.VMEM((2, tq, LANES), jnp.float32)],
        compiler_params=pltpu.CompilerParams(
            dimension_semantics=("parallel", "parallel", "arbitrary"), vmem_limit_bytes=VMEM_LIMIT_BYTES),
        name="attention",
    )(q, kt, v, kmt, vm)


def _tail_kernel(x_ref, y_ref, o_ref, sa_ref, sb_ref, wfo_ref, wao_ref, wo_ref, g2_ref, wg_ref, wu_ref, wd_ref,
                 gf_ref, out_ref):
    bf = jnp.bfloat16
    f32 = jnp.float32
    yf = jnp.concatenate([y_ref[0, g] for g in range(N_GROUPS)], axis=1)
    ya = jnp.dot(yf, wfo_ref[...], preferred_element_type=f32)
    yb = jnp.dot(o_ref[0], wao_ref[...], preferred_element_type=f32)
    merged = sa_ref[0].astype(f32) * ya + sb_ref[0].astype(f32) * yb
    x1 = x_ref[0] + jnp.dot(merged.astype(bf), wo_ref[...], preferred_element_type=f32)
    h2 = _rms(x1, g2_ref[...]).astype(bf)
    gate = jnp.dot(h2, wg_ref[...], preferred_element_type=f32)
    up = jnp.dot(h2, wu_ref[...], preferred_element_type=f32)
    act = (gate * jax.nn.sigmoid(gate) * up).astype(bf)
    x2 = x1 + jnp.dot(act, wd_ref[...], preferred_element_type=f32)
    out_ref[0] = _rms(x2, gf_ref[...])


def _tail(x, y, o, sa, sb, wfo, wao, wo, g2, wg, wu, wd, gf, tm):
    b, s, _ = x.shape
    tok = lambda bi, si: (bi, si, 0)
    const = lambda bi, si: (0, 0)

    def weight(shape):
        return pl.BlockSpec(shape, const, pipeline_mode=pl.Buffered(1))

    return pl.pallas_call(
        _tail_kernel,
        grid=(b, s // tm),
        in_specs=[
            pl.BlockSpec((1, tm, D_MODEL), tok),
            pl.BlockSpec((1, N_GROUPS, tm, GROUP_DIM), lambda bi, si: (bi, 0, si, 0)),
            pl.BlockSpec((1, tm, ATTN_DIM), tok),
            pl.BlockSpec((1, tm, D_MODEL), tok),
            pl.BlockSpec((1, tm, D_MODEL), tok),
            weight((FOURIER_DIM, D_MODEL)),
            weight((ATTN_DIM, D_MODEL)),
            weight((D_MODEL, D_MODEL)),
            pl.BlockSpec((1, D_MODEL), const),
            weight((D_MODEL, D_FF)),
            weight((D_MODEL, D_FF)),
            weight((D_FF, D_MODEL)),
            pl.BlockSpec((1, D_MODEL), const),
        ],
        out_specs=pl.BlockSpec((1, tm, D_MODEL), tok),
        out_shape=jax.ShapeDtypeStruct((b, s, D_MODEL), jnp.float32),
        compiler_params=pltpu.CompilerParams(
            dimension_semantics=("parallel", "parallel"), vmem_limit_bytes=VMEM_LIMIT_BYTES),
        name="tail",
    )(x, y, o, sa, sb, wfo, wao, wo, g2, wg, wu, wd, gf)


def _rope_tables(length):
    inv = 1.0 / (ROPE_THETA ** (jnp.arange(0, ROPE_DIM, 2, dtype=jnp.float32) / ROPE_DIM))
    ang = jnp.arange(length, dtype=jnp.float32)[:, None] * inv[None, :]
    c, s = jnp.cos(ang), jnp.sin(ang)
    return jnp.concatenate([c, c, c, c], axis=1), jnp.concatenate([-s, s, -s, s], axis=1)


def _prepare_weights(norm1_g, w_in, q_norm_g, kv_norm_g, w_uq, w_ukv, w_fourier_out, w_attn_out, w_o, norm2_g,
                     w_ffn_gate, w_ffn_up, w_ffn_down, final_norm_g):
    bf = jnp.bfloat16
    w = w_in[0]
    s_uf, s_cq, s_ckv, s_kr = FOURIER_DIM, FOURIER_DIM + Q_RANK, FOURIER_DIM + Q_RANK + KV_RANK, \
        FOURIER_DIM + Q_RANK + KV_RANK + ROPE_DIM
    win = jnp.concatenate(
        [w[:, :s_ckv], w[:, s_kr:], w[:, s_ckv:s_kr], jnp.zeros((D_MODEL, LANES - ROPE_DIM), w.dtype)], axis=1).astype(bf)
    wq = w_uq[0].reshape(Q_RANK, N_HEADS, QK_DIM)
    wuq = jnp.concatenate([wq[:, :, :NOPE_DIM].reshape(Q_RANK, -1), wq[:, :, NOPE_DIM:].reshape(Q_RANK, -1)], axis=1).astype(bf)
    wkv = w_ukv[0].reshape(KV_RANK, N_HEADS, NOPE_DIM + V_DIM)
    wkt = wkv[:, :, :NOPE_DIM].reshape(KV_RANK, -1).T.astype(bf)
    wv = wkv[:, :, NOPE_DIM:].reshape(KV_RANK, -1).astype(bf)
    row = lambda g: g.reshape(1, -1).astype(jnp.float32)
    return dict(
        g1=row(norm1_g[0]), win=win, qg=row(q_norm_g[0]), kvg=row(kv_norm_g[0]), wuq=wuq, wkt=wkt, wv=wv,
        wfo=w_fourier_out[0].astype(bf), wao=w_attn_out[0].astype(bf), wo=w_o[0].astype(bf), g2=row(norm2_g[0]),
        wg=w_ffn_gate[0].astype(bf), wu=w_ffn_up[0].astype(bf), wd=w_ffn_down[0].astype(bf), gf=row(final_norm_g))


def _trunk(x, meta, p, tk, tq, tm_tail):
    b, s, _ = x.shape
    cos2, sin2 = _rope_tables(s + N_META)
    proj = functools.partial(_inproj, g1=p["g1"], win=p["win"], qg=p["qg"], kvg=p["kvg"], wuq=p["wuq"], wkt=p["wkt"],
                             wv=p["wv"])
    meta_pad = jnp.zeros((1, LANES, D_MODEL), meta.dtype).at[0, :N_META].set(meta)
    uf_m, _, kt_m, v_m, _, _ = proj(meta_pad, cos2=cos2[:LANES], sin2=sin2[:LANES], tm=LANES)
    uf, q, kt, v, sa, sb = proj(x, cos2=cos2[N_META:], sin2=sin2[N_META:], tm=tk)
    y = _fourier(uf, uf_m[0, :N_META], *_fourier_tables(s))
    o = _attention(q, kt, v, kt_m[:, :, 0], v_m, tq)
    return _tail(x, y, o, sa, sb, p["wfo"], p["wao"], p["wo"], p["g2"], p["wg"], p["wu"], p["wd"], p["gf"], tm_tail)


def kernel(x_prompt, x_sample, meta_tokens, norm1_g, w_in, q_norm_g, kv_norm_g, w_uq, w_ukv, w_fourier_out,
           w_attn_out, w_o, norm2_g, w_ffn_gate, w_ffn_up, w_ffn_down, final_norm_g):
    p = _prepare_weights(norm1_g, w_in, q_norm_g, kv_norm_g, w_uq, w_ukv, w_fourier_out, w_attn_out, w_o, norm2_g,
                         w_ffn_gate, w_ffn_up, w_ffn_down, final_norm_g)
    cfg = dict(tk=512, tq=512, tm_tail=256)
    y_prompt = _trunk(x_prompt, meta_tokens, p, **cfg)
    y_sample = _trunk(x_sample, meta_tokens, p, **cfg)
    return (y_prompt, y_sample)
```

```python
import functools
import math

import jax
import jax.numpy as jnp
import numpy as np
from jax import lax
from jax.experimental import pallas as pl
from jax.experimental.pallas import tpu as pltpu

D_MODEL = 1024
N_META = 16
N_GROUPS = 4
GROUP_DIM = 128
FOURIER_DIM = N_GROUPS * GROUP_DIM
N_HEADS = 8
NOPE_DIM = 128
ROPE_DIM = 64
QK_DIM = NOPE_DIM + ROPE_DIM
V_DIM = 128
Q_RANK = 512
KV_RANK = 256
ATTN_DIM = N_HEADS * V_DIM
D_FF = 2816
ROPE_THETA = 10000.0
NORM_EPS = 1e-6
ATTN_SCALE = QK_DIM ** -0.5
LOG2_E = math.log2(math.e)
DFT_RADIX = 16

LANES = 128
VMEM_LIMIT_BYTES = 56 * 1024 * 1024
PAIR_UNROLL = 4

_C_UF = 0
_C_CQ = _C_UF + FOURIER_DIM
_C_CKV = _C_CQ + Q_RANK
_C_GA = _C_CKV + KV_RANK
_C_GB = _C_GA + D_MODEL
_C_KR = _C_GB + D_MODEL
IN_COLS = _C_KR + LANES


def _rms(x, g):
    return x * lax.rsqrt(jnp.mean(x * x, axis=-1, keepdims=True) + NORM_EPS) * g


def _rope128(x, cos2, sin2):
    lane = lax.broadcasted_iota(jnp.int32, x.shape, 1)
    first_half = (lane % ROPE_DIM) < (ROPE_DIM // 2)
    partner = jnp.where(first_half, pltpu.roll(x, LANES - ROPE_DIM // 2, 1), pltpu.roll(x, ROPE_DIM // 2, 1))
    return x * cos2 + partner * sin2


def _inproj_kernel(x_ref, g1_ref, win_ref, qg_ref, kvg_ref, wuq_ref, wkt_ref, wv_ref, cos_ref, sin_ref,
                   uf_ref, q_ref, kt_ref, v_ref, sa_ref, sb_ref):
    bf = jnp.bfloat16
    f32 = jnp.float32
    h = _rms(x_ref[0], g1_ref[...]).astype(bf)

    def proj(c0, width):
        return jnp.dot(h, win_ref[:, c0:c0 + width], preferred_element_type=f32)

    uf_ref[0] = proj(_C_UF, FOURIER_DIM)
    sa_ref[0] = jax.nn.sigmoid(proj(_C_GA, D_MODEL)).astype(bf)
    sb_ref[0] = jax.nn.sigmoid(proj(_C_GB, D_MODEL)).astype(bf)

    cos2 = cos_ref[...]
    sin2 = sin_ref[...]

    cq = _rms(proj(_C_CQ, Q_RANK), qg_ref[...]).astype(bf)
    q = jnp.dot(cq, wuq_ref[...], preferred_element_type=f32) * (ATTN_SCALE * LOG2_E)
    rope0 = N_HEADS * NOPE_DIM
    for pair in range(N_HEADS // 2):
        qr = _rope128(q[:, rope0 + pair * LANES:rope0 + (pair + 1) * LANES], cos2, sin2).astype(bf)
        for sub in range(2):
            hd = 2 * pair + sub
            q_ref[0, hd, :, :NOPE_DIM] = q[:, hd * NOPE_DIM:(hd + 1) * NOPE_DIM].astype(bf)
            q_ref[0, hd, :, NOPE_DIM:] = qr[:, sub * ROPE_DIM:(sub + 1) * ROPE_DIM]

    ckv = _rms(proj(_C_CKV, KV_RANK), kvg_ref[...])
    vals = jnp.dot(ckv.astype(bf), wv_ref[...], preferred_element_type=f32)
    ckv_t = ckv.T.astype(bf)
    k_t = jnp.dot(wkt_ref[...], ckv_t, preferred_element_type=f32)
    kr_t = _rope128(proj(_C_KR, LANES), cos2, sin2).T[:ROPE_DIM].astype(bf)
    ones = jnp.ones((vals.shape[0], V_DIM), bf)
    for hd in range(N_HEADS):
        kt_ref[0, hd, 0, :NOPE_DIM, :] = k_t[hd * NOPE_DIM:(hd + 1) * NOPE_DIM].astype(bf)
        kt_ref[0, hd, 0, NOPE_DIM:, :] = kr_t
        v_ref[0, hd, :, :V_DIM] = vals[:, hd * V_DIM:(hd + 1) * V_DIM].astype(bf)
        v_ref[0, hd, :, V_DIM:] = ones


def _const_spec(shape):
    return pl.BlockSpec(shape, lambda *_: (0,) * len(shape), pipeline_mode=pl.Buffered(1))


def _inproj(x, g1, win, qg, kvg, wuq, wkt, wv, cos2, sin2, tm):
    b, s, _ = x.shape
    tok = lambda bi, si: (bi, si, 0)
    head = lambda bi, si: (bi, 0, si, 0)
    bf = jnp.bfloat16
    return pl.pallas_call(
        _inproj_kernel,
        grid=(b, s // tm),
        in_specs=[
            pl.BlockSpec((1, tm, D_MODEL), tok),
            _const_spec((1, D_MODEL)),
            _const_spec((D_MODEL, IN_COLS)),
            _const_spec((1, Q_RANK)),
            _const_spec((1, KV_RANK)),
            _const_spec((Q_RANK, N_HEADS * QK_DIM)),
            _const_spec((N_HEADS * NOPE_DIM, KV_RANK)),
            _const_spec((KV_RANK, N_HEADS * V_DIM)),
            pl.BlockSpec((tm, LANES), lambda bi, si: (si, 0)),
            pl.BlockSpec((tm, LANES), lambda bi, si: (si, 0)),
        ],
        out_specs=[
            pl.BlockSpec((1, tm, FOURIER_DIM), tok),
            pl.BlockSpec((1, N_HEADS, tm, QK_DIM), head),
            pl.BlockSpec((1, N_HEADS, 1, QK_DIM, tm), lambda bi, si: (bi, 0, si, 0, 0)),
            pl.BlockSpec((1, N_HEADS, tm, 2 * V_DIM), head),
            pl.BlockSpec((1, tm, D_MODEL), tok),
            pl.BlockSpec((1, tm, D_MODEL), tok),
        ],
        out_shape=[
            jax.ShapeDtypeStruct((b, s, FOURIER_DIM), jnp.float32),
            jax.ShapeDtypeStruct((b, N_HEADS, s, QK_DIM), bf),
            jax.ShapeDtypeStruct((b, N_HEADS, s // tm, QK_DIM, tm), bf),
            jax.ShapeDtypeStruct((b, N_HEADS, s, 2 * V_DIM), bf),
            jax.ShapeDtypeStruct((b, s, D_MODEL), bf),
            jax.ShapeDtypeStruct((b, s, D_MODEL), bf),
        ],
        compiler_params=pltpu.CompilerParams(
            dimension_semantics=("parallel", "parallel"), vmem_limit_bytes=VMEM_LIMIT_BYTES),
        name="inproj",
    )(x, g1, win, qg, kvg, wuq, wkt, wv, cos2, sin2)


def _cneg(v):
    return None if v is None else -v


def _cadd(a, b):
    if a is None:
        return b
    if b is None:
        return a
    return a + b


def _csub(a, b):
    if b is None:
        return a
    if a is None:
        return -b
    return a - b


def _fft(xs):
    n = len(xs)
    if n == 1:
        return xs
    ev = _fft(xs[0::2])
    od = _fft(xs[1::2])
    out = [None] * n
    for k in range(n // 2):
        re, im = od[k]
        if k == 0:
            tr, ti = re, im
        elif 4 * k == n:
            tr, ti = im, _cneg(re)
        else:
            c = math.cos(2 * math.pi * k / n)
            s = math.sin(2 * math.pi * k / n)
            tr = _cadd(None if re is None else re * c, None if im is None else im * s)
            ti = _csub(None if im is None else im * c, None if re is None else re * s)
        out[k] = (_cadd(ev[k][0], tr), _cadd(ev[k][1], ti))
        out[k + n // 2] = (_csub(ev[k][0], tr), _csub(ev[k][1], ti))
    return out


def _dft16_real(rows):
    out = _fft([(r, None) for r in rows])
    zero = jnp.zeros_like(rows[0])
    out = [(zero if re is None else re, zero if im is None else im) for re, im in out]
    for k in range(DFT_RADIX // 2 + 1, DFT_RADIX):
        re, im = out[DFT_RADIX - k]
        out[k] = (re, -im)
    return out


def _fourier_kernel(n2, xr_ref, xm_ref, twc_ref, tws_ref, dmat_ref, cs_ref, y_ref, x0_ref, z_ref, z0_ref):
    m = n2 - 1
    bf = jnp.bfloat16
    x0_ref[0:N_META, :] = xm_ref[...]
    x0_ref[N_META:N_META + m, :] = xr_ref[0, 0:m, :]

    def run_start(j1):
        return j1 * n2 - N_META

    dc = _dft16_real([x0_ref[0:1, :]] + [xr_ref[0, run_start(j1):run_start(j1) + 1, :] for j1 in range(1, DFT_RADIX)])
    for k1 in range(DFT_RADIX):
        z0_ref[k1:k1 + 1, 0:GROUP_DIM] = dc[k1][0]
        z0_ref[k1:k1 + 1, GROUP_DIM:] = dc[k1][1]

    def chunk(c, carry):
        r0 = pl.multiple_of(c * 8, 8)
        rows = [x0_ref[pl.ds(r0 + 1, 8), :]]
        rows += [xr_ref[0, pl.ds(r0 + (run_start(j1) + 1), 8), :] for j1 in range(1, DFT_RADIX)]
        a = _dft16_real(rows)
        for k1 in range(DFT_RADIX):
            ar, ai = a[k1]
            if k1 == 0:
                zr, zi = ar, ai
            else:
                tc = twc_ref[k1, pl.ds(r0, 8), :]
                ts = tws_ref[k1, pl.ds(r0, 8), :]
                zr = tc * ar + ts * ai
                zi = tc * ai - ts * ar
            z_ref[k1, pl.ds(r0, 8), 0:GROUP_DIM] = zr
            z_ref[k1, pl.ds(r0, 8), GROUP_DIM:] = zi
            z_ref[k1, pl.ds(r0 + m, 8), 0:GROUP_DIM] = zi
            z_ref[k1, pl.ds(r0 + m, 8), GROUP_DIM:] = -zr
        return carry

    lax.fori_loop(0, m // 8, chunk, 0)

    scale = 1.0 / math.sqrt(GROUP_DIM * DFT_RADIX * n2)
    for k1 in range(DFT_RADIX):
        p = jnp.dot(dmat_ref[...], z_ref[k1].astype(bf), preferred_element_type=jnp.float32)
        p = p + z0_ref[k1:k1 + 1, :]
        y = jnp.dot(p.astype(bf), cs_ref[...], preferred_element_type=jnp.float32) * scale
        y_ref[0, 0, :, k1 * GROUP_DIM:(k1 + 1) * GROUP_DIM] = y.astype(bf)


def _fourier(uf, uf_meta, twc, tws, dmat, cs):
    b, s, _ = uf.shape
    n2 = (s + N_META) // DFT_RADIX
    m = n2 - 1
    out = pl.pallas_call(
        functools.partial(_fourier_kernel, n2),
        grid=(b, N_GROUPS),
        in_specs=[
            pl.BlockSpec((1, s, GROUP_DIM), lambda bi, g: (bi, 0, g)),
            pl.BlockSpec((N_META, GROUP_DIM), lambda bi, g: (0, g)),
            pl.BlockSpec((DFT_RADIX, m, GROUP_DIM), lambda bi, g: (0, 0, 0)),
            pl.BlockSpec((DFT_RADIX, m, GROUP_DIM), lambda bi, g: (0, 0, 0)),
            pl.BlockSpec((m, 2 * m), lambda bi, g: (0, 0)),
            pl.BlockSpec((2 * GROUP_DIM, GROUP_DIM), lambda bi, g: (0, 0)),
        ],
        out_specs=pl.BlockSpec((1, 1, m, DFT_RADIX * GROUP_DIM), lambda bi, g: (bi, g, 0, 0)),
        out_shape=jax.ShapeDtypeStruct((b, N_GROUPS, m, DFT_RADIX * GROUP_DIM), jnp.bfloat16),
        scratch_shapes=[
            pltpu.VMEM((N_META + m, GROUP_DIM), jnp.float32),
            pltpu.VMEM((DFT_RADIX, 2 * m, 2 * GROUP_DIM), jnp.float32),
            pltpu.VMEM((DFT_RADIX, 2 * GROUP_DIM), jnp.float32),
        ],
        compiler_params=pltpu.CompilerParams(
            dimension_semantics=("parallel", "parallel"), vmem_limit_bytes=VMEM_LIMIT_BYTES),
        name="fourier",
    )(uf, uf_meta, twc, tws, dmat, cs)
    return out.reshape(b, N_GROUPS, s, GROUP_DIM)


def _fourier_tables(s):
    n2 = (s + N_META) // DFT_RADIX
    length = DFT_RADIX * n2
    j2 = np.arange(1, n2, dtype=np.int64)
    k1 = np.arange(DFT_RADIX, dtype=np.int64)
    phi = 2.0 * np.pi * ((k1[:, None] * j2[None, :]) % length) / length
    twc = jnp.broadcast_to(jnp.asarray(np.cos(phi), jnp.float32)[:, :, None], (DFT_RADIX, n2 - 1, GROUP_DIM))
    tws = jnp.broadcast_to(jnp.asarray(np.sin(phi), jnp.float32)[:, :, None], (DFT_RADIX, n2 - 1, GROUP_DIM))
    theta = 2.0 * np.pi * ((j2[:, None] * j2[None, :]) % n2) / n2
    dmat = jnp.asarray(np.concatenate([np.cos(theta), np.sin(theta)], axis=1), jnp.float32)
    c = np.arange(GROUP_DIM, dtype=np.int64)
    psi = 2.0 * np.pi * ((c[:, None] * c[None, :]) % GROUP_DIM) / GROUP_DIM
    cs = jnp.asarray(np.concatenate([np.cos(psi), np.sin(psi)], axis=0), jnp.float32)
    return twc, tws, dmat.astype(jnp.bfloat16), cs.astype(jnp.bfloat16)


def _attn_kernel(q_ref, kt_ref, v_ref, kmt_ref, vm_ref, o_ref, s_ref, pm_ref):
    bf = jnp.bfloat16
    f32 = jnp.float32
    tq = o_ref.shape[1]
    n, _, tk = kt_ref.shape[2:]
    qi = pl.program_id(2)
    row_cur = pl.multiple_of(qi * tq, tq)
    row_nxt = pl.multiple_of(jnp.minimum(qi + 1, pl.num_programs(2) - 1) * tq, tq)

    def produce(slot, row0, c):
        s = jnp.dot(q_ref[0, 0, pl.ds(row0, tq), :], kt_ref[0, 0, c], preferred_element_type=f32)
        s_ref[slot] = s
        pm_ref[slot] = functools.reduce(jnp.maximum, [s[:, j * LANES:(j + 1) * LANES] for j in range(tk // LANES)])

    def absorb(slot, c, m_i, acc):
        m_new = jnp.maximum(m_i, jnp.max(pm_ref[slot], axis=1, keepdims=True))
        p = jnp.exp2((s_ref[slot] - m_new).astype(bf))
        vals = v_ref[0, 0, pl.ds(pl.multiple_of(c * tk, tk), tk), :]
        return m_new, jnp.exp2(m_i - m_new) * acc + jnp.dot(p, vals, preferred_element_type=f32)

    @pl.when(qi == 0)
    def _():
        produce(0, row_cur, 0)

    lane = lax.broadcasted_iota(jnp.int32, (tq, LANES), 1)
    s_meta = jnp.dot(q_ref[0, 0, pl.ds(row_cur, tq), :], kmt_ref[0, 0], preferred_element_type=f32)
    s_meta = jnp.where(lane < N_META, s_meta, -jnp.inf)
    m = jnp.max(s_meta, axis=1, keepdims=True)
    acc = jnp.dot(jnp.exp2((s_meta - m).astype(bf)), vm_ref[0, 0], preferred_element_type=f32)

    def pair(i, carry):
        m_i, acc_i = carry
        c = 2 * i
        produce(1, row_cur, c + 1)
        m_i, acc_i = absorb(0, c, m_i, acc_i)
        last = i == n // 2 - 1
        produce(0, jnp.where(last, row_nxt, row_cur), jnp.where(last, 0, c + 2))
        return absorb(1, c + 1, m_i, acc_i)

    m, acc = lax.fori_loop(0, n // 2, pair, (m, acc), unroll=PAIR_UNROLL)
    o_ref[0] = (acc[:, :V_DIM] / acc[:, V_DIM:]).astype(bf)


def _attention(q, kt, v, kmt, vm, tq):
    b, nh, s, _ = q.shape
    n, _, tk = kt.shape[2:]
    assert n % 2 == 0 and n * tk == s
    return pl.pallas_call(
        _attn_kernel,
        grid=(b, nh, s // tq),
        in_specs=[
            pl.BlockSpec((1, 1, s, QK_DIM), lambda bi, h, qi: (bi, h, 0, 0)),
            pl.BlockSpec((1, 1, n, QK_DIM, tk), lambda bi, h, qi: (bi, h, 0, 0, 0)),
            pl.BlockSpec((1, 1, s, 2 * V_DIM), lambda bi, h, qi: (bi, h, 0, 0)),
            pl.BlockSpec((1, 1, QK_DIM, LANES), lambda bi, h, qi: (0, h, 0, 0)),
            pl.BlockSpec((1, 1, LANES, 2 * V_DIM), lambda bi, h, qi: (0, h, 0, 0)),
        ],
        out_specs=pl.BlockSpec((1, tq, V_DIM), lambda bi, h, qi: (bi, qi, h)),
        out_shape=jax.ShapeDtypeStruct((b, s, ATTN_DIM), jnp.bfloat16),
        scratch_shapes=[pltpu.VMEM((2, tq, tk), jnp.float32), pltpu.VMEM((2, tq, LANES), jnp.float32)],
        compiler_params=pltpu.CompilerParams(
            dimension_semantics=("parallel", "parallel", "arbitrary"), vmem_limit_bytes=VMEM_LIMIT_BYTES),
        name="attention",
    )(q, kt, v, kmt, vm)


def _tail_kernel(x_ref, y_ref, o_ref, sa_ref, sb_ref, wfo_ref, wao_ref, wo_ref, g2_ref, wg_ref, wu_ref, wd_ref,
                 gf_ref, out_ref):
    bf = jnp.bfloat16
    f32 = jnp.float32
    yf = jnp.concatenate([y_ref[0, g] for g in range(N_GROUPS)], axis=1)
    ya = jnp.dot(yf, wfo_ref[...], preferred_element_type=f32)
    yb = jnp.dot(o_ref[0], wao_ref[...], preferred_element_type=f32)
    merged = sa_ref[0].astype(f32) * ya + sb_ref[0].astype(f32) * yb
    x1 = x_ref[0] + jnp.dot(merged.astype(bf), wo_ref[...], preferred_element_type=f32)
    h2 = _rms(x1, g2_ref[...]).astype(bf)
    gate = jnp.dot(h2, wg_ref[...], preferred_element_type=f32)
    up = jnp.dot(h2, wu_ref[...], preferred_element_type=f32)
    act = (gate * jax.nn.sigmoid(gate) * up).astype(bf)
    x2 = x1 + jnp.dot(act, wd_ref[...], preferred_element_type=f32)
    out_ref[0] = _rms(x2, gf_ref[...])


def _tail(x, y, o, sa, sb, wfo, wao, wo, g2, wg, wu, wd, gf, tm):
    b, s, _ = x.shape
    tok = lambda bi, si: (bi, si, 0)
    const = lambda bi, si: (0, 0)

    def weight(shape):
        return pl.BlockSpec(shape, const, pipeline_mode=pl.Buffered(1))

    return pl.pallas_call(
        _tail_kernel,
        grid=(b, s // tm),
        in_specs=[
            pl.BlockSpec((1, tm, D_MODEL), tok),
            pl.BlockSpec((1, N_GROUPS, tm, GROUP_DIM), lambda bi, si: (bi, 0, si, 0)),
            pl.BlockSpec((1, tm, ATTN_DIM), tok),
            pl.BlockSpec((1, tm, D_MODEL), tok),
            pl.BlockSpec((1, tm, D_MODEL), tok),
            weight((FOURIER_DIM, D_MODEL)),
            weight((ATTN_DIM, D_MODEL)),
            weight((D_MODEL, D_MODEL)),
            pl.BlockSpec((1, D_MODEL), const),
            weight((D_MODEL, D_FF)),
            weight((D_MODEL, D_FF)),
            weight((D_FF, D_MODEL)),
            pl.BlockSpec((1, D_MODEL), const),
        ],
        out_specs=pl.BlockSpec((1, tm, D_MODEL), tok),
        out_shape=jax.ShapeDtypeStruct((b, s, D_MODEL), jnp.float32),
        compiler_params=pltpu.CompilerParams(
            dimension_semantics=("parallel", "parallel"), vmem_limit_bytes=VMEM_LIMIT_BYTES),
        name="tail",
    )(x, y, o, sa, sb, wfo, wao, wo, g2, wg, wu, wd, gf)


def _rope_tables(length):
    inv = 1.0 / (ROPE_THETA ** (jnp.arange(0, ROPE_DIM, 2, dtype=jnp.float32) / ROPE_DIM))
    ang = jnp.arange(length, dtype=jnp.float32)[:, None] * inv[None, :]
    c, s = jnp.cos(ang), jnp.sin(ang)
    return jnp.concatenate([c, c, c, c], axis=1), jnp.concatenate([-s, s, -s, s], axis=1)


def _prepare_weights(norm1_g, w_in, q_norm_g, kv_norm_g, w_uq, w_ukv, w_fourier_out, w_attn_out, w_o, norm2_g,
                     w_ffn_gate, w_ffn_up, w_ffn_down, final_norm_g):
    bf = jnp.bfloat16
    w = w_in[0]
    s_uf, s_cq, s_ckv, s_kr = FOURIER_DIM, FOURIER_DIM + Q_RANK, FOURIER_DIM + Q_RANK + KV_RANK, \
        FOURIER_DIM + Q_RANK + KV_RANK + ROPE_DIM
    win = jnp.concatenate(
        [w[:, :s_ckv], w[:, s_kr:], w[:, s_ckv:s_kr], jnp.zeros((D_MODEL, LANES - ROPE_DIM), w.dtype)], axis=1).astype(bf)
    wq = w_uq[0].reshape(Q_RANK, N_HEADS, QK_DIM)
    wuq = jnp.concatenate([wq[:, :, :NOPE_DIM].reshape(Q_RANK, -1), wq[:, :, NOPE_DIM:].reshape(Q_RANK, -1)], axis=1).astype(bf)
    wkv = w_ukv[0].reshape(KV_RANK, N_HEADS, NOPE_DIM + V_DIM)
    wkt = wkv[:, :, :NOPE_DIM].reshape(KV_RANK, -1).T.astype(bf)
    wv = wkv[:, :, NOPE_DIM:].reshape(KV_RANK, -1).astype(bf)
    row = lambda g: g.reshape(1, -1).astype(jnp.float32)
    return dict(
        g1=row(norm1_g[0]), win=win, qg=row(q_norm_g[0]), kvg=row(kv_norm_g[0]), wuq=wuq, wkt=wkt, wv=wv,
        wfo=w_fourier_out[0].astype(bf), wao=w_attn_out[0].astype(bf), wo=w_o[0].astype(bf), g2=row(norm2_g[0]),
        wg=w_ffn_gate[0].astype(bf), wu=w_ffn_up[0].astype(bf), wd=w_ffn_down[0].astype(bf), gf=row(final_norm_g))


def _trunk(x, meta, p, tk, tq, tm_tail):
    b, s, _ = x.shape
    cos2, sin2 = _rope_tables(s + N_META)
    proj = functools.partial(_inproj, g1=p["g1"], win=p["win"], qg=p["qg"], kvg=p["kvg"], wuq=p["wuq"], wkt=p["wkt"],
                             wv=p["wv"])
    meta_pad = jnp.zeros((1, LANES, D_MODEL), meta.dtype).at[0, :N_META].set(meta)
    uf_m, _, kt_m, v_m, _, _ = proj(meta_pad, cos2=cos2[:LANES], sin2=sin2[:LANES], tm=LANES)
    uf, q, kt, v, sa, sb = proj(x, cos2=cos2[N_META:], sin2=sin2[N_META:], tm=tk)
    y = _fourier(uf, uf_m[0, :N_META], *_fourier_tables(s))
    o = _attention(q, kt, v, kt_m[:, :, 0], v_m, tq)
    return _tail(x, y, o, sa, sb, p["wfo"], p["wao"], p["wo"], p["g2"], p["wg"], p["wu"], p["wd"], p["gf"], tm_tail)


def kernel(x_prompt, x_sample, meta_tokens, norm1_g, w_in, q_norm_g, kv_norm_g, w_uq, w_ukv, w_fourier_out,
           w_attn_out, w_o, norm2_g, w_ffn_gate, w_ffn_up, w_ffn_down, final_norm_g):
    p = _prepare_weights(norm1_g, w_in, q_norm_g, kv_norm_g, w_uq, w_ukv, w_fourier_out, w_attn_out, w_o, norm2_g,
                         w_ffn_gate, w_ffn_up, w_ffn_down, final_norm_g)
    cfg = dict(tk=512, tq=1024, tm_tail=256)
    y_prompt = _trunk(x_prompt, meta_tokens, p, **cfg)
    y_sample = _trunk(x_sample, meta_tokens, p, **cfg)
    return (y_prompt, y_sample)
```

```python
import functools
import math

import jax
import jax.numpy as jnp
import numpy as np
from jax import lax
from jax.experimental import pallas as pl
from jax.experimental.pallas import tpu as pltpu

D_MODEL = 1024
N_META = 16
N_GROUPS = 4
GROUP_DIM = 128
FOURIER_DIM = N_GROUPS * GROUP_DIM
N_HEADS = 8
NOPE_DIM = 128
ROPE_DIM = 64
QK_DIM = NOPE_DIM + ROPE_DIM
V_DIM = 128
Q_RANK = 512
KV_RANK = 256
ATTN_DIM = N_HEADS * V_DIM
D_FF = 2816
ROPE_THETA = 10000.0
NORM_EPS = 1e-6
ATTN_SCALE = QK_DIM ** -0.5
LOG2_E = math.log2(math.e)
DFT_RADIX = 16

LANES = 128
VMEM_LIMIT_BYTES = 56 * 1024 * 1024
PAIR_UNROLL = 8
DFT_K1_BATCH = 4

_C_UF = 0
_C_CQ = _C_UF + FOURIER_DIM
_C_CKV = _C_CQ + Q_RANK
_C_GA = _C_CKV + KV_RANK
_C_GB = _C_GA + D_MODEL
_C_KR = _C_GB + D_MODEL
IN_COLS = _C_KR + LANES


def _rms(x, g):
    return x * lax.rsqrt(jnp.mean(x * x, axis=-1, keepdims=True) + NORM_EPS) * g


def _rope128(x, cos2, sin2):
    lane = lax.broadcasted_iota(jnp.int32, x.shape, 1)
    first_half = (lane % ROPE_DIM) < (ROPE_DIM // 2)
    partner = jnp.where(first_half, pltpu.roll(x, LANES - ROPE_DIM // 2, 1), pltpu.roll(x, ROPE_DIM // 2, 1))
    return x * cos2 + partner * sin2


def _inproj_kernel(x_ref, g1_ref, win_ref, qg_ref, kvg_ref, wuq_ref, wkt_ref, wv_ref, cos_ref, sin_ref,
                   uf_ref, q_ref, kt_ref, v_ref, sa_ref, sb_ref):
    bf = jnp.bfloat16
    f32 = jnp.float32
    h = _rms(x_ref[0], g1_ref[...]).astype(bf)

    def proj(c0, width):
        return jnp.dot(h, win_ref[:, c0:c0 + width], preferred_element_type=f32)

    uf_ref[0] = proj(_C_UF, FOURIER_DIM)
    sa_ref[0] = jax.nn.sigmoid(proj(_C_GA, D_MODEL)).astype(bf)
    sb_ref[0] = jax.nn.sigmoid(proj(_C_GB, D_MODEL)).astype(bf)

    cos2 = cos_ref[...]
    sin2 = sin_ref[...]

    cq = _rms(proj(_C_CQ, Q_RANK), qg_ref[...]).astype(bf)
    q = jnp.dot(cq, wuq_ref[...], preferred_element_type=f32) * (ATTN_SCALE * LOG2_E)
    rope0 = N_HEADS * NOPE_DIM
    for pair in range(N_HEADS // 2):
        qr = _rope128(q[:, rope0 + pair * LANES:rope0 + (pair + 1) * LANES], cos2, sin2).astype(bf)
        for sub in range(2):
            hd = 2 * pair + sub
            q_ref[0, hd, :, :NOPE_DIM] = q[:, hd * NOPE_DIM:(hd + 1) * NOPE_DIM].astype(bf)
            q_ref[0, hd, :, NOPE_DIM:] = qr[:, sub * ROPE_DIM:(sub + 1) * ROPE_DIM]

    ckv = _rms(proj(_C_CKV, KV_RANK), kvg_ref[...])
    vals = jnp.dot(ckv.astype(bf), wv_ref[...], preferred_element_type=f32)
    ckv_t = ckv.T.astype(bf)
    k_t = jnp.dot(wkt_ref[...], ckv_t, preferred_element_type=f32)
    kr_t = _rope128(proj(_C_KR, LANES), cos2, sin2).T[:ROPE_DIM].astype(bf)
    ones = jnp.ones((vals.shape[0], V_DIM), bf)
    for hd in range(N_HEADS):
        kt_ref[0, hd, 0, :NOPE_DIM, :] = k_t[hd * NOPE_DIM:(hd + 1) * NOPE_DIM].astype(bf)
        kt_ref[0, hd, 0, NOPE_DIM:, :] = kr_t
        v_ref[0, hd, :, :V_DIM] = vals[:, hd * V_DIM:(hd + 1) * V_DIM].astype(bf)
        v_ref[0, hd, :, V_DIM:] = ones


def _const_spec(shape):
    return pl.BlockSpec(shape, lambda *_: (0,) * len(shape), pipeline_mode=pl.Buffered(1))


def _inproj(x, g1, win, qg, kvg, wuq, wkt, wv, cos2, sin2, tm):
    b, s, _ = x.shape
    tok = lambda bi, si: (bi, si, 0)
    head = lambda bi, si: (bi, 0, si, 0)
    bf = jnp.bfloat16
    return pl.pallas_call(
        _inproj_kernel,
        grid=(b, s // tm),
        in_specs=[
            pl.BlockSpec((1, tm, D_MODEL), tok),
            _const_spec((1, D_MODEL)),
            _const_spec((D_MODEL, IN_COLS)),
            _const_spec((1, Q_RANK)),
            _const_spec((1, KV_RANK)),
            _const_spec((Q_RANK, N_HEADS * QK_DIM)),
            _const_spec((N_HEADS * NOPE_DIM, KV_RANK)),
            _const_spec((KV_RANK, N_HEADS * V_DIM)),
            pl.BlockSpec((tm, LANES), lambda bi, si: (si, 0)),
            pl.BlockSpec((tm, LANES), lambda bi, si: (si, 0)),
        ],
        out_specs=[
            pl.BlockSpec((1, tm, FOURIER_DIM), tok),
            pl.BlockSpec((1, N_HEADS, tm, QK_DIM), head),
            pl.BlockSpec((1, N_HEADS, 1, QK_DIM, tm), lambda bi, si: (bi, 0, si, 0, 0)),
            pl.BlockSpec((1, N_HEADS, tm, 2 * V_DIM), head),
            pl.BlockSpec((1, tm, D_MODEL), tok),
            pl.BlockSpec((1, tm, D_MODEL), tok),
        ],
        out_shape=[
            jax.ShapeDtypeStruct((b, s, FOURIER_DIM), jnp.float32),
            jax.ShapeDtypeStruct((b, N_HEADS, s, QK_DIM), bf),
            jax.ShapeDtypeStruct((b, N_HEADS, s // tm, QK_DIM, tm), bf),
            jax.ShapeDtypeStruct((b, N_HEADS, s, 2 * V_DIM), bf),
            jax.ShapeDtypeStruct((b, s, D_MODEL), bf),
            jax.ShapeDtypeStruct((b, s, D_MODEL), bf),
        ],
        compiler_params=pltpu.CompilerParams(
            dimension_semantics=("parallel", "parallel"), vmem_limit_bytes=VMEM_LIMIT_BYTES),
        name="inproj",
    )(x, g1, win, qg, kvg, wuq, wkt, wv, cos2, sin2)


def _cneg(v):
    return None if v is None else -v


def _cadd(a, b):
    if a is None:
        return b
    if b is None:
        return a
    return a + b


def _csub(a, b):
    if b is None:
        return a
    if a is None:
        return -b
    return a - b


def _fft(xs):
    n = len(xs)
    if n == 1:
        return xs
    ev = _fft(xs[0::2])
    od = _fft(xs[1::2])
    out = [None] * n
    for k in range(n // 2):
        re, im = od[k]
        if k == 0:
            tr, ti = re, im
        elif 4 * k == n:
            tr, ti = im, _cneg(re)
        else:
            c = math.cos(2 * math.pi * k / n)
            s = math.sin(2 * math.pi * k / n)
            tr = _cadd(None if re is None else re * c, None if im is None else im * s)
            ti = _csub(None if im is None else im * c, None if re is None else re * s)
        out[k] = (_cadd(ev[k][0], tr), _cadd(ev[k][1], ti))
        out[k + n // 2] = (_csub(ev[k][0], tr), _csub(ev[k][1], ti))
    return out


def _dft16_real(rows):
    out = _fft([(r, None) for r in rows])
    zero = jnp.zeros_like(rows[0])
    out = [(zero if re is None else re, zero if im is None else im) for re, im in out]
    for k in range(DFT_RADIX // 2 + 1, DFT_RADIX):
        re, im = out[DFT_RADIX - k]
        out[k] = (re, -im)
    return out


def _fourier_kernel(n2, xr_ref, xm_ref, twc_ref, tws_ref, dmat_ref, cs_ref, y_ref, x0_ref, z_ref, z0_ref):
    m = n2 - 1
    bf = jnp.bfloat16
    x0_ref[0:N_META, :] = xm_ref[...]
    x0_ref[N_META:N_META + m, :] = xr_ref[0, 0:m, :]

    def run_start(j1):
        return j1 * n2 - N_META

    dc = _dft16_real([x0_ref[0:1, :]] + [xr_ref[0, run_start(j1):run_start(j1) + 1, :] for j1 in range(1, DFT_RADIX)])
    zw = 2 * GROUP_DIM
    for k1 in range(DFT_RADIX):
        z0_ref[0:1, k1 * zw:k1 * zw + GROUP_DIM] = dc[k1][0]
        z0_ref[0:1, k1 * zw + GROUP_DIM:(k1 + 1) * zw] = dc[k1][1]

    def chunk(c, carry):
        r0 = pl.multiple_of(c * 8, 8)
        rows = [x0_ref[pl.ds(r0 + 1, 8), :]]
        rows += [xr_ref[0, pl.ds(r0 + (run_start(j1) + 1), 8), :] for j1 in range(1, DFT_RADIX)]
        a = _dft16_real(rows)
        for k1 in range(DFT_RADIX):
            ar, ai = a[k1]
            if k1 == 0:
                zr, zi = ar, ai
            else:
                tc = twc_ref[k1, pl.ds(r0, 8), :]
                ts = tws_ref[k1, pl.ds(r0, 8), :]
                zr = tc * ar + ts * ai
                zi = tc * ai - ts * ar
            z_ref[pl.ds(r0, 8), k1 * zw:k1 * zw + GROUP_DIM] = zr
            z_ref[pl.ds(r0, 8), k1 * zw + GROUP_DIM:(k1 + 1) * zw] = zi
            z_ref[pl.ds(r0 + m, 8), k1 * zw:k1 * zw + GROUP_DIM] = zi
            z_ref[pl.ds(r0 + m, 8), k1 * zw + GROUP_DIM:(k1 + 1) * zw] = -zr
        return carry

    lax.fori_loop(0, m // 8, chunk, 0)

    scale = 1.0 / math.sqrt(GROUP_DIM * DFT_RADIX * n2)
    for k0 in range(0, DFT_RADIX, DFT_K1_BATCH):
        cols = slice(k0 * zw, (k0 + DFT_K1_BATCH) * zw)
        p = jnp.dot(dmat_ref[...], z_ref[:, cols].astype(bf), preferred_element_type=jnp.float32)
        p = (p + z0_ref[0:1, cols]).astype(bf)
        stacked = jnp.concatenate([p[:, i * zw:(i + 1) * zw] for i in range(DFT_K1_BATCH)], axis=0)
        y = jnp.dot(stacked, cs_ref[...], preferred_element_type=jnp.float32) * scale
        for i in range(DFT_K1_BATCH):
            y_ref[0, 0, pl.ds(k0 + i, m, stride=DFT_RADIX), :] = y[i * m:(i + 1) * m]


def _fourier(uf, uf_meta, twc, tws, dmat, cs):
    b, s, _ = uf.shape
    n2 = (s + N_META) // DFT_RADIX
    m = n2 - 1
    out = pl.pallas_call(
        functools.partial(_fourier_kernel, n2),
        grid=(b, N_GROUPS),
        in_specs=[
            pl.BlockSpec((1, s, GROUP_DIM), lambda bi, g: (bi, 0, g)),
            pl.BlockSpec((N_META, GROUP_DIM), lambda bi, g: (0, g)),
            pl.BlockSpec((DFT_RADIX, m, GROUP_DIM), lambda bi, g: (0, 0, 0)),
            pl.BlockSpec((DFT_RADIX, m, GROUP_DIM), lambda bi, g: (0, 0, 0)),
            pl.BlockSpec((m, 2 * m), lambda bi, g: (0, 0)),
            pl.BlockSpec((2 * GROUP_DIM, GROUP_DIM), lambda bi, g: (0, 0)),
        ],
        out_specs=pl.BlockSpec((1, 1, s, GROUP_DIM), lambda bi, g: (bi, g, 0, 0)),
        out_shape=jax.ShapeDtypeStruct((b, N_GROUPS, s, GROUP_DIM), jnp.float32),
        scratch_shapes=[
            pltpu.VMEM((N_META + m, GROUP_DIM), jnp.float32),
            pltpu.VMEM((2 * m, DFT_RADIX * 2 * GROUP_DIM), jnp.float32),
            pltpu.VMEM((8, DFT_RADIX * 2 * GROUP_DIM), jnp.float32),
        ],
        compiler_params=pltpu.CompilerParams(
            dimension_semantics=("parallel", "parallel"), vmem_limit_bytes=VMEM_LIMIT_BYTES),
        name="fourier",
    )(uf, uf_meta, twc, tws, dmat, cs)
    return out


def _fourier_tables(s):
    n2 = (s + N_META) // DFT_RADIX
    length = DFT_RADIX * n2
    j2 = np.arange(1, n2, dtype=np.int64)
    k1 = np.arange(DFT_RADIX, dtype=np.int64)
    phi = 2.0 * np.pi * ((k1[:, None] * j2[None, :]) % length) / length
    twc = jnp.broadcast_to(jnp.asarray(np.cos(phi), jnp.float32)[:, :, None], (DFT_RADIX, n2 - 1, GROUP_DIM))
    tws = jnp.broadcast_to(jnp.asarray(np.sin(phi), jnp.float32)[:, :, None], (DFT_RADIX, n2 - 1, GROUP_DIM))
    theta = 2.0 * np.pi * ((j2[:, None] * j2[None, :]) % n2) / n2
    dmat = jnp.asarray(np.concatenate([np.cos(theta), np.sin(theta)], axis=1), jnp.float32)
    c = np.arange(GROUP_DIM, dtype=np.int64)
    psi = 2.0 * np.pi * ((c[:, None] * c[None, :]) % GROUP_DIM) / GROUP_DIM
    cs = jnp.asarray(np.concatenate([np.cos(psi), np.sin(psi)], axis=0), jnp.float32)
    return twc, tws, dmat.astype(jnp.bfloat16), cs.astype(jnp.bfloat16)


def _attn_kernel(q_ref, kt_ref, v_ref, kmt_ref, vm_ref, o_ref, s_ref, pm_ref):
    bf = jnp.bfloat16
    f32 = jnp.float32
    tq = o_ref.shape[1]
    n, _, tk = kt_ref.shape[2:]
    qi = pl.program_id(2)
    row_cur = pl.multiple_of(qi * tq, tq)
    row_nxt = pl.multiple_of(jnp.minimum(qi + 1, pl.num_programs(2) - 1) * tq, tq)

    def produce(slot, row0, c):
        s = jnp.dot(q_ref[0, 0, pl.ds(row0, tq), :], kt_ref[0, 0, c], preferred_element_type=f32)
        s_ref[slot] = s
        pm_ref[slot] = functools.reduce(jnp.maximum, [s[:, j * LANES:(j + 1) * LANES] for j in range(tk // LANES)])

    def absorb(slot, c, m_i, acc):
        m_new = jnp.maximum(m_i, jnp.max(pm_ref[slot], axis=1, keepdims=True))
        p = jnp.exp2((s_ref[slot] - m_new).astype(bf))
        vals = v_ref[0, 0, pl.ds(pl.multiple_of(c * tk, tk), tk), :]
        return m_new, jnp.exp2(m_i - m_new) * acc + jnp.dot(p, vals, preferred_element_type=f32)

    @pl.when(qi == 0)
    def _():
        produce(0, row_cur, 0)

    lane = lax.broadcasted_iota(jnp.int32, (tq, LANES), 1)
    s_meta = jnp.dot(q_ref[0, 0, pl.ds(row_cur, tq), :], kmt_ref[0, 0], preferred_element_type=f32)
    s_meta = jnp.where(lane < N_META, s_meta, -jnp.inf)
    m = jnp.max(s_meta, axis=1, keepdims=True)
    acc = jnp.dot(jnp.exp2((s_meta - m).astype(bf)), vm_ref[0, 0], preferred_element_type=f32)

    def pair(i, carry):
        m_i, acc_i = carry
        c = 2 * i
        produce(1, row_cur, c + 1)
        m_i, acc_i = absorb(0, c, m_i, acc_i)
        last = i == n // 2 - 1
        produce(0, jnp.where(last, row_nxt, row_cur), jnp.where(last, 0, c + 2))
        return absorb(1, c + 1, m_i, acc_i)

    m, acc = lax.fori_loop(0, n // 2, pair, (m, acc), unroll=PAIR_UNROLL)
    o_ref[0] = (acc[:, :V_DIM] / acc[:, V_DIM:]).astype(bf)


def _attention(q, kt, v, kmt, vm, tq):
    b, nh, s, _ = q.shape
    n, _, tk = kt.shape[2:]
    assert n % 2 == 0 and n * tk == s
    return pl.pallas_call(
        _attn_kernel,
        grid=(b, nh, s // tq),
        in_specs=[
            pl.BlockSpec((1, 1, s, QK_DIM), lambda bi, h, qi: (bi, h, 0, 0)),
            pl.BlockSpec((1, 1, n, QK_DIM, tk), lambda bi, h, qi: (bi, h, 0, 0, 0)),
            pl.BlockSpec((1, 1, s, 2 * V_DIM), lambda bi, h, qi: (bi, h, 0, 0)),
            pl.BlockSpec((1, 1, QK_DIM, LANES), lambda bi, h, qi: (0, h, 0, 0)),
            pl.BlockSpec((1, 1, LANES, 2 * V_DIM), lambda bi, h, qi: (0, h, 0, 0)),
        ],
        out_specs=pl.BlockSpec((1, tq, V_DIM), lambda bi, h, qi: (bi, qi, h)),
        out_shape=jax.ShapeDtypeStruct((b, s, ATTN_DIM), jnp.bfloat16),
        scratch_shapes=[pltpu.VMEM((2, tq, tk), jnp.float32), pltpu.VMEM((2, tq, LANES), jnp.float32)],
        compiler_params=pltpu.CompilerParams(
            dimension_semantics=("parallel", "parallel", "arbitrary"), vmem_limit_bytes=VMEM_LIMIT_BYTES),
        name="attention",
    )(q, kt, v, kmt, vm)


def _tail_kernel(x_ref, y_ref, o_ref, sa_ref, sb_ref, wfo_ref, wao_ref, wo_ref, g2_ref, wg_ref, wu_ref, wd_ref,
                 gf_ref, out_ref):
    bf = jnp.bfloat16
    f32 = jnp.float32
    yf = jnp.concatenate([y_ref[0, g] for g in range(N_GROUPS)], axis=1).astype(bf)
    ya = jnp.dot(yf, wfo_ref[...], preferred_element_type=f32)
    yb = jnp.dot(o_ref[0], wao_ref[...], preferred_element_type=f32)
    merged = sa_ref[0].astype(f32) * ya + sb_ref[0].astype(f32) * yb
    x1 = x_ref[0] + jnp.dot(merged.astype(bf), wo_ref[...], preferred_element_type=f32)
    h2 = _rms(x1, g2_ref[...]).astype(bf)
    gate = jnp.dot(h2, wg_ref[...], preferred_element_type=f32)
    up = jnp.dot(h2, wu_ref[...], preferred_element_type=f32)
    act = (gate * jax.nn.sigmoid(gate) * up).astype(bf)
    x2 = x1 + jnp.dot(act, wd_ref[...], preferred_element_type=f32)
    out_ref[0] = _rms(x2, gf_ref[...])


def _tail(x, y, o, sa, sb, wfo, wao, wo, g2, wg, wu, wd, gf, tm):
    b, s, _ = x.shape
    tok = lambda bi, si: (bi, si, 0)
    const = lambda bi, si: (0, 0)

    def weight(shape):
        return pl.BlockSpec(shape, const, pipeline_mode=pl.Buffered(1))

    return pl.pallas_call(
        _tail_kernel,
        grid=(b, s // tm),
        in_specs=[
            pl.BlockSpec((1, tm, D_MODEL), tok),
            pl.BlockSpec((1, N_GROUPS, tm, GROUP_DIM), lambda bi, si: (bi, 0, si, 0)),
            pl.BlockSpec((1, tm, ATTN_DIM), tok),
            pl.BlockSpec((1, tm, D_MODEL), tok),
            pl.BlockSpec((1, tm, D_MODEL), tok),
            weight((FOURIER_DIM, D_MODEL)),
            weight((ATTN_DIM, D_MODEL)),
            weight((D_MODEL, D_MODEL)),
            pl.BlockSpec((1, D_MODEL), const),
            weight((D_MODEL, D_FF)),
            weight((D_MODEL, D_FF)),
            weight((D_FF, D_MODEL)),
            pl.BlockSpec((1, D_MODEL), const),
        ],
        out_specs=pl.BlockSpec((1, tm, D_MODEL), tok),
        out_shape=jax.ShapeDtypeStruct((b, s, D_MODEL), jnp.float32),
        compiler_params=pltpu.CompilerParams(
            dimension_semantics=("parallel", "parallel"), vmem_limit_bytes=VMEM_LIMIT_BYTES),
        name="tail",
    )(x, y, o, sa, sb, wfo, wao, wo, g2, wg, wu, wd, gf)


def _rope_tables(length):
    inv = 1.0 / (ROPE_THETA ** (jnp.arange(0, ROPE_DIM, 2, dtype=jnp.float32) / ROPE_DIM))
    ang = jnp.arange(length, dtype=jnp.float32)[:, None] * inv[None, :]
    c, s = jnp.cos(ang), jnp.sin(ang)
    return jnp.concatenate([c, c, c, c], axis=1), jnp.concatenate([-s, s, -s, s], axis=1)


def _prepare_weights(norm1_g, w_in, q_norm_g, kv_norm_g, w_uq, w_ukv, w_fourier_out, w_attn_out, w_o, norm2_g,
                     w_ffn_gate, w_ffn_up, w_ffn_down, final_norm_g):
    bf = jnp.bfloat16
    w = w_in[0]
    s_uf, s_cq, s_ckv, s_kr = FOURIER_DIM, FOURIER_DIM + Q_RANK, FOURIER_DIM + Q_RANK + KV_RANK, \
        FOURIER_DIM + Q_RANK + KV_RANK + ROPE_DIM
    win = jnp.concatenate(
        [w[:, :s_ckv], w[:, s_kr:], w[:, s_ckv:s_kr], jnp.zeros((D_MODEL, LANES - ROPE_DIM), w.dtype)], axis=1).astype(bf)
    wq = w_uq[0].reshape(Q_RANK, N_HEADS, QK_DIM)
    wuq = jnp.concatenate([wq[:, :, :NOPE_DIM].reshape(Q_RANK, -1), wq[:, :, NOPE_DIM:].reshape(Q_RANK, -1)], axis=1).astype(bf)
    wkv = w_ukv[0].reshape(KV_RANK, N_HEADS, NOPE_DIM + V_DIM)
    wkt = wkv[:, :, :NOPE_DIM].reshape(KV_RANK, -1).T.astype(bf)
    wv = wkv[:, :, NOPE_DIM:].reshape(KV_RANK, -1).astype(bf)
    row = lambda g: g.reshape(1, -1).astype(jnp.float32)
    return dict(
        g1=row(norm1_g[0]), win=win, qg=row(q_norm_g[0]), kvg=row(kv_norm_g[0]), wuq=wuq, wkt=wkt, wv=wv,
        wfo=w_fourier_out[0].astype(bf), wao=w_attn_out[0].astype(bf), wo=w_o[0].astype(bf), g2=row(norm2_g[0]),
        wg=w_ffn_gate[0].astype(bf), wu=w_ffn_up[0].astype(bf), wd=w_ffn_down[0].astype(bf), gf=row(final_norm_g))


def _project(p, x, cos2, sin2, tm):
    return _inproj(x, p["g1"], p["win"], p["qg"], p["kvg"], p["wuq"], p["wkt"], p["wv"], cos2, sin2, tm)


def _meta_projection(meta, p, cos2, sin2):
    meta_pad = jnp.zeros((1, LANES, D_MODEL), meta.dtype).at[0, :N_META].set(meta)
    uf_m, _, kt_m, v_m, _, _ = _project(p, meta_pad, cos2[:LANES], sin2[:LANES], LANES)
    return uf_m[0, :N_META], kt_m[:, :, 0], v_m


def _trunk(x, meta_proj, rope, p, tk, tq, tm_tail):
    s = x.shape[1]
    uf_m, kt_m, v_m = meta_proj
    cos2, sin2 = rope
    uf, q, kt, v, sa, sb = _project(p, x, cos2[N_META:N_META + s], sin2[N_META:N_META + s], tk)
    y = _fourier(uf, uf_m, *_fourier_tables(s))
    o = _attention(q, kt, v, kt_m, v_m, tq)
    return _tail(x, y, o, sa, sb, p["wfo"], p["wao"], p["wo"], p["g2"], p["wg"], p["wu"], p["wd"], p["gf"], tm_tail)


def kernel(x_prompt, x_sample, meta_tokens, norm1_g, w_in, q_norm_g, kv_norm_g, w_uq, w_ukv, w_fourier_out,
           w_attn_out, w_o, norm2_g, w_ffn_gate, w_ffn_up, w_ffn_down, final_norm_g):
    p = _prepare_weights(norm1_g, w_in, q_norm_g, kv_norm_g, w_uq, w_ukv, w_fourier_out, w_attn_out, w_o, norm2_g,
                         w_ffn_gate, w_ffn_up, w_ffn_down, final_norm_g)
    rope = _rope_tables(max(x_prompt.shape[1], x_sample.shape[1]) + N_META)
    meta_proj = _meta_projection(meta_tokens, p, *rope)
    cfg = dict(tk=512, tq=1024, tm_tail=256)
    y_prompt = _trunk(x_prompt, meta_proj, rope, p, **cfg)
    y_sample = _trunk(x_sample, meta_proj, rope, p, **cfg)
    return (y_prompt, y_sample)
```

```python
import functools
import math

import jax
import jax.numpy as jnp
import numpy as np
from jax import lax
from jax.experimental import pallas as pl
from jax.experimental.pallas import tpu as pltpu

D_MODEL = 1024
N_META = 16
N_GROUPS = 4
GROUP_DIM = 128
FOURIER_DIM = N_GROUPS * GROUP_DIM
N_HEADS = 8
NOPE_DIM = 128
ROPE_DIM = 64
QK_DIM = NOPE_DIM + ROPE_DIM
V_DIM = 128
Q_RANK = 512
KV_RANK = 256
ATTN_DIM = N_HEADS * V_DIM
D_FF = 2816
ROPE_THETA = 10000.0
NORM_EPS = 1e-6
ATTN_SCALE = QK_DIM ** -0.5
LOG2_E = math.log2(math.e)
DFT_RADIX = 16

LANES = 128
VMEM_LIMIT_BYTES = 56 * 1024 * 1024
DFT_K1_BATCH = 4

_C_UF = 0
_C_CQ = _C_UF + FOURIER_DIM
_C_CKV = _C_CQ + Q_RANK
_C_GA = _C_CKV + KV_RANK
_C_GB = _C_GA + D_MODEL
_C_KR = _C_GB + D_MODEL
IN_COLS = _C_KR + LANES


def _rms(x, g):
    return x * lax.rsqrt(jnp.mean(x * x, axis=-1, keepdims=True) + NORM_EPS) * g


def _rope128(x, cos2, sin2):
    lane = lax.broadcasted_iota(jnp.int32, x.shape, 1)
    first_half = (lane % ROPE_DIM) < (ROPE_DIM // 2)
    partner = jnp.where(first_half, pltpu.roll(x, LANES - ROPE_DIM // 2, 1), pltpu.roll(x, ROPE_DIM // 2, 1))
    return x * cos2 + partner * sin2


def _inproj_kernel(x_ref, g1_ref, win_ref, qg_ref, kvg_ref, wuq_ref, wkt_ref, wv_ref, cos_ref, sin_ref,
                   uf_ref, q_ref, kt_ref, v_ref, sa_ref, sb_ref):
    bf = jnp.bfloat16
    f32 = jnp.float32
    h = _rms(x_ref[0], g1_ref[...]).astype(bf)

    def proj(c0, width):
        return jnp.dot(h, win_ref[:, c0:c0 + width], preferred_element_type=f32)

    uf_ref[0] = proj(_C_UF, FOURIER_DIM)
    sa_ref[0] = jax.nn.sigmoid(proj(_C_GA, D_MODEL)).astype(bf)
    sb_ref[0] = jax.nn.sigmoid(proj(_C_GB, D_MODEL)).astype(bf)

    cos2 = cos_ref[...]
    sin2 = sin_ref[...]

    cq = _rms(proj(_C_CQ, Q_RANK), qg_ref[...]).astype(bf)
    q = jnp.dot(cq, wuq_ref[...], preferred_element_type=f32) * (ATTN_SCALE * LOG2_E)
    rope0 = N_HEADS * NOPE_DIM
    for pair in range(N_HEADS // 2):
        qr = _rope128(q[:, rope0 + pair * LANES:rope0 + (pair + 1) * LANES], cos2, sin2).astype(bf)
        for sub in range(2):
            hd = 2 * pair + sub
            q_ref[0, hd, :, :NOPE_DIM] = q[:, hd * NOPE_DIM:(hd + 1) * NOPE_DIM].astype(bf)
            q_ref[0, hd, :, NOPE_DIM:] = qr[:, sub * ROPE_DIM:(sub + 1) * ROPE_DIM]

    ckv = _rms(proj(_C_CKV, KV_RANK), kvg_ref[...])
    vals = jnp.dot(ckv.astype(bf), wv_ref[...], preferred_element_type=f32)
    ckv_t = ckv.T.astype(bf)
    k_t = jnp.dot(wkt_ref[...], ckv_t, preferred_element_type=f32)
    kr_t = _rope128(proj(_C_KR, LANES), cos2, sin2).T[:ROPE_DIM].astype(bf)
    ones = jnp.ones((vals.shape[0], V_DIM), bf)
    for hd in range(N_HEADS):
        kt_ref[0, hd, 0, :NOPE_DIM, :] = k_t[hd * NOPE_DIM:(hd + 1) * NOPE_DIM].astype(bf)
        kt_ref[0, hd, 0, NOPE_DIM:, :] = kr_t
        v_ref[0, hd, :, :V_DIM] = vals[:, hd * V_DIM:(hd + 1) * V_DIM].astype(bf)
        v_ref[0, hd, :, V_DIM:] = ones


def _const_spec(shape):
    return pl.BlockSpec(shape, lambda *_: (0,) * len(shape), pipeline_mode=pl.Buffered(1))


def _inproj(x, g1, win, qg, kvg, wuq, wkt, wv, cos2, sin2, tm):
    b, s, _ = x.shape
    tok = lambda bi, si: (bi, si, 0)
    head = lambda bi, si: (bi, 0, si, 0)
    bf = jnp.bfloat16
    return pl.pallas_call(
        _inproj_kernel,
        grid=(b, s // tm),
        in_specs=[
            pl.BlockSpec((1, tm, D_MODEL), tok),
            _const_spec((1, D_MODEL)),
            _const_spec((D_MODEL, IN_COLS)),
            _const_spec((1, Q_RANK)),
            _const_spec((1, KV_RANK)),
            _const_spec((Q_RANK, N_HEADS * QK_DIM)),
            _const_spec((N_HEADS * NOPE_DIM, KV_RANK)),
            _const_spec((KV_RANK, N_HEADS * V_DIM)),
            pl.BlockSpec((tm, LANES), lambda bi, si: (si, 0)),
            pl.BlockSpec((tm, LANES), lambda bi, si: (si, 0)),
        ],
        out_specs=[
            pl.BlockSpec((1, tm, FOURIER_DIM), tok),
            pl.BlockSpec((1, N_HEADS, tm, QK_DIM), head),
            pl.BlockSpec((1, N_HEADS, 1, QK_DIM, tm), lambda bi, si: (bi, 0, si, 0, 0)),
            pl.BlockSpec((1, N_HEADS, tm, 2 * V_DIM), head),
            pl.BlockSpec((1, tm, D_MODEL), tok),
            pl.BlockSpec((1, tm, D_MODEL), tok),
        ],
        out_shape=[
            jax.ShapeDtypeStruct((b, s, FOURIER_DIM), jnp.float32),
            jax.ShapeDtypeStruct((b, N_HEADS, s, QK_DIM), bf),
            jax.ShapeDtypeStruct((b, N_HEADS, s // tm, QK_DIM, tm), bf),
            jax.ShapeDtypeStruct((b, N_HEADS, s, 2 * V_DIM), bf),
            jax.ShapeDtypeStruct((b, s, D_MODEL), bf),
            jax.ShapeDtypeStruct((b, s, D_MODEL), bf),
        ],
        compiler_params=pltpu.CompilerParams(
            dimension_semantics=("parallel", "parallel"), vmem_limit_bytes=VMEM_LIMIT_BYTES),
        name="inproj",
    )(x, g1, win, qg, kvg, wuq, wkt, wv, cos2, sin2)


def _cneg(v):
    return None if v is None else -v


def _cadd(a, b):
    if a is None:
        return b
    if b is None:
        return a
    return a + b


def _csub(a, b):
    if b is None:
        return a
    if a is None:
        return -b
    return a - b


def _fft(xs):
    n = len(xs)
    if n == 1:
        return xs
    ev = _fft(xs[0::2])
    od = _fft(xs[1::2])
    out = [None] * n
    for k in range(n // 2):
        re, im = od[k]
        if k == 0:
            tr, ti = re, im
        elif 4 * k == n:
            tr, ti = im, _cneg(re)
        else:
            c = math.cos(2 * math.pi * k / n)
            s = math.sin(2 * math.pi * k / n)
            tr = _cadd(None if re is None else re * c, None if im is None else im * s)
            ti = _csub(None if im is None else im * c, None if re is None else re * s)
        out[k] = (_cadd(ev[k][0], tr), _cadd(ev[k][1], ti))
        out[k + n // 2] = (_csub(ev[k][0], tr), _csub(ev[k][1], ti))
    return out


def _dft16_real(rows):
    out = _fft([(r, None) for r in rows])
    zero = jnp.zeros_like(rows[0])
    out = [(zero if re is None else re, zero if im is None else im) for re, im in out]
    for k in range(DFT_RADIX // 2 + 1, DFT_RADIX):
        re, im = out[DFT_RADIX - k]
        out[k] = (re, -im)
    return out


def _fourier_kernel(n2, xr_ref, xm_ref, twc_ref, tws_ref, dmat_ref, cs_ref, y_ref, x0_ref, z_ref, z0_ref):
    m = n2 - 1
    bf = jnp.bfloat16
    x0_ref[0:N_META, :] = xm_ref[...]
    x0_ref[N_META:N_META + m, :] = xr_ref[0, 0:m, :]

    def run_start(j1):
        return j1 * n2 - N_META

    dc = _dft16_real([x0_ref[0:1, :]] + [xr_ref[0, run_start(j1):run_start(j1) + 1, :] for j1 in range(1, DFT_RADIX)])
    zw = 2 * GROUP_DIM
    for k1 in range(DFT_RADIX):
        z0_ref[0:1, k1 * zw:k1 * zw + GROUP_DIM] = dc[k1][0]
        z0_ref[0:1, k1 * zw + GROUP_DIM:(k1 + 1) * zw] = dc[k1][1]

    def chunk(c, carry):
        r0 = pl.multiple_of(c * 8, 8)
        rows = [x0_ref[pl.ds(r0 + 1, 8), :]]
        rows += [xr_ref[0, pl.ds(r0 + (run_start(j1) + 1), 8), :] for j1 in range(1, DFT_RADIX)]
        a = _dft16_real(rows)
        for k1 in range(DFT_RADIX):
            ar, ai = a[k1]
            if k1 == 0:
                zr, zi = ar, ai
            else:
                tc = twc_ref[k1, pl.ds(r0, 8), :]
                ts = tws_ref[k1, pl.ds(r0, 8), :]
                zr = tc * ar + ts * ai
                zi = tc * ai - ts * ar
            z_ref[pl.ds(r0, 8), k1 * zw:k1 * zw + GROUP_DIM] = zr
            z_ref[pl.ds(r0, 8), k1 * zw + GROUP_DIM:(k1 + 1) * zw] = zi
            z_ref[pl.ds(r0 + m, 8), k1 * zw:k1 * zw + GROUP_DIM] = zi
            z_ref[pl.ds(r0 + m, 8), k1 * zw + GROUP_DIM:(k1 + 1) * zw] = -zr
        return carry

    lax.fori_loop(0, m // 8, chunk, 0)

    scale = 1.0 / math.sqrt(GROUP_DIM * DFT_RADIX * n2)
    for k0 in range(0, DFT_RADIX, DFT_K1_BATCH):
        cols = slice(k0 * zw, (k0 + DFT_K1_BATCH) * zw)
        p = jnp.dot(dmat_ref[...], z_ref[:, cols].astype(bf), preferred_element_type=jnp.float32)
        p = (p + z0_ref[0:1, cols]).astype(bf)
        stacked = jnp.concatenate([p[:, i * zw:(i + 1) * zw] for i in range(DFT_K1_BATCH)], axis=0)
        y = jnp.dot(stacked, cs_ref[...], preferred_element_type=jnp.float32) * scale
        for i in range(DFT_K1_BATCH):
            y_ref[0, 0, pl.ds(k0 + i, m, stride=DFT_RADIX), :] = y[i * m:(i + 1) * m]


def _fourier(uf, uf_meta, twc, tws, dmat, cs):
    b, s, _ = uf.shape
    n2 = (s + N_META) // DFT_RADIX
    m = n2 - 1
    out = pl.pallas_call(
        functools.partial(_fourier_kernel, n2),
        grid=(b, N_GROUPS),
        in_specs=[
            pl.BlockSpec((1, s, GROUP_DIM), lambda bi, g: (bi, 0, g)),
            pl.BlockSpec((N_META, GROUP_DIM), lambda bi, g: (0, g)),
            pl.BlockSpec((DFT_RADIX, m, GROUP_DIM), lambda bi, g: (0, 0, 0)),
            pl.BlockSpec((DFT_RADIX, m, GROUP_DIM), lambda bi, g: (0, 0, 0)),
            pl.BlockSpec((m, 2 * m), lambda bi, g: (0, 0)),
            pl.BlockSpec((2 * GROUP_DIM, GROUP_DIM), lambda bi, g: (0, 0)),
        ],
        out_specs=pl.BlockSpec((1, 1, s, GROUP_DIM), lambda bi, g: (bi, g, 0, 0)),
        out_shape=jax.ShapeDtypeStruct((b, N_GROUPS, s, GROUP_DIM), jnp.float32),
        scratch_shapes=[
            pltpu.VMEM((N_META + m, GROUP_DIM), jnp.float32),
            pltpu.VMEM((2 * m, DFT_RADIX * 2 * GROUP_DIM), jnp.float32),
            pltpu.VMEM((8, DFT_RADIX * 2 * GROUP_DIM), jnp.float32),
        ],
        compiler_params=pltpu.CompilerParams(
            dimension_semantics=("parallel", "parallel"), vmem_limit_bytes=VMEM_LIMIT_BYTES),
        name="fourier",
    )(uf, uf_meta, twc, tws, dmat, cs)
    return out


def _fourier_tables(s):
    n2 = (s + N_META) // DFT_RADIX
    length = DFT_RADIX * n2
    j2 = np.arange(1, n2, dtype=np.int64)
    k1 = np.arange(DFT_RADIX, dtype=np.int64)
    phi = 2.0 * np.pi * ((k1[:, None] * j2[None, :]) % length) / length
    twc = jnp.broadcast_to(jnp.asarray(np.cos(phi), jnp.float32)[:, :, None], (DFT_RADIX, n2 - 1, GROUP_DIM))
    tws = jnp.broadcast_to(jnp.asarray(np.sin(phi), jnp.float32)[:, :, None], (DFT_RADIX, n2 - 1, GROUP_DIM))
    theta = 2.0 * np.pi * ((j2[:, None] * j2[None, :]) % n2) / n2
    dmat = jnp.asarray(np.concatenate([np.cos(theta), np.sin(theta)], axis=1), jnp.float32)
    c = np.arange(GROUP_DIM, dtype=np.int64)
    psi = 2.0 * np.pi * ((c[:, None] * c[None, :]) % GROUP_DIM) / GROUP_DIM
    cs = jnp.asarray(np.concatenate([np.cos(psi), np.sin(psi)], axis=0), jnp.float32)
    return twc, tws, dmat.astype(jnp.bfloat16), cs.astype(jnp.bfloat16)


def _attn_kernel(q_ref, kt_ref, v_ref, kmt_ref, vm_ref, o_ref, s_ref, pm_ref):
    bf = jnp.bfloat16
    f32 = jnp.float32
    tq = o_ref.shape[1]
    n, _, tk = kt_ref.shape[2:]
    qi = pl.program_id(2)
    row_cur = pl.multiple_of(qi * tq, tq)
    row_nxt = pl.multiple_of(jnp.minimum(qi + 1, pl.num_programs(2) - 1) * tq, tq)

    group = s_ref.shape[2] // tk
    n_groups = n // group

    def produce(slot, row0, g):
        qt = q_ref[0, 0, pl.ds(row0, tq), :]
        pmax = None
        for j in range(group):
            s = jnp.dot(qt, kt_ref[0, 0, g * group + j], preferred_element_type=f32)
            s_ref[slot, :, j * tk:(j + 1) * tk] = s
            for l in range(tk // LANES):
                blk = s[:, l * LANES:(l + 1) * LANES]
                pmax = blk if pmax is None else jnp.maximum(pmax, blk)
        pm_ref[slot] = pmax

    def absorb(slot, g, m_i, acc):
        pmax = pm_ref[slot]
        if g == 0:
            lane = lax.broadcasted_iota(jnp.int32, (tq, LANES), 1)
            s_meta = jnp.dot(q_ref[0, 0, pl.ds(row_cur, tq), :], kmt_ref[0, 0], preferred_element_type=f32)
            s_meta = jnp.where(lane < N_META, s_meta, -jnp.inf)
            pmax = jnp.maximum(pmax, s_meta)
        m_new = jnp.max(pmax, axis=1, keepdims=True)
        if g > 0:
            m_new = jnp.maximum(m_i, m_new)
        p = jnp.exp2((s_ref[slot] - m_new).astype(bf))
        pv = jnp.dot(p, v_ref[0, 0, g * group * tk:(g + 1) * group * tk, :], preferred_element_type=f32)
        if g == 0:
            return m_new, pv + jnp.dot(jnp.exp2((s_meta - m_new).astype(bf)), vm_ref[0, 0], preferred_element_type=f32)
        return m_new, jnp.exp2(m_i - m_new) * acc + pv

    @pl.when(qi == 0)
    def _():
        produce(0, row_cur, 0)

    m = acc = None
    for g in range(n_groups):
        if g + 1 < n_groups:
            produce((g + 1) % 2, row_cur, g + 1)
        else:
            produce(0, row_nxt, 0)
        m, acc = absorb(g % 2, g, m, acc)
    o_ref[0] = (acc[:, :V_DIM] / acc[:, V_DIM:]).astype(bf)


def _attention(q, kt, v, kmt, vm, tq, tg):
    b, nh, s, _ = q.shape
    n, _, tk = kt.shape[2:]
    assert n * tk == s and tg % tk == 0 and (s // tg) % 2 == 0
    return pl.pallas_call(
        _attn_kernel,
        grid=(b, nh, s // tq),
        in_specs=[
            pl.BlockSpec((1, 1, s, QK_DIM), lambda bi, h, qi: (bi, h, 0, 0)),
            pl.BlockSpec((1, 1, n, QK_DIM, tk), lambda bi, h, qi: (bi, h, 0, 0, 0)),
            pl.BlockSpec((1, 1, s, 2 * V_DIM), lambda bi, h, qi: (bi, h, 0, 0)),
            pl.BlockSpec((1, 1, QK_DIM, LANES), lambda bi, h, qi: (0, h, 0, 0)),
            pl.BlockSpec((1, 1, LANES, 2 * V_DIM), lambda bi, h, qi: (0, h, 0, 0)),
        ],
        out_specs=pl.BlockSpec((1, tq, V_DIM), lambda bi, h, qi: (bi, qi, h)),
        out_shape=jax.ShapeDtypeStruct((b, s, ATTN_DIM), jnp.bfloat16),
        scratch_shapes=[pltpu.VMEM((2, tq, tg), jnp.float32), pltpu.VMEM((2, tq, LANES), jnp.float32)],
        compiler_params=pltpu.CompilerParams(
            dimension_semantics=("parallel", "parallel", "arbitrary"), vmem_limit_bytes=VMEM_LIMIT_BYTES),
        name="attention",
    )(q, kt, v, kmt, vm)


def _tail_kernel(x_ref, y_ref, o_ref, sa_ref, sb_ref, wfo_ref, wao_ref, wo_ref, g2_ref, wg_ref, wu_ref, wd_ref,
                 gf_ref, out_ref):
    bf = jnp.bfloat16
    f32 = jnp.float32
    yf = jnp.concatenate([y_ref[0, g] for g in range(N_GROUPS)], axis=1).astype(bf)
    ya = jnp.dot(yf, wfo_ref[...], preferred_element_type=f32)
    yb = jnp.dot(o_ref[0], wao_ref[...], preferred_element_type=f32)
    merged = sa_ref[0].astype(f32) * ya + sb_ref[0].astype(f32) * yb
    x1 = x_ref[0] + jnp.dot(merged.astype(bf), wo_ref[...], preferred_element_type=f32)
    h2 = _rms(x1, g2_ref[...]).astype(bf)
    gate = jnp.dot(h2, wg_ref[...], preferred_element_type=f32)
    up = jnp.dot(h2, wu_ref[...], preferred_element_type=f32)
    act = (gate * jax.nn.sigmoid(gate) * up).astype(bf)
    x2 = x1 + jnp.dot(act, wd_ref[...], preferred_element_type=f32)
    out_ref[0] = _rms(x2, gf_ref[...])


def _tail(x, y, o, sa, sb, wfo, wao, wo, g2, wg, wu, wd, gf, tm):
    b, s, _ = x.shape
    tok = lambda bi, si: (bi, si, 0)
    const = lambda bi, si: (0, 0)

    def weight(shape):
        return pl.BlockSpec(shape, const, pipeline_mode=pl.Buffered(1))

    return pl.pallas_call(
        _tail_kernel,
        grid=(b, s // tm),
        in_specs=[
            pl.BlockSpec((1, tm, D_MODEL), tok),
            pl.BlockSpec((1, N_GROUPS, tm, GROUP_DIM), lambda bi, si: (bi, 0, si, 0)),
            pl.BlockSpec((1, tm, ATTN_DIM), tok),
            pl.BlockSpec((1, tm, D_MODEL), tok),
            pl.BlockSpec((1, tm, D_MODEL), tok),
            weight((FOURIER_DIM, D_MODEL)),
            weight((ATTN_DIM, D_MODEL)),
            weight((D_MODEL, D_MODEL)),
            pl.BlockSpec((1, D_MODEL), const),
            weight((D_MODEL, D_FF)),
            weight((D_MODEL, D_FF)),
            weight((D_FF, D_MODEL)),
            pl.BlockSpec((1, D_MODEL), const),
        ],
        out_specs=pl.BlockSpec((1, tm, D_MODEL), tok),
        out_shape=jax.ShapeDtypeStruct((b, s, D_MODEL), jnp.float32),
        compiler_params=pltpu.CompilerParams(
            dimension_semantics=("parallel", "parallel"), vmem_limit_bytes=VMEM_LIMIT_BYTES),
        name="tail",
    )(x, y, o, sa, sb, wfo, wao, wo, g2, wg, wu, wd, gf)


def _rope_tables(length):
    inv = 1.0 / (ROPE_THETA ** (jnp.arange(0, ROPE_DIM, 2, dtype=jnp.float32) / ROPE_DIM))
    ang = jnp.arange(length, dtype=jnp.float32)[:, None] * inv[None, :]
    c, s = jnp.cos(ang), jnp.sin(ang)
    return jnp.concatenate([c, c, c, c], axis=1), jnp.concatenate([-s, s, -s, s], axis=1)


def _prepare_weights(norm1_g, w_in, q_norm_g, kv_norm_g, w_uq, w_ukv, w_fourier_out, w_attn_out, w_o, norm2_g,
                     w_ffn_gate, w_ffn_up, w_ffn_down, final_norm_g):
    bf = jnp.bfloat16
    w = w_in[0]
    s_uf, s_cq, s_ckv, s_kr = FOURIER_DIM, FOURIER_DIM + Q_RANK, FOURIER_DIM + Q_RANK + KV_RANK, \
        FOURIER_DIM + Q_RANK + KV_RANK + ROPE_DIM
    win = jnp.concatenate(
        [w[:, :s_ckv], w[:, s_kr:], w[:, s_ckv:s_kr], jnp.zeros((D_MODEL, LANES - ROPE_DIM), w.dtype)], axis=1).astype(bf)
    wq = w_uq[0].reshape(Q_RANK, N_HEADS, QK_DIM)
    wuq = jnp.concatenate([wq[:, :, :NOPE_DIM].reshape(Q_RANK, -1), wq[:, :, NOPE_DIM:].reshape(Q_RANK, -1)], axis=1).astype(bf)
    wkv = w_ukv[0].reshape(KV_RANK, N_HEADS, NOPE_DIM + V_DIM)
    wkt = wkv[:, :, :NOPE_DIM].reshape(KV_RANK, -1).T.astype(bf)
    wv = wkv[:, :, NOPE_DIM:].reshape(KV_RANK, -1).astype(bf)
    row = lambda g: g.reshape(1, -1).astype(jnp.float32)
    return dict(
        g1=row(norm1_g[0]), win=win, qg=row(q_norm_g[0]), kvg=row(kv_norm_g[0]), wuq=wuq, wkt=wkt, wv=wv,
        wfo=w_fourier_out[0].astype(bf), wao=w_attn_out[0].astype(bf), wo=w_o[0].astype(bf), g2=row(norm2_g[0]),
        wg=w_ffn_gate[0].astype(bf), wu=w_ffn_up[0].astype(bf), wd=w_ffn_down[0].astype(bf), gf=row(final_norm_g))


def _project(p, x, cos2, sin2, tm):
    return _inproj(x, p["g1"], p["win"], p["qg"], p["kvg"], p["wuq"], p["wkt"], p["wv"], cos2, sin2, tm)


def _meta_projection(meta, p, cos2, sin2):
    meta_pad = jnp.zeros((1, LANES, D_MODEL), meta.dtype).at[0, :N_META].set(meta)
    uf_m, _, kt_m, v_m, _, _ = _project(p, meta_pad, cos2[:LANES], sin2[:LANES], LANES)
    return uf_m[0, :N_META], kt_m[:, :, 0], v_m


def _trunk(x, meta_proj, rope, p, tk, tq, tg, tm_tail):
    s = x.shape[1]
    uf_m, kt_m, v_m = meta_proj
    cos2, sin2 = rope
    uf, q, kt, v, sa, sb = _project(p, x, cos2[N_META:N_META + s], sin2[N_META:N_META + s], tk)
    y = _fourier(uf, uf_m, *_fourier_tables(s))
    o = _attention(q, kt, v, kt_m, v_m, tq, tg)
    return _tail(x, y, o, sa, sb, p["wfo"], p["wao"], p["wo"], p["g2"], p["wg"], p["wu"], p["wd"], p["gf"], tm_tail)


def kernel(x_prompt, x_sample, meta_tokens, norm1_g, w_in, q_norm_g, kv_norm_g, w_uq, w_ukv, w_fourier_out,
           w_attn_out, w_o, norm2_g, w_ffn_gate, w_ffn_up, w_ffn_down, final_norm_g):
    p = _prepare_weights(norm1_g, w_in, q_norm_g, kv_norm_g, w_uq, w_ukv, w_fourier_out, w_attn_out, w_o, norm2_g,
                         w_ffn_gate, w_ffn_up, w_ffn_down, final_norm_g)
    rope = _rope_tables(max(x_prompt.shape[1], x_sample.shape[1]) + N_META)
    meta_proj = _meta_projection(meta_tokens, p, *rope)
    cfg = dict(tk=512, tq=1024, tg=1024, tm_tail=256)
    y_prompt = _trunk(x_prompt, meta_proj, rope, p, **cfg)
    y_sample = _trunk(x_sample, meta_proj, rope, p, **cfg)
    return (y_prompt, y_sample)
```

```python
import functools
import math

import jax
import jax.numpy as jnp
import numpy as np
from jax import lax
from jax.experimental import pallas as pl
from jax.experimental.pallas import tpu as pltpu

D_MODEL = 1024
N_META = 16
N_GROUPS = 4
GROUP_DIM = 128
FOURIER_DIM = N_GROUPS * GROUP_DIM
N_HEADS = 8
NOPE_DIM = 128
ROPE_DIM = 64
QK_DIM = NOPE_DIM + ROPE_DIM
V_DIM = 128
Q_RANK = 512
KV_RANK = 256
ATTN_DIM = N_HEADS * V_DIM
D_FF = 2816
ROPE_THETA = 10000.0
NORM_EPS = 1e-6
ATTN_SCALE = QK_DIM ** -0.5
LOG2_E = math.log2(math.e)
DFT_RADIX = 16

LANES = 128
VMEM_LIMIT_BYTES = 56 * 1024 * 1024
DFT_K1_BATCH = 4
FF_CHUNK = 1024

_C_UF = 0
_C_CQ = _C_UF + FOURIER_DIM
_C_CKV = _C_CQ + Q_RANK
_C_GA = _C_CKV + KV_RANK
_C_GB = _C_GA + D_MODEL
_C_KR = _C_GB + D_MODEL
IN_COLS = _C_KR + LANES


def _rms(x, g):
    return x * lax.rsqrt(jnp.mean(x * x, axis=-1, keepdims=True) + NORM_EPS) * g


def _rope128(x, cos2, sin2):
    lane = lax.broadcasted_iota(jnp.int32, x.shape, 1)
    first_half = (lane % ROPE_DIM) < (ROPE_DIM // 2)
    partner = jnp.where(first_half, pltpu.roll(x, LANES - ROPE_DIM // 2, 1), pltpu.roll(x, ROPE_DIM // 2, 1))
    return x * cos2 + partner * sin2


def _inproj_kernel(x_ref, g1_ref, win_ref, qg_ref, kvg_ref, wuq_ref, wkt_ref, wv_ref, cos_ref, sin_ref,
                   uf_ref, q_ref, kt_ref, v_ref, sa_ref, sb_ref):
    bf = jnp.bfloat16
    f32 = jnp.float32
    h = _rms(x_ref[0], g1_ref[...]).astype(bf)

    def proj(c0, width):
        return jnp.dot(h, win_ref[:, c0:c0 + width], preferred_element_type=f32)

    uf_ref[0] = proj(_C_UF, FOURIER_DIM)
    sa_ref[0] = jax.nn.sigmoid(proj(_C_GA, D_MODEL)).astype(bf)
    sb_ref[0] = jax.nn.sigmoid(proj(_C_GB, D_MODEL)).astype(bf)

    cos2 = cos_ref[...]
    sin2 = sin_ref[...]

    cq = _rms(proj(_C_CQ, Q_RANK), qg_ref[...]).astype(bf)
    q = jnp.dot(cq, wuq_ref[...], preferred_element_type=f32) * (ATTN_SCALE * LOG2_E)
    rope0 = N_HEADS * NOPE_DIM
    for pair in range(N_HEADS // 2):
        qr = _rope128(q[:, rope0 + pair * LANES:rope0 + (pair + 1) * LANES], cos2, sin2).astype(bf)
        for sub in range(2):
            hd = 2 * pair + sub
            q_ref[0, hd, :, :NOPE_DIM] = q[:, hd * NOPE_DIM:(hd + 1) * NOPE_DIM].astype(bf)
            q_ref[0, hd, :, NOPE_DIM:] = qr[:, sub * ROPE_DIM:(sub + 1) * ROPE_DIM]

    ckv = _rms(proj(_C_CKV, KV_RANK), kvg_ref[...])
    vals = jnp.dot(ckv.astype(bf), wv_ref[...], preferred_element_type=f32)
    ckv_t = ckv.T.astype(bf)
    k_t = jnp.dot(wkt_ref[...], ckv_t, preferred_element_type=f32)
    kr_t = _rope128(proj(_C_KR, LANES), cos2, sin2).T[:ROPE_DIM].astype(bf)
    for hd in range(N_HEADS):
        kt_ref[0, hd, 0, :NOPE_DIM, :] = k_t[hd * NOPE_DIM:(hd + 1) * NOPE_DIM].astype(bf)
        kt_ref[0, hd, 0, NOPE_DIM:, :] = kr_t
        v_ref[0, hd] = vals[:, hd * V_DIM:(hd + 1) * V_DIM].astype(bf)


def _const_spec(shape):
    return pl.BlockSpec(shape, lambda *_: (0,) * len(shape), pipeline_mode=pl.Buffered(1))


def _inproj(x, g1, win, qg, kvg, wuq, wkt, wv, cos2, sin2, tm):
    b, s, _ = x.shape
    tok = lambda bi, si: (bi, si, 0)
    head = lambda bi, si: (bi, 0, si, 0)
    bf = jnp.bfloat16
    return pl.pallas_call(
        _inproj_kernel,
        grid=(b, s // tm),
        in_specs=[
            pl.BlockSpec((1, tm, D_MODEL), tok),
            _const_spec((1, D_MODEL)),
            _const_spec((D_MODEL, IN_COLS)),
            _const_spec((1, Q_RANK)),
            _const_spec((1, KV_RANK)),
            _const_spec((Q_RANK, N_HEADS * QK_DIM)),
            _const_spec((N_HEADS * NOPE_DIM, KV_RANK)),
            _const_spec((KV_RANK, N_HEADS * V_DIM)),
            pl.BlockSpec((tm, LANES), lambda bi, si: (si, 0)),
            pl.BlockSpec((tm, LANES), lambda bi, si: (si, 0)),
        ],
        out_specs=[
            pl.BlockSpec((1, tm, FOURIER_DIM), tok),
            pl.BlockSpec((1, N_HEADS, tm, QK_DIM), head),
            pl.BlockSpec((1, N_HEADS, 1, QK_DIM, tm), lambda bi, si: (bi, 0, si, 0, 0)),
            pl.BlockSpec((1, N_HEADS, tm, V_DIM), head),
            pl.BlockSpec((1, tm, D_MODEL), tok),
            pl.BlockSpec((1, tm, D_MODEL), tok),
        ],
        out_shape=[
            jax.ShapeDtypeStruct((b, s, FOURIER_DIM), jnp.float32),
            jax.ShapeDtypeStruct((b, N_HEADS, s, QK_DIM), bf),
            jax.ShapeDtypeStruct((b, N_HEADS, s // tm, QK_DIM, tm), bf),
            jax.ShapeDtypeStruct((b, N_HEADS, s, V_DIM), bf),
            jax.ShapeDtypeStruct((b, s, D_MODEL), bf),
            jax.ShapeDtypeStruct((b, s, D_MODEL), bf),
        ],
        compiler_params=pltpu.CompilerParams(
            dimension_semantics=("parallel", "parallel"), vmem_limit_bytes=VMEM_LIMIT_BYTES),
        name="inproj",
    )(x, g1, win, qg, kvg, wuq, wkt, wv, cos2, sin2)


def _cneg(v):
    return None if v is None else -v


def _cadd(a, b):
    if a is None:
        return b
    if b is None:
        return a
    return a + b


def _csub(a, b):
    if b is None:
        return a
    if a is None:
        return -b
    return a - b


def _fft(xs):
    n = len(xs)
    if n == 1:
        return xs
    ev = _fft(xs[0::2])
    od = _fft(xs[1::2])
    out = [None] * n
    for k in range(n // 2):
        re, im = od[k]
        if k == 0:
            tr, ti = re, im
        elif 4 * k == n:
            tr, ti = im, _cneg(re)
        else:
            c = math.cos(2 * math.pi * k / n)
            s = math.sin(2 * math.pi * k / n)
            tr = _cadd(None if re is None else re * c, None if im is None else im * s)
            ti = _csub(None if im is None else im * c, None if re is None else re * s)
        out[k] = (_cadd(ev[k][0], tr), _cadd(ev[k][1], ti))
        out[k + n // 2] = (_csub(ev[k][0], tr), _csub(ev[k][1], ti))
    return out


def _dft16_real(rows):
    out = _fft([(r, None) for r in rows])
    zero = jnp.zeros_like(rows[0])
    out = [(zero if re is None else re, zero if im is None else im) for re, im in out]
    for k in range(DFT_RADIX // 2 + 1, DFT_RADIX):
        re, im = out[DFT_RADIX - k]
        out[k] = (re, -im)
    return out


def _fourier_kernel(n2, xr_ref, xm_ref, twc_ref, tws_ref, dmat_ref, cs_ref, y_ref, x0_ref, z_ref, z0_ref):
    m = n2 - 1
    bf = jnp.bfloat16
    x0_ref[0:N_META, :] = xm_ref[...]
    x0_ref[N_META:N_META + m, :] = xr_ref[0, 0:m, :]

    def run_start(j1):
        return j1 * n2 - N_META

    dc = _dft16_real([x0_ref[0:1, :]] + [xr_ref[0, run_start(j1):run_start(j1) + 1, :] for j1 in range(1, DFT_RADIX)])
    zw = 2 * GROUP_DIM
    for k1 in range(DFT_RADIX):
        z0_ref[0:1, k1 * zw:k1 * zw + GROUP_DIM] = dc[k1][0]
        z0_ref[0:1, k1 * zw + GROUP_DIM:(k1 + 1) * zw] = dc[k1][1]

    def chunk(c, carry):
        r0 = pl.multiple_of(c * 8, 8)
        rows = [x0_ref[pl.ds(r0 + 1, 8), :]]
        rows += [xr_ref[0, pl.ds(r0 + (run_start(j1) + 1), 8), :] for j1 in range(1, DFT_RADIX)]
        a = _dft16_real(rows)
        for k1 in range(DFT_RADIX):
            ar, ai = a[k1]
            if k1 == 0:
                zr, zi = ar, ai
            else:
                tc = twc_ref[k1, pl.ds(r0, 8), :]
                ts = tws_ref[k1, pl.ds(r0, 8), :]
                zr = tc * ar + ts * ai
                zi = tc * ai - ts * ar
            z_ref[pl.ds(r0, 8), k1 * zw:k1 * zw + GROUP_DIM] = zr
            z_ref[pl.ds(r0, 8), k1 * zw + GROUP_DIM:(k1 + 1) * zw] = zi
            z_ref[pl.ds(r0 + m, 8), k1 * zw:k1 * zw + GROUP_DIM] = zi
            z_ref[pl.ds(r0 + m, 8), k1 * zw + GROUP_DIM:(k1 + 1) * zw] = -zr
        return carry

    lax.fori_loop(0, m // 8, chunk, 0)

    scale = 1.0 / math.sqrt(GROUP_DIM * DFT_RADIX * n2)
    for k0 in range(0, DFT_RADIX, DFT_K1_BATCH):
        cols = slice(k0 * zw, (k0 + DFT_K1_BATCH) * zw)
        p = jnp.dot(dmat_ref[...], z_ref[:, cols].astype(bf), preferred_element_type=jnp.float32)
        p = (p + z0_ref[0:1, cols]).astype(bf)
        stacked = jnp.concatenate([p[:, i * zw:(i + 1) * zw] for i in range(DFT_K1_BATCH)], axis=0)
        y = jnp.dot(stacked, cs_ref[...], preferred_element_type=jnp.float32) * scale
        for i in range(DFT_K1_BATCH):
            y_ref[0, 0, pl.ds(k0 + i, m, stride=DFT_RADIX), :] = y[i * m:(i + 1) * m]


def _fourier(uf, uf_meta, twc, tws, dmat, cs):
    b, s, _ = uf.shape
    n2 = (s + N_META) // DFT_RADIX
    m = n2 - 1
    out = pl.pallas_call(
        functools.partial(_fourier_kernel, n2),
        grid=(b, N_GROUPS),
        in_specs=[
            pl.BlockSpec((1, s, GROUP_DIM), lambda bi, g: (bi, 0, g)),
            pl.BlockSpec((N_META, GROUP_DIM), lambda bi, g: (0, g)),
            pl.BlockSpec((DFT_RADIX, m, GROUP_DIM), lambda bi, g: (0, 0, 0)),
            pl.BlockSpec((DFT_RADIX, m, GROUP_DIM), lambda bi, g: (0, 0, 0)),
            pl.BlockSpec((m, 2 * m), lambda bi, g: (0, 0)),
            pl.BlockSpec((2 * GROUP_DIM, GROUP_DIM), lambda bi, g: (0, 0)),
        ],
        out_specs=pl.BlockSpec((1, 1, s, GROUP_DIM), lambda bi, g: (bi, g, 0, 0)),
        out_shape=jax.ShapeDtypeStruct((b, N_GROUPS, s, GROUP_DIM), jnp.float32),
        scratch_shapes=[
            pltpu.VMEM((N_META + m, GROUP_DIM), jnp.float32),
            pltpu.VMEM((2 * m, DFT_RADIX * 2 * GROUP_DIM), jnp.float32),
            pltpu.VMEM((8, DFT_RADIX * 2 * GROUP_DIM), jnp.float32),
        ],
        compiler_params=pltpu.CompilerParams(
            dimension_semantics=("parallel", "parallel"), vmem_limit_bytes=VMEM_LIMIT_BYTES),
        name="fourier",
    )(uf, uf_meta, twc, tws, dmat, cs)
    return out


def _fourier_tables(s):
    n2 = (s + N_META) // DFT_RADIX
    length = DFT_RADIX * n2
    j2 = np.arange(1, n2, dtype=np.int64)
    k1 = np.arange(DFT_RADIX, dtype=np.int64)
    phi = 2.0 * np.pi * ((k1[:, None] * j2[None, :]) % length) / length
    twc = jnp.broadcast_to(jnp.asarray(np.cos(phi), jnp.float32)[:, :, None], (DFT_RADIX, n2 - 1, GROUP_DIM))
    tws = jnp.broadcast_to(jnp.asarray(np.sin(phi), jnp.float32)[:, :, None], (DFT_RADIX, n2 - 1, GROUP_DIM))
    theta = 2.0 * np.pi * ((j2[:, None] * j2[None, :]) % n2) / n2
    dmat = jnp.asarray(np.concatenate([np.cos(theta), np.sin(theta)], axis=1), jnp.float32)
    c = np.arange(GROUP_DIM, dtype=np.int64)
    psi = 2.0 * np.pi * ((c[:, None] * c[None, :]) % GROUP_DIM) / GROUP_DIM
    cs = jnp.asarray(np.concatenate([np.cos(psi), np.sin(psi)], axis=0), jnp.float32)
    return twc, tws, dmat.astype(jnp.bfloat16), cs.astype(jnp.bfloat16)


def _attn_kernel(q_ref, kt_ref, v_ref, kmt_ref, vm_ref, o_ref, s_ref, pm_ref):
    bf = jnp.bfloat16
    f32 = jnp.float32
    tq = o_ref.shape[1]
    n, _, tk = kt_ref.shape[2:]
    qi = pl.program_id(2)
    row_cur = pl.multiple_of(qi * tq, tq)
    row_nxt = pl.multiple_of(jnp.minimum(qi + 1, pl.num_programs(2) - 1) * tq, tq)

    group = s_ref.shape[2] // tk
    n_groups = n // group

    def produce(slot, row0, g):
        qt = q_ref[0, 0, pl.ds(row0, tq), :]
        pmax = None
        for j in range(group):
            s = jnp.dot(qt, kt_ref[0, 0, g * group + j], preferred_element_type=f32)
            s_ref[slot, :, j * tk:(j + 1) * tk] = s
            for l in range(tk // LANES):
                blk = s[:, l * LANES:(l + 1) * LANES]
                pmax = blk if pmax is None else jnp.maximum(pmax, blk)
        pm_ref[slot] = pmax

    def with_ones(vals):
        return jnp.concatenate([vals, jnp.ones_like(vals)], axis=1)

    def absorb(slot, g, m_i, acc):
        pmax = pm_ref[slot]
        if g == 0:
            lane = lax.broadcasted_iota(jnp.int32, (tq, LANES), 1)
            s_meta = jnp.dot(q_ref[0, 0, pl.ds(row_cur, tq), :], kmt_ref[0, 0], preferred_element_type=f32)
            s_meta = jnp.where(lane < N_META, s_meta, -jnp.inf)
            pmax = jnp.maximum(pmax, s_meta)
        m_new = jnp.max(pmax, axis=1, keepdims=True)
        if g > 0:
            m_new = jnp.maximum(m_i, m_new)
        p = jnp.exp2((s_ref[slot] - m_new).astype(bf))
        pv = jnp.dot(p, with_ones(v_ref[0, 0, g * group * tk:(g + 1) * group * tk, :]), preferred_element_type=f32)
        if g == 0:
            p_meta = jnp.exp2((s_meta - m_new).astype(bf))
            return m_new, pv + jnp.dot(p_meta, with_ones(vm_ref[0, 0]), preferred_element_type=f32)
        return m_new, jnp.exp2(m_i - m_new) * acc + pv

    @pl.when(qi == 0)
    def _():
        produce(0, row_cur, 0)

    m = acc = None
    for g in range(n_groups):
        if g + 1 < n_groups:
            produce((g + 1) % 2, row_cur, g + 1)
        else:
            produce(0, row_nxt, 0)
        m, acc = absorb(g % 2, g, m, acc)
    o_ref[0] = (acc[:, :V_DIM] / acc[:, V_DIM:]).astype(bf)


def _attention(q, kt, v, kmt, vm, tq, tg):
    b, nh, s, _ = q.shape
    n, _, tk = kt.shape[2:]
    assert n * tk == s and tg % tk == 0 and (s // tg) % 2 == 0
    return pl.pallas_call(
        _attn_kernel,
        grid=(b, nh, s // tq),
        in_specs=[
            pl.BlockSpec((1, 1, s, QK_DIM), lambda bi, h, qi: (bi, h, 0, 0)),
            pl.BlockSpec((1, 1, n, QK_DIM, tk), lambda bi, h, qi: (bi, h, 0, 0, 0)),
            pl.BlockSpec((1, 1, s, V_DIM), lambda bi, h, qi: (bi, h, 0, 0)),
            pl.BlockSpec((1, 1, QK_DIM, LANES), lambda bi, h, qi: (0, h, 0, 0)),
            pl.BlockSpec((1, 1, LANES, V_DIM), lambda bi, h, qi: (0, h, 0, 0)),
        ],
        out_specs=pl.BlockSpec((1, tq, V_DIM), lambda bi, h, qi: (bi, qi, h)),
        out_shape=jax.ShapeDtypeStruct((b, s, ATTN_DIM), jnp.bfloat16),
        scratch_shapes=[pltpu.VMEM((2, tq, tg), jnp.float32), pltpu.VMEM((2, tq, LANES), jnp.float32)],
        compiler_params=pltpu.CompilerParams(
            dimension_semantics=("parallel", "parallel", "arbitrary"), vmem_limit_bytes=VMEM_LIMIT_BYTES),
        name="attention",
    )(q, kt, v, kmt, vm)


def _tail_kernel(x_ref, y_ref, o_ref, sa_ref, sb_ref, wfo_ref, wao_ref, wo_ref, g2_ref, wg_ref, wu_ref, wd_ref,
                 gf_ref, out_ref):
    bf = jnp.bfloat16
    f32 = jnp.float32
    yf = jnp.concatenate([y_ref[0, g] for g in range(N_GROUPS)], axis=1).astype(bf)
    ya = jnp.dot(yf, wfo_ref[...], preferred_element_type=f32)
    yb = jnp.dot(o_ref[0], wao_ref[...], preferred_element_type=f32)
    merged = sa_ref[0].astype(f32) * ya + sb_ref[0].astype(f32) * yb
    x1 = x_ref[0] + jnp.dot(merged.astype(bf), wo_ref[...], preferred_element_type=f32)
    h2 = _rms(x1, g2_ref[...]).astype(bf)
    x2 = x1
    for c0 in range(0, D_FF, FF_CHUNK):
        c1 = min(c0 + FF_CHUNK, D_FF)
        gate = jnp.dot(h2, wg_ref[:, c0:c1], preferred_element_type=f32)
        up = jnp.dot(h2, wu_ref[:, c0:c1], preferred_element_type=f32)
        act = (gate * jax.nn.sigmoid(gate) * up).astype(bf)
        x2 = x2 + jnp.dot(act, wd_ref[c0:c1, :], preferred_element_type=f32)
    out_ref[0] = _rms(x2, gf_ref[...])


def _tail(x, y, o, sa, sb, wfo, wao, wo, g2, wg, wu, wd, gf, tm):
    b, s, _ = x.shape
    tok = lambda bi, si: (bi, si, 0)
    const = lambda bi, si: (0, 0)

    def weight(shape):
        return pl.BlockSpec(shape, const, pipeline_mode=pl.Buffered(1))

    return pl.pallas_call(
        _tail_kernel,
        grid=(b, s // tm),
        in_specs=[
            pl.BlockSpec((1, tm, D_MODEL), tok),
            pl.BlockSpec((1, N_GROUPS, tm, GROUP_DIM), lambda bi, si: (bi, 0, si, 0)),
            pl.BlockSpec((1, tm, ATTN_DIM), tok),
            pl.BlockSpec((1, tm, D_MODEL), tok),
            pl.BlockSpec((1, tm, D_MODEL), tok),
            weight((FOURIER_DIM, D_MODEL)),
            weight((ATTN_DIM, D_MODEL)),
            weight((D_MODEL, D_MODEL)),
            pl.BlockSpec((1, D_MODEL), const),
            weight((D_MODEL, D_FF)),
            weight((D_MODEL, D_FF)),
            weight((D_FF, D_MODEL)),
            pl.BlockSpec((1, D_MODEL), const),
        ],
        out_specs=pl.BlockSpec((1, tm, D_MODEL), tok),
        out_shape=jax.ShapeDtypeStruct((b, s, D_MODEL), jnp.float32),
        compiler_params=pltpu.CompilerParams(
            dimension_semantics=("parallel", "parallel"), vmem_limit_bytes=VMEM_LIMIT_BYTES),
        name="tail",
    )(x, y, o, sa, sb, wfo, wao, wo, g2, wg, wu, wd, gf)


def _rope_tables(length):
    inv = 1.0 / (ROPE_THETA ** (jnp.arange(0, ROPE_DIM, 2, dtype=jnp.float32) / ROPE_DIM))
    ang = jnp.arange(length, dtype=jnp.float32)[:, None] * inv[None, :]
    c, s = jnp.cos(ang), jnp.sin(ang)
    return jnp.concatenate([c, c, c, c], axis=1), jnp.concatenate([-s, s, -s, s], axis=1)


def _prepare_weights(norm1_g, w_in, q_norm_g, kv_norm_g, w_uq, w_ukv, w_fourier_out, w_attn_out, w_o, norm2_g,
                     w_ffn_gate, w_ffn_up, w_ffn_down, final_norm_g):
    bf = jnp.bfloat16
    w = w_in[0]
    s_uf, s_cq, s_ckv, s_kr = FOURIER_DIM, FOURIER_DIM + Q_RANK, FOURIER_DIM + Q_RANK + KV_RANK, \
        FOURIER_DIM + Q_RANK + KV_RANK + ROPE_DIM
    win = jnp.concatenate(
        [w[:, :s_ckv], w[:, s_kr:], w[:, s_ckv:s_kr], jnp.zeros((D_MODEL, LANES - ROPE_DIM), w.dtype)], axis=1).astype(bf)
    wq = w_uq[0].reshape(Q_RANK, N_HEADS, QK_DIM)
    wuq = jnp.concatenate([wq[:, :, :NOPE_DIM].reshape(Q_RANK, -1), wq[:, :, NOPE_DIM:].reshape(Q_RANK, -1)], axis=1).astype(bf)
    wkv = w_ukv[0].reshape(KV_RANK, N_HEADS, NOPE_DIM + V_DIM)
    wkt = wkv[:, :, :NOPE_DIM].reshape(KV_RANK, -1).T.astype(bf)
    wv = wkv[:, :, NOPE_DIM:].reshape(KV_RANK, -1).astype(bf)
    row = lambda g: g.reshape(1, -1).astype(jnp.float32)
    return dict(
        g1=row(norm1_g[0]), win=win, qg=row(q_norm_g[0]), kvg=row(kv_norm_g[0]), wuq=wuq, wkt=wkt, wv=wv,
        wfo=w_fourier_out[0].astype(bf), wao=w_attn_out[0].astype(bf), wo=w_o[0].astype(bf), g2=row(norm2_g[0]),
        wg=w_ffn_gate[0].astype(bf), wu=w_ffn_up[0].astype(bf), wd=w_ffn_down[0].astype(bf), gf=row(final_norm_g))


def _project(p, x, cos2, sin2, tm):
    return _inproj(x, p["g1"], p["win"], p["qg"], p["kvg"], p["wuq"], p["wkt"], p["wv"], cos2, sin2, tm)


def _meta_projection(meta, p, cos2, sin2):
    meta_pad = jnp.zeros((1, LANES, D_MODEL), meta.dtype).at[0, :N_META].set(meta)
    uf_m, _, kt_m, v_m, _, _ = _project(p, meta_pad, cos2[:LANES], sin2[:LANES], LANES)
    return uf_m[0, :N_META], kt_m[:, :, 0], v_m


def _trunk(x, meta_proj, rope, p, tk, tq, tg, tm_tail):
    s = x.shape[1]
    uf_m, kt_m, v_m = meta_proj
    cos2, sin2 = rope
    uf, q, kt, v, sa, sb = _project(p, x, cos2[N_META:N_META + s], sin2[N_META:N_META + s], tk)
    y = _fourier(uf, uf_m, *_fourier_tables(s))
    o = _attention(q, kt, v, kt_m, v_m, tq, tg)
    return _tail(x, y, o, sa, sb, p["wfo"], p["wao"], p["wo"], p["g2"], p["wg"], p["wu"], p["wd"], p["gf"], tm_tail)


def kernel(x_prompt, x_sample, meta_tokens, norm1_g, w_in, q_norm_g, kv_norm_g, w_uq, w_ukv, w_fourier_out,
           w_attn_out, w_o, norm2_g, w_ffn_gate, w_ffn_up, w_ffn_down, final_norm_g):
    p = _prepare_weights(norm1_g, w_in, q_norm_g, kv_norm_g, w_uq, w_ukv, w_fourier_out, w_attn_out, w_o, norm2_g,
                         w_ffn_gate, w_ffn_up, w_ffn_down, final_norm_g)
    rope = _rope_tables(max(x_prompt.shape[1], x_sample.shape[1]) + N_META)
    meta_proj = _meta_projection(meta_tokens, p, *rope)
    cfg = dict(tk=512, tq=1024, tg=1024, tm_tail=512)
    y_prompt = _trunk(x_prompt, meta_proj, rope, p, **cfg)
    y_sample = _trunk(x_sample, meta_proj, rope, p, **cfg)
    return (y_prompt, y_sample)
```

```python
import functools
import math

import jax
import jax.numpy as jnp
import numpy as np
from jax import lax
from jax.experimental import pallas as pl
from jax.experimental.pallas import tpu as pltpu

D_MODEL = 1024
N_META = 16
N_GROUPS = 4
GROUP_DIM = 128
FOURIER_DIM = N_GROUPS * GROUP_DIM
N_HEADS = 8
NOPE_DIM = 128
ROPE_DIM = 64
QK_DIM = NOPE_DIM + ROPE_DIM
V_DIM = 128
Q_RANK = 512
KV_RANK = 256
ATTN_DIM = N_HEADS * V_DIM
D_FF = 2816
ROPE_THETA = 10000.0
NORM_EPS = 1e-6
ATTN_SCALE = QK_DIM ** -0.5
LOG2_E = math.log2(math.e)
DFT_RADIX = 16

LANES = 128
VMEM_LIMIT_BYTES = 56 * 1024 * 1024
DFT_K1_BATCH = 4
FF_CHUNK = 1024

_C_UF = 0
_C_CQ = _C_UF + FOURIER_DIM
_C_CKV = _C_CQ + Q_RANK
_C_GA = _C_CKV + KV_RANK
_C_GB = _C_GA + D_MODEL
_C_KR = _C_GB + D_MODEL
IN_COLS = _C_KR + LANES


def _rms(x, g):
    return x * lax.rsqrt(jnp.mean(x * x, axis=-1, keepdims=True) + NORM_EPS) * g


def _rope128(x, cos2, sin2):
    lane = lax.broadcasted_iota(jnp.int32, x.shape, 1)
    first_half = (lane % ROPE_DIM) < (ROPE_DIM // 2)
    partner = jnp.where(first_half, pltpu.roll(x, LANES - ROPE_DIM // 2, 1), pltpu.roll(x, ROPE_DIM // 2, 1))
    return x * cos2 + partner * sin2


def _inproj_kernel(x_ref, g1_ref, win_ref, qg_ref, kvg_ref, wuq_ref, wkt_ref, wv_ref, cos_ref, sin_ref,
                   uf_ref, q_ref, kt_ref, v_ref, sa_ref, sb_ref):
    bf = jnp.bfloat16
    f32 = jnp.float32
    h = _rms(x_ref[0], g1_ref[...]).astype(bf)

    def proj(c0, width):
        return jnp.dot(h, win_ref[:, c0:c0 + width], preferred_element_type=f32)

    cos2 = cos_ref[...]
    sin2 = sin_ref[...]

    ckv_raw = proj(_C_CKV, KV_RANK)
    kr_raw = proj(_C_KR, LANES)
    cq_raw = proj(_C_CQ, Q_RANK)

    ckv = _rms(ckv_raw, kvg_ref[...])
    ckv_t = ckv.T.astype(bf)
    k_t = jnp.dot(wkt_ref[...], ckv_t, preferred_element_type=f32)
    vals = jnp.dot(ckv.astype(bf), wv_ref[...], preferred_element_type=f32)
    cq = _rms(cq_raw, qg_ref[...]).astype(bf)
    q = jnp.dot(cq, wuq_ref[...], preferred_element_type=f32) * (ATTN_SCALE * LOG2_E)

    kr_t = _rope128(kr_raw, cos2, sin2).T[:ROPE_DIM].astype(bf)
    for hd in range(N_HEADS):
        kt_ref[0, hd, 0, :NOPE_DIM, :] = k_t[hd * NOPE_DIM:(hd + 1) * NOPE_DIM].astype(bf)
        kt_ref[0, hd, 0, NOPE_DIM:, :] = kr_t
        v_ref[0, hd] = vals[:, hd * V_DIM:(hd + 1) * V_DIM].astype(bf)

    rope0 = N_HEADS * NOPE_DIM
    for pair in range(N_HEADS // 2):
        qr = _rope128(q[:, rope0 + pair * LANES:rope0 + (pair + 1) * LANES], cos2, sin2).astype(bf)
        for sub in range(2):
            hd = 2 * pair + sub
            q_ref[0, hd, :, :NOPE_DIM] = q[:, hd * NOPE_DIM:(hd + 1) * NOPE_DIM].astype(bf)
            q_ref[0, hd, :, NOPE_DIM:] = qr[:, sub * ROPE_DIM:(sub + 1) * ROPE_DIM]

    uf_ref[0] = proj(_C_UF, FOURIER_DIM)
    sa_ref[0] = jax.nn.sigmoid(proj(_C_GA, D_MODEL)).astype(bf)
    sb_ref[0] = jax.nn.sigmoid(proj(_C_GB, D_MODEL)).astype(bf)


def _const_spec(shape):
    return pl.BlockSpec(shape, lambda *_: (0,) * len(shape), pipeline_mode=pl.Buffered(1))


def _inproj(x, g1, win, qg, kvg, wuq, wkt, wv, cos2, sin2, tm):
    b, s, _ = x.shape
    tok = lambda bi, si: (bi, si, 0)
    head = lambda bi, si: (bi, 0, si, 0)
    bf = jnp.bfloat16
    return pl.pallas_call(
        _inproj_kernel,
        grid=(b, s // tm),
        in_specs=[
            pl.BlockSpec((1, tm, D_MODEL), tok),
            _const_spec((1, D_MODEL)),
            _const_spec((D_MODEL, IN_COLS)),
            _const_spec((1, Q_RANK)),
            _const_spec((1, KV_RANK)),
            _const_spec((Q_RANK, N_HEADS * QK_DIM)),
            _const_spec((N_HEADS * NOPE_DIM, KV_RANK)),
            _const_spec((KV_RANK, N_HEADS * V_DIM)),
            pl.BlockSpec((tm, LANES), lambda bi, si: (si, 0)),
            pl.BlockSpec((tm, LANES), lambda bi, si: (si, 0)),
        ],
        out_specs=[
            pl.BlockSpec((1, tm, FOURIER_DIM), tok),
            pl.BlockSpec((1, N_HEADS, tm, QK_DIM), head),
            pl.BlockSpec((1, N_HEADS, 1, QK_DIM, tm), lambda bi, si: (bi, 0, si, 0, 0)),
            pl.BlockSpec((1, N_HEADS, tm, V_DIM), head),
            pl.BlockSpec((1, tm, D_MODEL), tok),
            pl.BlockSpec((1, tm, D_MODEL), tok),
        ],
        out_shape=[
            jax.ShapeDtypeStruct((b, s, FOURIER_DIM), jnp.float32),
            jax.ShapeDtypeStruct((b, N_HEADS, s, QK_DIM), bf),
            jax.ShapeDtypeStruct((b, N_HEADS, s // tm, QK_DIM, tm), bf),
            jax.ShapeDtypeStruct((b, N_HEADS, s, V_DIM), bf),
            jax.ShapeDtypeStruct((b, s, D_MODEL), bf),
            jax.ShapeDtypeStruct((b, s, D_MODEL), bf),
        ],
        compiler_params=pltpu.CompilerParams(
            dimension_semantics=("parallel", "parallel"), vmem_limit_bytes=VMEM_LIMIT_BYTES),
        name="inproj",
    )(x, g1, win, qg, kvg, wuq, wkt, wv, cos2, sin2)


def _cneg(v):
    return None if v is None else -v


def _cadd(a, b):
    if a is None:
        return b
    if b is None:
        return a
    return a + b


def _csub(a, b):
    if b is None:
        return a
    if a is None:
        return -b
    return a - b


def _fft(xs):
    n = len(xs)
    if n == 1:
        return xs
    ev = _fft(xs[0::2])
    od = _fft(xs[1::2])
    out = [None] * n
    for k in range(n // 2):
        re, im = od[k]
        if k == 0:
            tr, ti = re, im
        elif 4 * k == n:
            tr, ti = im, _cneg(re)
        else:
            c = math.cos(2 * math.pi * k / n)
            s = math.sin(2 * math.pi * k / n)
            tr = _cadd(None if re is None else re * c, None if im is None else im * s)
            ti = _csub(None if im is None else im * c, None if re is None else re * s)
        out[k] = (_cadd(ev[k][0], tr), _cadd(ev[k][1], ti))
        out[k + n // 2] = (_csub(ev[k][0], tr), _csub(ev[k][1], ti))
    return out


def _dft16_real(rows):
    out = _fft([(r, None) for r in rows])
    zero = jnp.zeros_like(rows[0])
    out = [(zero if re is None else re, zero if im is None else im) for re, im in out]
    for k in range(DFT_RADIX // 2 + 1, DFT_RADIX):
        re, im = out[DFT_RADIX - k]
        out[k] = (re, -im)
    return out


def _fourier_kernel(n2, xr_ref, xm_ref, twc_ref, tws_ref, dmat_ref, cs_ref, y_ref, x0_ref, z_ref, z0_ref):
    m = n2 - 1
    bf = jnp.bfloat16
    x0_ref[0:N_META, :] = xm_ref[...]
    x0_ref[N_META:N_META + m, :] = xr_ref[0, 0:m, :]

    def run_start(j1):
        return j1 * n2 - N_META

    dc = _dft16_real([x0_ref[0:1, :]] + [xr_ref[0, run_start(j1):run_start(j1) + 1, :] for j1 in range(1, DFT_RADIX)])
    zw = 2 * GROUP_DIM
    for k1 in range(DFT_RADIX):
        z0_ref[0:1, k1 * zw:k1 * zw + GROUP_DIM] = dc[k1][0]
        z0_ref[0:1, k1 * zw + GROUP_DIM:(k1 + 1) * zw] = dc[k1][1]

    def chunk(c, carry):
        r0 = pl.multiple_of(c * 8, 8)
        rows = [x0_ref[pl.ds(r0 + 1, 8), :]]
        rows += [xr_ref[0, pl.ds(r0 + (run_start(j1) + 1), 8), :] for j1 in range(1, DFT_RADIX)]
        a = _dft16_real(rows)
        for k1 in range(DFT_RADIX):
            ar, ai = a[k1]
            if k1 == 0:
                zr, zi = ar, ai
            else:
                tc = twc_ref[k1, pl.ds(r0, 8), :]
                ts = tws_ref[k1, pl.ds(r0, 8), :]
                zr = tc * ar + ts * ai
                zi = tc * ai - ts * ar
            z_ref[pl.ds(r0, 8), k1 * zw:k1 * zw + GROUP_DIM] = zr
            z_ref[pl.ds(r0, 8), k1 * zw + GROUP_DIM:(k1 + 1) * zw] = zi
            z_ref[pl.ds(r0 + m, 8), k1 * zw:k1 * zw + GROUP_DIM] = zi
            z_ref[pl.ds(r0 + m, 8), k1 * zw + GROUP_DIM:(k1 + 1) * zw] = -zr
        return carry

    lax.fori_loop(0, m // 8, chunk, 0)

    scale = 1.0 / math.sqrt(GROUP_DIM * DFT_RADIX * n2)
    for k0 in range(0, DFT_RADIX, DFT_K1_BATCH):
        cols = slice(k0 * zw, (k0 + DFT_K1_BATCH) * zw)
        p = jnp.dot(dmat_ref[...], z_ref[:, cols].astype(bf), preferred_element_type=jnp.float32)
        p = (p + z0_ref[0:1, cols]).astype(bf)
        stacked = jnp.concatenate([p[:, i * zw:(i + 1) * zw] for i in range(DFT_K1_BATCH)], axis=0)
        y = jnp.dot(stacked, cs_ref[...], preferred_element_type=jnp.float32) * scale
        for i in range(DFT_K1_BATCH):
            y_ref[0, 0, pl.ds(k0 + i, m, stride=DFT_RADIX), :] = y[i * m:(i + 1) * m]


def _fourier(uf, uf_meta, twc, tws, dmat, cs):
    b, s, _ = uf.shape
    n2 = (s + N_META) // DFT_RADIX
    m = n2 - 1
    out = pl.pallas_call(
        functools.partial(_fourier_kernel, n2),
        grid=(b, N_GROUPS),
        in_specs=[
            pl.BlockSpec((1, s, GROUP_DIM), lambda bi, g: (bi, 0, g)),
            pl.BlockSpec((N_META, GROUP_DIM), lambda bi, g: (0, g)),
            pl.BlockSpec((DFT_RADIX, m, GROUP_DIM), lambda bi, g: (0, 0, 0)),
            pl.BlockSpec((DFT_RADIX, m, GROUP_DIM), lambda bi, g: (0, 0, 0)),
            pl.BlockSpec((m, 2 * m), lambda bi, g: (0, 0)),
            pl.BlockSpec((2 * GROUP_DIM, GROUP_DIM), lambda bi, g: (0, 0)),
        ],
        out_specs=pl.BlockSpec((1, 1, s, GROUP_DIM), lambda bi, g: (bi, g, 0, 0)),
        out_shape=jax.ShapeDtypeStruct((b, N_GROUPS, s, GROUP_DIM), jnp.float32),
        scratch_shapes=[
            pltpu.VMEM((N_META + m, GROUP_DIM), jnp.float32),
            pltpu.VMEM((2 * m, DFT_RADIX * 2 * GROUP_DIM), jnp.float32),
            pltpu.VMEM((8, DFT_RADIX * 2 * GROUP_DIM), jnp.float32),
        ],
        compiler_params=pltpu.CompilerParams(
            dimension_semantics=("parallel", "parallel"), vmem_limit_bytes=VMEM_LIMIT_BYTES),
        name="fourier",
    )(uf, uf_meta, twc, tws, dmat, cs)
    return out


def _fourier_tables(s):
    n2 = (s + N_META) // DFT_RADIX
    length = DFT_RADIX * n2
    j2 = np.arange(1, n2, dtype=np.int64)
    k1 = np.arange(DFT_RADIX, dtype=np.int64)
    phi = 2.0 * np.pi * ((k1[:, None] * j2[None, :]) % length) / length
    twc = jnp.broadcast_to(jnp.asarray(np.cos(phi), jnp.float32)[:, :, None], (DFT_RADIX, n2 - 1, GROUP_DIM))
    tws = jnp.broadcast_to(jnp.asarray(np.sin(phi), jnp.float32)[:, :, None], (DFT_RADIX, n2 - 1, GROUP_DIM))
    theta = 2.0 * np.pi * ((j2[:, None] * j2[None, :]) % n2) / n2
    dmat = jnp.asarray(np.concatenate([np.cos(theta), np.sin(theta)], axis=1), jnp.float32)
    c = np.arange(GROUP_DIM, dtype=np.int64)
    psi = 2.0 * np.pi * ((c[:, None] * c[None, :]) % GROUP_DIM) / GROUP_DIM
    cs = jnp.asarray(np.concatenate([np.cos(psi), np.sin(psi)], axis=0), jnp.float32)
    return twc, tws, dmat.astype(jnp.bfloat16), cs.astype(jnp.bfloat16)


def _attn_kernel(q_ref, kt_ref, v_ref, kmt_ref, vm_ref, o_ref, s_ref, pm_ref):
    bf = jnp.bfloat16
    f32 = jnp.float32
    tq = o_ref.shape[1]
    n, _, tk = kt_ref.shape[2:]
    qi = pl.program_id(2)
    row_cur = pl.multiple_of(qi * tq, tq)
    row_nxt = pl.multiple_of(jnp.minimum(qi + 1, pl.num_programs(2) - 1) * tq, tq)

    group = s_ref.shape[2] // tk
    n_groups = n // group

    def produce(slot, row0, g):
        qt = q_ref[0, 0, pl.ds(row0, tq), :]
        pmax = None
        for j in range(group):
            s = jnp.dot(qt, kt_ref[0, 0, g * group + j], preferred_element_type=f32)
            s_ref[slot, :, j * tk:(j + 1) * tk] = s
            for l in range(tk // LANES):
                blk = s[:, l * LANES:(l + 1) * LANES]
                pmax = blk if pmax is None else jnp.maximum(pmax, blk)
        pm_ref[slot] = pmax

    def with_ones(vals):
        return jnp.concatenate([vals, jnp.ones_like(vals)], axis=1)

    def absorb(slot, g, m_i, acc):
        pmax = pm_ref[slot]
        if g == 0:
            lane = lax.broadcasted_iota(jnp.int32, (tq, LANES), 1)
            s_meta = jnp.dot(q_ref[0, 0, pl.ds(row_cur, tq), :], kmt_ref[0, 0], preferred_element_type=f32)
            s_meta = jnp.where(lane < N_META, s_meta, -jnp.inf)
            pmax = jnp.maximum(pmax, s_meta)
        m_new = jnp.max(pmax, axis=1, keepdims=True)
        if g > 0:
            m_new = jnp.maximum(m_i, m_new)
        p = jnp.exp2((s_ref[slot] - m_new).astype(bf))
        pv = jnp.dot(p, with_ones(v_ref[0, 0, g * group * tk:(g + 1) * group * tk, :]), preferred_element_type=f32)
        if g == 0:
            p_meta = jnp.exp2((s_meta - m_new).astype(bf))
            return m_new, pv + jnp.dot(p_meta, with_ones(vm_ref[0, 0]), preferred_element_type=f32)
        return m_new, jnp.exp2(m_i - m_new) * acc + pv

    @pl.when(qi == 0)
    def _():
        produce(0, row_cur, 0)

    m = acc = None
    for g in range(n_groups):
        if g + 1 < n_groups:
            produce((g + 1) % 2, row_cur, g + 1)
        else:
            produce(0, row_nxt, 0)
        m, acc = absorb(g % 2, g, m, acc)
    o_ref[0] = (acc[:, :V_DIM] / acc[:, V_DIM:]).astype(bf)


def _attention(q, kt, v, kmt, vm, tq, tg):
    b, nh, s, _ = q.shape
    n, _, tk = kt.shape[2:]
    assert n * tk == s and tg % tk == 0 and (s // tg) % 2 == 0
    return pl.pallas_call(
        _attn_kernel,
        grid=(b, nh, s // tq),
        in_specs=[
            pl.BlockSpec((1, 1, s, QK_DIM), lambda bi, h, qi: (bi, h, 0, 0)),
            pl.BlockSpec((1, 1, n, QK_DIM, tk), lambda bi, h, qi: (bi, h, 0, 0, 0)),
            pl.BlockSpec((1, 1, s, V_DIM), lambda bi, h, qi: (bi, h, 0, 0)),
            pl.BlockSpec((1, 1, QK_DIM, LANES), lambda bi, h, qi: (0, h, 0, 0)),
            pl.BlockSpec((1, 1, LANES, V_DIM), lambda bi, h, qi: (0, h, 0, 0)),
        ],
        out_specs=pl.BlockSpec((1, tq, V_DIM), lambda bi, h, qi: (bi, qi, h)),
        out_shape=jax.ShapeDtypeStruct((b, s, ATTN_DIM), jnp.bfloat16),
        scratch_shapes=[pltpu.VMEM((2, tq, tg), jnp.float32), pltpu.VMEM((2, tq, LANES), jnp.float32)],
        compiler_params=pltpu.CompilerParams(
            dimension_semantics=("parallel", "parallel", "arbitrary"), vmem_limit_bytes=VMEM_LIMIT_BYTES),
        name="attention",
    )(q, kt, v, kmt, vm)


def _tail_kernel(x_ref, y_ref, o_ref, sa_ref, sb_ref, wfo_ref, wao_ref, wo_ref, g2_ref, wg_ref, wu_ref, wd_ref,
                 gf_ref, out_ref):
    bf = jnp.bfloat16
    f32 = jnp.float32
    yf = jnp.concatenate([y_ref[0, g] for g in range(N_GROUPS)], axis=1).astype(bf)
    ya = jnp.dot(yf, wfo_ref[...], preferred_element_type=f32)
    yb = jnp.dot(o_ref[0], wao_ref[...], preferred_element_type=f32)
    merged = sa_ref[0].astype(f32) * ya + sb_ref[0].astype(f32) * yb
    x1 = x_ref[0] + jnp.dot(merged.astype(bf), wo_ref[...], preferred_element_type=f32)
    h2 = _rms(x1, g2_ref[...]).astype(bf)
    x2 = x1
    for c0 in range(0, D_FF, FF_CHUNK):
        c1 = min(c0 + FF_CHUNK, D_FF)
        gate = jnp.dot(h2, wg_ref[:, c0:c1], preferred_element_type=f32)
        up = jnp.dot(h2, wu_ref[:, c0:c1], preferred_element_type=f32)
        act = (gate * jax.nn.sigmoid(gate) * up).astype(bf)
        x2 = x2 + jnp.dot(act, wd_ref[c0:c1, :], preferred_element_type=f32)
    out_ref[0] = _rms(x2, gf_ref[...])


def _tail(x, y, o, sa, sb, wfo, wao, wo, g2, wg, wu, wd, gf, tm):
    b, s, _ = x.shape
    tok = lambda bi, si: (bi, si, 0)
    const = lambda bi, si: (0, 0)

    def weight(shape):
        return pl.BlockSpec(shape, const, pipeline_mode=pl.Buffered(1))

    return pl.pallas_call(
        _tail_kernel,
        grid=(b, s // tm),
        in_specs=[
            pl.BlockSpec((1, tm, D_MODEL), tok),
            pl.BlockSpec((1, N_GROUPS, tm, GROUP_DIM), lambda bi, si: (bi, 0, si, 0)),
            pl.BlockSpec((1, tm, ATTN_DIM), tok),
            pl.BlockSpec((1, tm, D_MODEL), tok),
            pl.BlockSpec((1, tm, D_MODEL), tok),
            weight((FOURIER_DIM, D_MODEL)),
            weight((ATTN_DIM, D_MODEL)),
            weight((D_MODEL, D_MODEL)),
            pl.BlockSpec((1, D_MODEL), const),
            weight((D_MODEL, D_FF)),
            weight((D_MODEL, D_FF)),
            weight((D_FF, D_MODEL)),
            pl.BlockSpec((1, D_MODEL), const),
        ],
        out_specs=pl.BlockSpec((1, tm, D_MODEL), tok),
        out_shape=jax.ShapeDtypeStruct((b, s, D_MODEL), jnp.float32),
        compiler_params=pltpu.CompilerParams(
            dimension_semantics=("parallel", "parallel"), vmem_limit_bytes=VMEM_LIMIT_BYTES),
        name="tail",
    )(x, y, o, sa, sb, wfo, wao, wo, g2, wg, wu, wd, gf)


def _rope_tables(length):
    inv = 1.0 / (ROPE_THETA ** (jnp.arange(0, ROPE_DIM, 2, dtype=jnp.float32) / ROPE_DIM))
    ang = jnp.arange(length, dtype=jnp.float32)[:, None] * inv[None, :]
    c, s = jnp.cos(ang), jnp.sin(ang)
    return jnp.concatenate([c, c, c, c], axis=1), jnp.concatenate([-s, s, -s, s], axis=1)


def _prepare_weights(norm1_g, w_in, q_norm_g, kv_norm_g, w_uq, w_ukv, w_fourier_out, w_attn_out, w_o, norm2_g,
                     w_ffn_gate, w_ffn_up, w_ffn_down, final_norm_g):
    bf = jnp.bfloat16
    w = w_in[0]
    s_uf, s_cq, s_ckv, s_kr = FOURIER_DIM, FOURIER_DIM + Q_RANK, FOURIER_DIM + Q_RANK + KV_RANK, \
        FOURIER_DIM + Q_RANK + KV_RANK + ROPE_DIM
    win = jnp.concatenate(
        [w[:, :s_ckv], w[:, s_kr:], w[:, s_ckv:s_kr], jnp.zeros((D_MODEL, LANES - ROPE_DIM), w.dtype)], axis=1).astype(bf)
    wq = w_uq[0].reshape(Q_RANK, N_HEADS, QK_DIM)
    wuq = jnp.concatenate([wq[:, :, :NOPE_DIM].reshape(Q_RANK, -1), wq[:, :, NOPE_DIM:].reshape(Q_RANK, -1)], axis=1).astype(bf)
    wkv = w_ukv[0].reshape(KV_RANK, N_HEADS, NOPE_DIM + V_DIM)
    wkt = wkv[:, :, :NOPE_DIM].reshape(KV_RANK, -1).T.astype(bf)
    wv = wkv[:, :, NOPE_DIM:].reshape(KV_RANK, -1).astype(bf)
    row = lambda g: g.reshape(1, -1).astype(jnp.float32)
    return dict(
        g1=row(norm1_g[0]), win=win, qg=row(q_norm_g[0]), kvg=row(kv_norm_g[0]), wuq=wuq, wkt=wkt, wv=wv,
        wfo=w_fourier_out[0].astype(bf), wao=w_attn_out[0].astype(bf), wo=w_o[0].astype(bf), g2=row(norm2_g[0]),
        wg=w_ffn_gate[0].astype(bf), wu=w_ffn_up[0].astype(bf), wd=w_ffn_down[0].astype(bf), gf=row(final_norm_g))


def _project(p, x, cos2, sin2, tm):
    return _inproj(x, p["g1"], p["win"], p["qg"], p["kvg"], p["wuq"], p["wkt"], p["wv"], cos2, sin2, tm)


def _meta_projection(meta, p, cos2, sin2):
    meta_pad = jnp.zeros((1, LANES, D_MODEL), meta.dtype).at[0, :N_META].set(meta)
    uf_m, _, kt_m, v_m, _, _ = _project(p, meta_pad, cos2[:LANES], sin2[:LANES], LANES)
    return uf_m[0, :N_META], kt_m[:, :, 0], v_m


def _trunk(x, meta_proj, rope, p, tk, tq, tg, tm_tail):
    s = x.shape[1]
    uf_m, kt_m, v_m = meta_proj
    cos2, sin2 = rope
    uf, q, kt, v, sa, sb = _project(p, x, cos2[N_META:N_META + s], sin2[N_META:N_META + s], tk)
    y = _fourier(uf, uf_m, *_fourier_tables(s))
    o = _attention(q, kt, v, kt_m, v_m, tq, tg)
    return _tail(x, y, o, sa, sb, p["wfo"], p["wao"], p["wo"], p["g2"], p["wg"], p["wu"], p["wd"], p["gf"], tm_tail)


def kernel(x_prompt, x_sample, meta_tokens, norm1_g, w_in, q_norm_g, kv_norm_g, w_uq, w_ukv, w_fourier_out,
           w_attn_out, w_o, norm2_g, w_ffn_gate, w_ffn_up, w_ffn_down, final_norm_g):
    p = _prepare_weights(norm1_g, w_in, q_norm_g, kv_norm_g, w_uq, w_ukv, w_fourier_out, w_attn_out, w_o, norm2_g,
                         w_ffn_gate, w_ffn_up, w_ffn_down, final_norm_g)
    rope = _rope_tables(max(x_prompt.shape[1], x_sample.shape[1]) + N_META)
    meta_proj = _meta_projection(meta_tokens, p, *rope)
    cfg = dict(tk=512, tq=1024, tg=1024, tm_tail=512)
    y_prompt = _trunk(x_prompt, meta_proj, rope, p, **cfg)
    y_sample = _trunk(x_sample, meta_proj, rope, p, **cfg)
    return (y_prompt, y_sample)
```

```python
import functools
import math

import jax
import jax.numpy as jnp
import numpy as np
from jax import lax
from jax.experimental import pallas as pl
from jax.experimental.pallas import tpu as pltpu

D_MODEL = 1024
N_META = 16
N_GROUPS = 4
GROUP_DIM = 128
FOURIER_DIM = N_GROUPS * GROUP_DIM
N_HEADS = 8
NOPE_DIM = 128
ROPE_DIM = 64
QK_DIM = NOPE_DIM + ROPE_DIM
V_DIM = 128
Q_RANK = 512
KV_RANK = 256
ATTN_DIM = N_HEADS * V_DIM
D_FF = 2816
ROPE_THETA = 10000.0
NORM_EPS = 1e-6
ATTN_SCALE = QK_DIM ** -0.5
LOG2_E = math.log2(math.e)
DFT_RADIX = 16

LANES = 128
VMEM_LIMIT_BYTES = 56 * 1024 * 1024
DFT_K1_BATCH = 4
FF_CHUNK = 1024

_C_UF = 0
_C_CQ = _C_UF + FOURIER_DIM
_C_CKV = _C_CQ + Q_RANK
_C_GA = _C_CKV + KV_RANK
_C_GB = _C_GA + D_MODEL
_C_KR = _C_GB + D_MODEL
IN_COLS = _C_KR + LANES


def _rms(x, g):
    return x * lax.rsqrt(jnp.mean(x * x, axis=-1, keepdims=True) + NORM_EPS) * g


def _rope128(x, cos2, sin2):
    lane = lax.broadcasted_iota(jnp.int32, x.shape, 1)
    first_half = (lane % ROPE_DIM) < (ROPE_DIM // 2)
    partner = jnp.where(first_half, pltpu.roll(x, LANES - ROPE_DIM // 2, 1), pltpu.roll(x, ROPE_DIM // 2, 1))
    return x * cos2 + partner * sin2


def _inproj_kernel(x_ref, g1_ref, win_ref, qg_ref, kvg_ref, wuq_ref, wkt_ref, wv_ref, cos_ref, sin_ref,
                   uf_ref, q_ref, kt_ref, v_ref, sa_ref, sb_ref):
    bf = jnp.bfloat16
    f32 = jnp.float32
    h = _rms(x_ref[0], g1_ref[...]).astype(bf)

    def proj(c0, width):
        return jnp.dot(h, win_ref[:, c0:c0 + width], preferred_element_type=f32)

    cos2 = cos_ref[...]
    sin2 = sin_ref[...]

    ckv_raw = proj(_C_CKV, KV_RANK)
    kr_raw = proj(_C_KR, LANES)
    cq_raw = proj(_C_CQ, Q_RANK)

    ckv = _rms(ckv_raw, kvg_ref[...])
    ckv_t = ckv.T.astype(bf)
    k_t = jnp.dot(wkt_ref[...], ckv_t, preferred_element_type=f32)
    vals = jnp.dot(ckv.astype(bf), wv_ref[...], preferred_element_type=f32)
    cq = _rms(cq_raw, qg_ref[...]).astype(bf)
    q = jnp.dot(cq, wuq_ref[...], preferred_element_type=f32) * (ATTN_SCALE * LOG2_E)

    kr_t = _rope128(kr_raw, cos2, sin2).T[:ROPE_DIM].astype(bf)
    for hd in range(N_HEADS):
        kt_ref[0, hd, 0, :NOPE_DIM, :] = k_t[hd * NOPE_DIM:(hd + 1) * NOPE_DIM].astype(bf)
        kt_ref[0, hd, 0, NOPE_DIM:, :] = kr_t
        v_ref[0, hd] = vals[:, hd * V_DIM:(hd + 1) * V_DIM].astype(bf)

    rope0 = N_HEADS * NOPE_DIM
    for pair in range(N_HEADS // 2):
        qr = _rope128(q[:, rope0 + pair * LANES:rope0 + (pair + 1) * LANES], cos2, sin2).astype(bf)
        for sub in range(2):
            hd = 2 * pair + sub
            q_ref[0, hd, :, :NOPE_DIM] = q[:, hd * NOPE_DIM:(hd + 1) * NOPE_DIM].astype(bf)
            q_ref[0, hd, :, NOPE_DIM:] = qr[:, sub * ROPE_DIM:(sub + 1) * ROPE_DIM]

    uf_ref[0] = proj(_C_UF, FOURIER_DIM)
    sa_ref[0] = jax.nn.sigmoid(proj(_C_GA, D_MODEL)).astype(bf)
    sb_ref[0] = jax.nn.sigmoid(proj(_C_GB, D_MODEL)).astype(bf)


def _const_spec(shape):
    return pl.BlockSpec(shape, lambda *_: (0,) * len(shape), pipeline_mode=pl.Buffered(1))


def _inproj(x, g1, win, qg, kvg, wuq, wkt, wv, cos2, sin2, tm):
    b, s, _ = x.shape
    tok = lambda bi, si: (bi, si, 0)
    head = lambda bi, si: (bi, 0, si, 0)
    bf = jnp.bfloat16
    return pl.pallas_call(
        _inproj_kernel,
        grid=(b, s // tm),
        in_specs=[
            pl.BlockSpec((1, tm, D_MODEL), tok),
            _const_spec((1, D_MODEL)),
            _const_spec((D_MODEL, IN_COLS)),
            _const_spec((1, Q_RANK)),
            _const_spec((1, KV_RANK)),
            _const_spec((Q_RANK, N_HEADS * QK_DIM)),
            _const_spec((N_HEADS * NOPE_DIM, KV_RANK)),
            _const_spec((KV_RANK, N_HEADS * V_DIM)),
            pl.BlockSpec((tm, LANES), lambda bi, si: (si, 0)),
            pl.BlockSpec((tm, LANES), lambda bi, si: (si, 0)),
        ],
        out_specs=[
            pl.BlockSpec((1, tm, FOURIER_DIM), tok),
            pl.BlockSpec((1, N_HEADS, tm, QK_DIM), head),
            pl.BlockSpec((1, N_HEADS, 1, QK_DIM, tm), lambda bi, si: (bi, 0, si, 0, 0)),
            pl.BlockSpec((1, N_HEADS, tm, V_DIM), head),
            pl.BlockSpec((1, tm, D_MODEL), tok),
            pl.BlockSpec((1, tm, D_MODEL), tok),
        ],
        out_shape=[
            jax.ShapeDtypeStruct((b, s, FOURIER_DIM), jnp.float32),
            jax.ShapeDtypeStruct((b, N_HEADS, s, QK_DIM), bf),
            jax.ShapeDtypeStruct((b, N_HEADS, s // tm, QK_DIM, tm), bf),
            jax.ShapeDtypeStruct((b, N_HEADS, s, V_DIM), bf),
            jax.ShapeDtypeStruct((b, s, D_MODEL), bf),
            jax.ShapeDtypeStruct((b, s, D_MODEL), bf),
        ],
        compiler_params=pltpu.CompilerParams(
            dimension_semantics=("parallel", "parallel"), vmem_limit_bytes=VMEM_LIMIT_BYTES),
        name="inproj",
    )(x, g1, win, qg, kvg, wuq, wkt, wv, cos2, sin2)


def _cneg(v):
    return None if v is None else -v


def _cadd(a, b):
    if a is None:
        return b
    if b is None:
        return a
    return a + b


def _csub(a, b):
    if b is None:
        return a
    if a is None:
        return -b
    return a - b


def _fft(xs):
    n = len(xs)
    if n == 1:
        return xs
    ev = _fft(xs[0::2])
    od = _fft(xs[1::2])
    out = [None] * n
    for k in range(n // 2):
        re, im = od[k]
        if k == 0:
            tr, ti = re, im
        elif 4 * k == n:
            tr, ti = im, _cneg(re)
        else:
            c = math.cos(2 * math.pi * k / n)
            s = math.sin(2 * math.pi * k / n)
            tr = _cadd(None if re is None else re * c, None if im is None else im * s)
            ti = _csub(None if im is None else im * c, None if re is None else re * s)
        out[k] = (_cadd(ev[k][0], tr), _cadd(ev[k][1], ti))
        out[k + n // 2] = (_csub(ev[k][0], tr), _csub(ev[k][1], ti))
    return out


def _dft16_real(rows):
    out = _fft([(r, None) for r in rows])
    zero = jnp.zeros_like(rows[0])
    out = [(zero if re is None else re, zero if im is None else im) for re, im in out]
    for k in range(DFT_RADIX // 2 + 1, DFT_RADIX):
        re, im = out[DFT_RADIX - k]
        out[k] = (re, -im)
    return out


def _fourier_kernel(n2, xr_ref, xm_ref, twc_ref, tws_ref, dmat_ref, cs_ref, y_ref, x0_ref, z_ref, z0_ref):
    m = n2 - 1
    bf = jnp.bfloat16
    x0_ref[0:N_META, :] = xm_ref[...]
    x0_ref[N_META:N_META + m, :] = xr_ref[0, 0:m, :]

    def run_start(j1):
        return j1 * n2 - N_META

    dc = _dft16_real([x0_ref[0:1, :]] + [xr_ref[0, run_start(j1):run_start(j1) + 1, :] for j1 in range(1, DFT_RADIX)])
    zw = 2 * GROUP_DIM
    for k1 in range(DFT_RADIX):
        z0_ref[0:1, k1 * zw:k1 * zw + GROUP_DIM] = dc[k1][0]
        z0_ref[0:1, k1 * zw + GROUP_DIM:(k1 + 1) * zw] = dc[k1][1]

    def chunk(c, carry):
        r0 = pl.multiple_of(c * 8, 8)
        rows = [x0_ref[pl.ds(r0 + 1, 8), :]]
        rows += [xr_ref[0, pl.ds(r0 + (run_start(j1) + 1), 8), :] for j1 in range(1, DFT_RADIX)]
        a = _dft16_real(rows)
        for k1 in range(DFT_RADIX):
            ar, ai = a[k1]
            if k1 == 0:
                zr, zi = ar, ai
            else:
                tc = twc_ref[k1, pl.ds(r0, 8), :]
                ts = tws_ref[k1, pl.ds(r0, 8), :]
                zr = tc * ar + ts * ai
                zi = tc * ai - ts * ar
            z_ref[pl.ds(r0, 8), k1 * zw:k1 * zw + GROUP_DIM] = zr
            z_ref[pl.ds(r0, 8), k1 * zw + GROUP_DIM:(k1 + 1) * zw] = zi
            z_ref[pl.ds(r0 + m, 8), k1 * zw:k1 * zw + GROUP_DIM] = zi
            z_ref[pl.ds(r0 + m, 8), k1 * zw + GROUP_DIM:(k1 + 1) * zw] = -zr
        return carry

    lax.fori_loop(0, m // 8, chunk, 0)

    scale = 1.0 / math.sqrt(GROUP_DIM * DFT_RADIX * n2)
    for k0 in range(0, DFT_RADIX, DFT_K1_BATCH):
        cols = slice(k0 * zw, (k0 + DFT_K1_BATCH) * zw)
        p = jnp.dot(dmat_ref[...], z_ref[:, cols].astype(bf), preferred_element_type=jnp.float32)
        p = (p + z0_ref[0:1, cols]).astype(bf)
        stacked = jnp.concatenate([p[:, i * zw:(i + 1) * zw] for i in range(DFT_K1_BATCH)], axis=0)
        y = jnp.dot(stacked, cs_ref[...], preferred_element_type=jnp.float32) * scale
        for i in range(DFT_K1_BATCH):
            y_ref[0, 0, pl.ds(k0 + i, m, stride=DFT_RADIX), :] = y[i * m:(i + 1) * m]


def _fourier(uf, uf_meta, twc, tws, dmat, cs):
    b, s, _ = uf.shape
    n2 = (s + N_META) // DFT_RADIX
    m = n2 - 1
    out = pl.pallas_call(
        functools.partial(_fourier_kernel, n2),
        grid=(b, N_GROUPS),
        in_specs=[
            pl.BlockSpec((1, s, GROUP_DIM), lambda bi, g: (bi, 0, g)),
            pl.BlockSpec((N_META, GROUP_DIM), lambda bi, g: (0, g)),
            pl.BlockSpec((DFT_RADIX, m, GROUP_DIM), lambda bi, g: (0, 0, 0)),
            pl.BlockSpec((DFT_RADIX, m, GROUP_DIM), lambda bi, g: (0, 0, 0)),
            pl.BlockSpec((m, 2 * m), lambda bi, g: (0, 0)),
            pl.BlockSpec((2 * GROUP_DIM, GROUP_DIM), lambda bi, g: (0, 0)),
        ],
        out_specs=pl.BlockSpec((1, 1, s, GROUP_DIM), lambda bi, g: (bi, g, 0, 0)),
        out_shape=jax.ShapeDtypeStruct((b, N_GROUPS, s, GROUP_DIM), jnp.float32),
        scratch_shapes=[
            pltpu.VMEM((N_META + m, GROUP_DIM), jnp.float32),
            pltpu.VMEM((2 * m, DFT_RADIX * 2 * GROUP_DIM), jnp.float32),
            pltpu.VMEM((8, DFT_RADIX * 2 * GROUP_DIM), jnp.float32),
        ],
        compiler_params=pltpu.CompilerParams(
            dimension_semantics=("parallel", "parallel"), vmem_limit_bytes=VMEM_LIMIT_BYTES),
        name="fourier",
    )(uf, uf_meta, twc, tws, dmat, cs)
    return out


def _fourier_tables(s):
    n2 = (s + N_META) // DFT_RADIX
    length = DFT_RADIX * n2
    j2 = np.arange(1, n2, dtype=np.int64)
    k1 = np.arange(DFT_RADIX, dtype=np.int64)
    phi = 2.0 * np.pi * ((k1[:, None] * j2[None, :]) % length) / length
    twc = jnp.broadcast_to(jnp.asarray(np.cos(phi), jnp.float32)[:, :, None], (DFT_RADIX, n2 - 1, GROUP_DIM))
    tws = jnp.broadcast_to(jnp.asarray(np.sin(phi), jnp.float32)[:, :, None], (DFT_RADIX, n2 - 1, GROUP_DIM))
    theta = 2.0 * np.pi * ((j2[:, None] * j2[None, :]) % n2) / n2
    dmat = jnp.asarray(np.concatenate([np.cos(theta), np.sin(theta)], axis=1), jnp.float32)
    c = np.arange(GROUP_DIM, dtype=np.int64)
    psi = 2.0 * np.pi * ((c[:, None] * c[None, :]) % GROUP_DIM) / GROUP_DIM
    cs = jnp.asarray(np.concatenate([np.cos(psi), np.sin(psi)], axis=0), jnp.float32)
    return twc, tws, dmat.astype(jnp.bfloat16), cs.astype(jnp.bfloat16)


def _attn_kernel(q_ref, qn_ref, kt_ref, ktn_ref, v_ref, kmt_ref, vm_ref, o_ref, s_ref, pm_ref):
    bf = jnp.bfloat16
    f32 = jnp.float32
    tq = o_ref.shape[1]
    n, _, tk = kt_ref.shape[2:]
    group = s_ref.shape[2] // tk
    n_groups = n // group

    def produce(slot, qt, key_chunk):
        pmax = None
        for j in range(group):
            s = jnp.dot(qt, key_chunk(j), preferred_element_type=f32)
            s_ref[slot, :, j * tk:(j + 1) * tk] = s
            for l in range(tk // LANES):
                blk = s[:, l * LANES:(l + 1) * LANES]
                pmax = blk if pmax is None else jnp.maximum(pmax, blk)
        pm_ref[slot] = pmax

    def with_ones(vals):
        return jnp.concatenate([vals, jnp.ones_like(vals)], axis=1)

    def absorb(slot, g, m_i, acc):
        pmax = pm_ref[slot]
        if g == 0:
            lane = lax.broadcasted_iota(jnp.int32, (tq, LANES), 1)
            s_meta = jnp.dot(q_ref[0, 0], kmt_ref[0, 0], preferred_element_type=f32)
            s_meta = jnp.where(lane < N_META, s_meta, -jnp.inf)
            pmax = jnp.maximum(pmax, s_meta)
        m_new = jnp.max(pmax, axis=1, keepdims=True)
        if g > 0:
            m_new = jnp.maximum(m_i, m_new)
        p = jnp.exp2((s_ref[slot] - m_new).astype(bf))
        pv = jnp.dot(p, with_ones(v_ref[0, 0, g * group * tk:(g + 1) * group * tk, :]), preferred_element_type=f32)
        if g == 0:
            p_meta = jnp.exp2((s_meta - m_new).astype(bf))
            return m_new, pv + jnp.dot(p_meta, with_ones(vm_ref[0, 0]), preferred_element_type=f32)
        return m_new, jnp.exp2(m_i - m_new) * acc + pv

    first_step = (pl.program_id(0) == 0) & (pl.program_id(1) == 0) & (pl.program_id(2) == 0)

    @pl.when(first_step)
    def _():
        produce(0, q_ref[0, 0], lambda j: kt_ref[0, 0, j])

    m = acc = None
    for g in range(n_groups):
        if g + 1 < n_groups:
            produce((g + 1) % 2, q_ref[0, 0], lambda j, g=g: kt_ref[0, 0, (g + 1) * group + j])
        else:
            produce(0, qn_ref[0, 0], lambda j: ktn_ref[0, 0, j])
        m, acc = absorb(g % 2, g, m, acc)
    o_ref[0] = (acc[:, :V_DIM] / acc[:, V_DIM:]).astype(bf)


def _attention(q, kt, v, kmt, vm, tq, tg):
    b, nh, s, _ = q.shape
    n, _, tk = kt.shape[2:]
    nq = s // tq
    assert n * tk == s and tg % tk == 0 and (s // tg) % 2 == 0

    def next_step(bi, h, qi):
        wrap_q = qi + 1 == nq
        wrap_h = wrap_q & (h + 1 == nh)
        bn = jnp.where(wrap_h, jnp.where(bi + 1 == b, 0, bi + 1), bi)
        hn = jnp.where(wrap_q, jnp.where(h + 1 == nh, 0, h + 1), h)
        return bn, hn, jnp.where(wrap_q, 0, qi + 1)

    def next_q(bi, h, qi):
        bn, hn, qn = next_step(bi, h, qi)
        return bn, hn, qn, 0

    def next_keys(bi, h, qi):
        bn, hn, _ = next_step(bi, h, qi)
        return bn, hn, 0, 0, 0

    return pl.pallas_call(
        _attn_kernel,
        grid=(b, nh, nq),
        in_specs=[
            pl.BlockSpec((1, 1, tq, QK_DIM), lambda bi, h, qi: (bi, h, qi, 0)),
            pl.BlockSpec((1, 1, tq, QK_DIM), next_q),
            pl.BlockSpec((1, 1, n, QK_DIM, tk), lambda bi, h, qi: (bi, h, 0, 0, 0)),
            pl.BlockSpec((1, 1, tg // tk, QK_DIM, tk), next_keys),
            pl.BlockSpec((1, 1, s, V_DIM), lambda bi, h, qi: (bi, h, 0, 0)),
            pl.BlockSpec((1, 1, QK_DIM, LANES), lambda bi, h, qi: (0, h, 0, 0)),
            pl.BlockSpec((1, 1, LANES, V_DIM), lambda bi, h, qi: (0, h, 0, 0)),
        ],
        out_specs=pl.BlockSpec((1, tq, V_DIM), lambda bi, h, qi: (bi, qi, h)),
        out_shape=jax.ShapeDtypeStruct((b, s, ATTN_DIM), jnp.bfloat16),
        scratch_shapes=[pltpu.VMEM((2, tq, tg), jnp.float32), pltpu.VMEM((2, tq, LANES), jnp.float32)],
        compiler_params=pltpu.CompilerParams(
            dimension_semantics=("arbitrary", "arbitrary", "arbitrary"), vmem_limit_bytes=VMEM_LIMIT_BYTES),
        name="attention",
    )(q, q, kt, kt, v, kmt, vm)


def _tail_kernel(x_ref, y_ref, o_ref, sa_ref, sb_ref, wfo_ref, wao_ref, wo_ref, g2_ref, wg_ref, wu_ref, wd_ref,
                 gf_ref, out_ref):
    bf = jnp.bfloat16
    f32 = jnp.float32
    yf = jnp.concatenate([y_ref[0, g] for g in range(N_GROUPS)], axis=1).astype(bf)
    ya = jnp.dot(yf, wfo_ref[...], preferred_element_type=f32)
    yb = jnp.dot(o_ref[0], wao_ref[...], preferred_element_type=f32)
    merged = sa_ref[0].astype(f32) * ya + sb_ref[0].astype(f32) * yb
    x1 = x_ref[0] + jnp.dot(merged.astype(bf), wo_ref[...], preferred_element_type=f32)
    h2 = _rms(x1, g2_ref[...]).astype(bf)
    x2 = x1
    for c0 in range(0, D_FF, FF_CHUNK):
        c1 = min(c0 + FF_CHUNK, D_FF)
        gate = jnp.dot(h2, wg_ref[:, c0:c1], preferred_element_type=f32)
        up = jnp.dot(h2, wu_ref[:, c0:c1], preferred_element_type=f32)
        act = (gate * jax.nn.sigmoid(gate) * up).astype(bf)
        x2 = x2 + jnp.dot(act, wd_ref[c0:c1, :], preferred_element_type=f32)
    out_ref[0] = _rms(x2, gf_ref[...])


def _tail(x, y, o, sa, sb, wfo, wao, wo, g2, wg, wu, wd, gf, tm):
    b, s, _ = x.shape
    tok = lambda bi, si: (bi, si, 0)
    const = lambda bi, si: (0, 0)

    def weight(shape):
        return pl.BlockSpec(shape, const, pipeline_mode=pl.Buffered(1))

    return pl.pallas_call(
        _tail_kernel,
        grid=(b, s // tm),
        in_specs=[
            pl.BlockSpec((1, tm, D_MODEL), tok),
            pl.BlockSpec((1, N_GROUPS, tm, GROUP_DIM), lambda bi, si: (bi, 0, si, 0)),
            pl.BlockSpec((1, tm, ATTN_DIM), tok),
            pl.BlockSpec((1, tm, D_MODEL), tok),
            pl.BlockSpec((1, tm, D_MODEL), tok),
            weight((FOURIER_DIM, D_MODEL)),
            weight((ATTN_DIM, D_MODEL)),
            weight((D_MODEL, D_MODEL)),
            pl.BlockSpec((1, D_MODEL), const),
            weight((D_MODEL, D_FF)),
            weight((D_MODEL, D_FF)),
            weight((D_FF, D_MODEL)),
            pl.BlockSpec((1, D_MODEL), const),
        ],
        out_specs=pl.BlockSpec((1, tm, D_MODEL), tok),
        out_shape=jax.ShapeDtypeStruct((b, s, D_MODEL), jnp.float32),
        compiler_params=pltpu.CompilerParams(
            dimension_semantics=("parallel", "parallel"), vmem_limit_bytes=VMEM_LIMIT_BYTES),
        name="tail",
    )(x, y, o, sa, sb, wfo, wao, wo, g2, wg, wu, wd, gf)


def _rope_tables(length):
    inv = 1.0 / (ROPE_THETA ** (jnp.arange(0, ROPE_DIM, 2, dtype=jnp.float32) / ROPE_DIM))
    ang = jnp.arange(length, dtype=jnp.float32)[:, None] * inv[None, :]
    c, s = jnp.cos(ang), jnp.sin(ang)
    return jnp.concatenate([c, c, c, c], axis=1), jnp.concatenate([-s, s, -s, s], axis=1)


def _prepare_weights(norm1_g, w_in, q_norm_g, kv_norm_g, w_uq, w_ukv, w_fourier_out, w_attn_out, w_o, norm2_g,
                     w_ffn_gate, w_ffn_up, w_ffn_down, final_norm_g):
    bf = jnp.bfloat16
    w = w_in[0]
    s_uf, s_cq, s_ckv, s_kr = FOURIER_DIM, FOURIER_DIM + Q_RANK, FOURIER_DIM + Q_RANK + KV_RANK, \
        FOURIER_DIM + Q_RANK + KV_RANK + ROPE_DIM
    win = jnp.concatenate(
        [w[:, :s_ckv], w[:, s_kr:], w[:, s_ckv:s_kr], jnp.zeros((D_MODEL, LANES - ROPE_DIM), w.dtype)], axis=1).astype(bf)
    wq = w_uq[0].reshape(Q_RANK, N_HEADS, QK_DIM)
    wuq = jnp.concatenate([wq[:, :, :NOPE_DIM].reshape(Q_RANK, -1), wq[:, :, NOPE_DIM:].reshape(Q_RANK, -1)], axis=1).astype(bf)
    wkv = w_ukv[0].reshape(KV_RANK, N_HEADS, NOPE_DIM + V_DIM)
    wkt = wkv[:, :, :NOPE_DIM].reshape(KV_RANK, -1).T.astype(bf)
    wv = wkv[:, :, NOPE_DIM:].reshape(KV_RANK, -1).astype(bf)
    row = lambda g: g.reshape(1, -1).astype(jnp.float32)
    return dict(
        g1=row(norm1_g[0]), win=win, qg=row(q_norm_g[0]), kvg=row(kv_norm_g[0]), wuq=wuq, wkt=wkt, wv=wv,
        wfo=w_fourier_out[0].astype(bf), wao=w_attn_out[0].astype(bf), wo=w_o[0].astype(bf), g2=row(norm2_g[0]),
        wg=w_ffn_gate[0].astype(bf), wu=w_ffn_up[0].astype(bf), wd=w_ffn_down[0].astype(bf), gf=row(final_norm_g))


def _project(p, x, cos2, sin2, tm):
    return _inproj(x, p["g1"], p["win"], p["qg"], p["kvg"], p["wuq"], p["wkt"], p["wv"], cos2, sin2, tm)


def _meta_projection(meta, p, cos2, sin2):
    meta_pad = jnp.zeros((1, LANES, D_MODEL), meta.dtype).at[0, :N_META].set(meta)
    uf_m, _, kt_m, v_m, _, _ = _project(p, meta_pad, cos2[:LANES], sin2[:LANES], LANES)
    return uf_m[0, :N_META], kt_m[:, :, 0], v_m


def _trunk(x, meta_proj, rope, p, tk, tq, tg, tm_tail):
    s = x.shape[1]
    uf_m, kt_m, v_m = meta_proj
    cos2, sin2 = rope
    uf, q, kt, v, sa, sb = _project(p, x, cos2[N_META:N_META + s], sin2[N_META:N_META + s], tk)
    y = _fourier(uf, uf_m, *_fourier_tables(s))
    o = _attention(q, kt, v, kt_m, v_m, tq, tg)
    return _tail(x, y, o, sa, sb, p["wfo"], p["wao"], p["wo"], p["g2"], p["wg"], p["wu"], p["wd"], p["gf"], tm_tail)


def kernel(x_prompt, x_sample, meta_tokens, norm1_g, w_in, q_norm_g, kv_norm_g, w_uq, w_ukv, w_fourier_out,
           w_attn_out, w_o, norm2_g, w_ffn_gate, w_ffn_up, w_ffn_down, final_norm_g):
    p = _prepare_weights(norm1_g, w_in, q_norm_g, kv_norm_g, w_uq, w_ukv, w_fourier_out, w_attn_out, w_o, norm2_g,
                         w_ffn_gate, w_ffn_up, w_ffn_down, final_norm_g)
    rope = _rope_tables(max(x_prompt.shape[1], x_sample.shape[1]) + N_META)
    meta_proj = _meta_projection(meta_tokens, p, *rope)
    cfg = dict(tk=512, tq=1024, tg=1024, tm_tail=512)
    y_prompt = _trunk(x_prompt, meta_proj, rope, p, **cfg)
    y_sample = _trunk(x_sample, meta_proj, rope, p, **cfg)
    return (y_prompt, y_sample)
```

```python
import functools
import math

import jax
import jax.numpy as jnp
import numpy as np
from jax import lax
from jax.experimental import pallas as pl
from jax.experimental.pallas import tpu as pltpu

D_MODEL = 1024
N_META = 16
N_GROUPS = 4
GROUP_DIM = 128
FOURIER_DIM = N_GROUPS * GROUP_DIM
N_HEADS = 8
NOPE_DIM = 128
ROPE_DIM = 64
QK_DIM = NOPE_DIM + ROPE_DIM
V_DIM = 128
Q_RANK = 512
KV_RANK = 256
ATTN_DIM = N_HEADS * V_DIM
D_FF = 2816
ROPE_THETA = 10000.0
NORM_EPS = 1e-6
ATTN_SCALE = QK_DIM ** -0.5
LOG2_E = math.log2(math.e)
DFT_RADIX = 16

LANES = 128
VMEM_LIMIT_BYTES = 56 * 1024 * 1024
DFT_K1_BATCH = 4
FF_CHUNK = 1024

_C_UF = 0
_C_CQ = _C_UF + FOURIER_DIM
_C_CKV = _C_CQ + Q_RANK
_C_GA = _C_CKV + KV_RANK
_C_GB = _C_GA + D_MODEL
_C_KR = _C_GB + D_MODEL
IN_COLS = _C_KR + LANES


def _rms(x, g):
    return x * lax.rsqrt(jnp.mean(x * x, axis=-1, keepdims=True) + NORM_EPS) * g


def _rope128(x, cos2, sin2):
    lane = lax.broadcasted_iota(jnp.int32, x.shape, 1)
    first_half = (lane % ROPE_DIM) < (ROPE_DIM // 2)
    partner = jnp.where(first_half, pltpu.roll(x, LANES - ROPE_DIM // 2, 1), pltpu.roll(x, ROPE_DIM // 2, 1))
    return x * cos2 + partner * sin2


def _inproj_kernel(x_ref, g1_ref, win_ref, qg_ref, kvg_ref, wuq_ref, wkt_ref, wv_ref, cos_ref, sin_ref,
                   uf_ref, q_ref, kt_ref, v_ref, sa_ref, sb_ref):
    bf = jnp.bfloat16
    f32 = jnp.float32
    h = _rms(x_ref[0], g1_ref[...]).astype(bf)

    def proj(c0, width):
        return jnp.dot(h, win_ref[:, c0:c0 + width], preferred_element_type=f32)

    cos2 = cos_ref[...]
    sin2 = sin_ref[...]

    ckv_raw = proj(_C_CKV, KV_RANK)
    kr_raw = proj(_C_KR, LANES)
    cq_raw = proj(_C_CQ, Q_RANK)

    ckv = _rms(ckv_raw, kvg_ref[...])
    ckv_t = ckv.T.astype(bf)
    k_t = jnp.dot(wkt_ref[...], ckv_t, preferred_element_type=f32)
    vals = jnp.dot(ckv.astype(bf), wv_ref[...], preferred_element_type=f32)
    cq = _rms(cq_raw, qg_ref[...]).astype(bf)
    q = jnp.dot(cq, wuq_ref[...], preferred_element_type=f32) * (ATTN_SCALE * LOG2_E)

    kr_t = _rope128(kr_raw, cos2, sin2).T[:ROPE_DIM].astype(bf)
    for hd in range(N_HEADS):
        kt_ref[0, hd, 0, :NOPE_DIM, :] = k_t[hd * NOPE_DIM:(hd + 1) * NOPE_DIM].astype(bf)
        kt_ref[0, hd, 0, NOPE_DIM:, :] = kr_t
        v_ref[0, hd] = vals[:, hd * V_DIM:(hd + 1) * V_DIM].astype(bf)

    rope0 = N_HEADS * NOPE_DIM
    for pair in range(N_HEADS // 2):
        qr = _rope128(q[:, rope0 + pair * LANES:rope0 + (pair + 1) * LANES], cos2, sin2).astype(bf)
        for sub in range(2):
            hd = 2 * pair + sub
            q_ref[0, hd, :, :NOPE_DIM] = q[:, hd * NOPE_DIM:(hd + 1) * NOPE_DIM].astype(bf)
            q_ref[0, hd, :, NOPE_DIM:] = qr[:, sub * ROPE_DIM:(sub + 1) * ROPE_DIM]

    uf_ref[0] = proj(_C_UF, FOURIER_DIM)
    sa_ref[0] = jax.nn.sigmoid(proj(_C_GA, D_MODEL)).astype(bf)
    sb_ref[0] = jax.nn.sigmoid(proj(_C_GB, D_MODEL)).astype(bf)


def _const_spec(shape):
    return pl.BlockSpec(shape, lambda *_: (0,) * len(shape), pipeline_mode=pl.Buffered(1))


def _inproj(x, g1, win, qg, kvg, wuq, wkt, wv, cos2, sin2, tm):
    b, s, _ = x.shape
    tok = lambda bi, si: (bi, si, 0)
    head = lambda bi, si: (bi, 0, si, 0)
    bf = jnp.bfloat16
    return pl.pallas_call(
        _inproj_kernel,
        grid=(b, s // tm),
        in_specs=[
            pl.BlockSpec((1, tm, D_MODEL), tok),
            _const_spec((1, D_MODEL)),
            _const_spec((D_MODEL, IN_COLS)),
            _const_spec((1, Q_RANK)),
            _const_spec((1, KV_RANK)),
            _const_spec((Q_RANK, N_HEADS * QK_DIM)),
            _const_spec((N_HEADS * NOPE_DIM, KV_RANK)),
            _const_spec((KV_RANK, N_HEADS * V_DIM)),
            pl.BlockSpec((tm, LANES), lambda bi, si: (si, 0)),
            pl.BlockSpec((tm, LANES), lambda bi, si: (si, 0)),
        ],
        out_specs=[
            pl.BlockSpec((1, tm, FOURIER_DIM), tok),
            pl.BlockSpec((1, N_HEADS, tm, QK_DIM), head),
            pl.BlockSpec((1, N_HEADS, 1, QK_DIM, tm), lambda bi, si: (bi, 0, si, 0, 0)),
            pl.BlockSpec((1, N_HEADS, tm, V_DIM), head),
            pl.BlockSpec((1, tm, D_MODEL), tok),
            pl.BlockSpec((1, tm, D_MODEL), tok),
        ],
        out_shape=[
            jax.ShapeDtypeStruct((b, s, FOURIER_DIM), jnp.float32),
            jax.ShapeDtypeStruct((b, N_HEADS, s, QK_DIM), bf),
            jax.ShapeDtypeStruct((b, N_HEADS, s // tm, QK_DIM, tm), bf),
            jax.ShapeDtypeStruct((b, N_HEADS, s, V_DIM), bf),
            jax.ShapeDtypeStruct((b, s, D_MODEL), bf),
            jax.ShapeDtypeStruct((b, s, D_MODEL), bf),
        ],
        compiler_params=pltpu.CompilerParams(
            dimension_semantics=("parallel", "parallel"), vmem_limit_bytes=VMEM_LIMIT_BYTES),
        name="inproj",
    )(x, g1, win, qg, kvg, wuq, wkt, wv, cos2, sin2)


def _cneg(v):
    return None if v is None else -v


def _cadd(a, b):
    if a is None:
        return b
    if b is None:
        return a
    return a + b


def _csub(a, b):
    if b is None:
        return a
    if a is None:
        return -b
    return a - b


def _fft(xs):
    n = len(xs)
    if n == 1:
        return xs
    ev = _fft(xs[0::2])
    od = _fft(xs[1::2])
    out = [None] * n
    for k in range(n // 2):
        re, im = od[k]
        if k == 0:
            tr, ti = re, im
        elif 4 * k == n:
            tr, ti = im, _cneg(re)
        else:
            c = math.cos(2 * math.pi * k / n)
            s = math.sin(2 * math.pi * k / n)
            tr = _cadd(None if re is None else re * c, None if im is None else im * s)
            ti = _csub(None if im is None else im * c, None if re is None else re * s)
        out[k] = (_cadd(ev[k][0], tr), _cadd(ev[k][1], ti))
        out[k + n // 2] = (_csub(ev[k][0], tr), _csub(ev[k][1], ti))
    return out


def _dft16_real(rows):
    out = _fft([(r, None) for r in rows])
    zero = jnp.zeros_like(rows[0])
    out = [(zero if re is None else re, zero if im is None else im) for re, im in out]
    for k in range(DFT_RADIX // 2 + 1, DFT_RADIX):
        re, im = out[DFT_RADIX - k]
        out[k] = (re, -im)
    return out


def _fourier_kernel(n2, xr_ref, xm_ref, twc_ref, tws_ref, dmat_ref, cs_ref, y_ref, x0_ref, z_ref, z0_ref):
    m = n2 - 1
    bf = jnp.bfloat16
    x0_ref[0:N_META, :] = xm_ref[...]
    x0_ref[N_META:N_META + m, :] = xr_ref[0, 0:m, :]

    def run_start(j1):
        return j1 * n2 - N_META

    dc = _dft16_real([x0_ref[0:1, :]] + [xr_ref[0, run_start(j1):run_start(j1) + 1, :] for j1 in range(1, DFT_RADIX)])
    zw = 2 * GROUP_DIM
    for k1 in range(DFT_RADIX):
        z0_ref[0:1, k1 * zw:k1 * zw + GROUP_DIM] = dc[k1][0]
        z0_ref[0:1, k1 * zw + GROUP_DIM:(k1 + 1) * zw] = dc[k1][1]

    def chunk(c, carry):
        r0 = pl.multiple_of(c * 8, 8)
        rows = [x0_ref[pl.ds(r0 + 1, 8), :]]
        rows += [xr_ref[0, pl.ds(r0 + (run_start(j1) + 1), 8), :] for j1 in range(1, DFT_RADIX)]
        a = _dft16_real(rows)
        for k1 in range(DFT_RADIX):
            ar, ai = a[k1]
            if k1 == 0:
                zr, zi = ar, ai
            else:
                tc = twc_ref[k1, pl.ds(r0, 8), :]
                ts = tws_ref[k1, pl.ds(r0, 8), :]
                zr = tc * ar + ts * ai
                zi = tc * ai - ts * ar
            z_ref[pl.ds(r0, 8), k1 * zw:k1 * zw + GROUP_DIM] = zr
            z_ref[pl.ds(r0, 8), k1 * zw + GROUP_DIM:(k1 + 1) * zw] = zi
            z_ref[pl.ds(r0 + m, 8), k1 * zw:k1 * zw + GROUP_DIM] = zi
            z_ref[pl.ds(r0 + m, 8), k1 * zw + GROUP_DIM:(k1 + 1) * zw] = -zr
        return carry

    lax.fori_loop(0, m // 8, chunk, 0)

    scale = 1.0 / math.sqrt(GROUP_DIM * DFT_RADIX * n2)
    for k0 in range(0, DFT_RADIX, DFT_K1_BATCH):
        cols = slice(k0 * zw, (k0 + DFT_K1_BATCH) * zw)
        p = jnp.dot(dmat_ref[...], z_ref[:, cols].astype(bf), preferred_element_type=jnp.float32)
        p = (p + z0_ref[0:1, cols]).astype(bf)
        stacked = jnp.concatenate([p[:, i * zw:(i + 1) * zw] for i in range(DFT_K1_BATCH)], axis=0)
        y = jnp.dot(stacked, cs_ref[...], preferred_element_type=jnp.float32) * scale
        for i in range(DFT_K1_BATCH):
            y_ref[0, 0, pl.ds(k0 + i, m, stride=DFT_RADIX), :] = y[i * m:(i + 1) * m]


def _fourier(uf, uf_meta, twc, tws, dmat, cs):
    b, s, _ = uf.shape
    n2 = (s + N_META) // DFT_RADIX
    m = n2 - 1
    out = pl.pallas_call(
        functools.partial(_fourier_kernel, n2),
        grid=(b, N_GROUPS),
        in_specs=[
            pl.BlockSpec((1, s, GROUP_DIM), lambda bi, g: (bi, 0, g)),
            pl.BlockSpec((N_META, GROUP_DIM), lambda bi, g: (0, g)),
            pl.BlockSpec((DFT_RADIX, m, GROUP_DIM), lambda bi, g: (0, 0, 0)),
            pl.BlockSpec((DFT_RADIX, m, GROUP_DIM), lambda bi, g: (0, 0, 0)),
            pl.BlockSpec((m, 2 * m), lambda bi, g: (0, 0)),
            pl.BlockSpec((2 * GROUP_DIM, GROUP_DIM), lambda bi, g: (0, 0)),
        ],
        out_specs=pl.BlockSpec((1, 1, s, GROUP_DIM), lambda bi, g: (bi, g, 0, 0)),
        out_shape=jax.ShapeDtypeStruct((b, N_GROUPS, s, GROUP_DIM), jnp.float32),
        scratch_shapes=[
            pltpu.VMEM((N_META + m, GROUP_DIM), jnp.float32),
            pltpu.VMEM((2 * m, DFT_RADIX * 2 * GROUP_DIM), jnp.float32),
            pltpu.VMEM((8, DFT_RADIX * 2 * GROUP_DIM), jnp.float32),
        ],
        compiler_params=pltpu.CompilerParams(
            dimension_semantics=("parallel", "parallel"), vmem_limit_bytes=VMEM_LIMIT_BYTES),
        name="fourier",
    )(uf, uf_meta, twc, tws, dmat, cs)
    return out


def _fourier_tables(s):
    n2 = (s + N_META) // DFT_RADIX
    length = DFT_RADIX * n2
    j2 = np.arange(1, n2, dtype=np.int64)
    k1 = np.arange(DFT_RADIX, dtype=np.int64)
    phi = 2.0 * np.pi * ((k1[:, None] * j2[None, :]) % length) / length
    twc = jnp.broadcast_to(jnp.asarray(np.cos(phi), jnp.float32)[:, :, None], (DFT_RADIX, n2 - 1, GROUP_DIM))
    tws = jnp.broadcast_to(jnp.asarray(np.sin(phi), jnp.float32)[:, :, None], (DFT_RADIX, n2 - 1, GROUP_DIM))
    theta = 2.0 * np.pi * ((j2[:, None] * j2[None, :]) % n2) / n2
    dmat = jnp.asarray(np.concatenate([np.cos(theta), np.sin(theta)], axis=1), jnp.float32)
    c = np.arange(GROUP_DIM, dtype=np.int64)
    psi = 2.0 * np.pi * ((c[:, None] * c[None, :]) % GROUP_DIM) / GROUP_DIM
    cs = jnp.asarray(np.concatenate([np.cos(psi), np.sin(psi)], axis=0), jnp.float32)
    return twc, tws, dmat.astype(jnp.bfloat16), cs.astype(jnp.bfloat16)


def _attn_kernel(q_ref, qn_ref, kt_ref, ktn_ref, v_ref, kmt_ref, vm_ref, o_ref, s_ref, pm_ref):
    bf = jnp.bfloat16
    f32 = jnp.float32
    tq = o_ref.shape[1]
    n, _, tk = kt_ref.shape[2:]
    group = s_ref.shape[2] // tk
    n_groups = n // group

    def produce(slot, qt, key_chunk):
        pmax = None
        for j in range(group):
            s = jnp.dot(qt, key_chunk(j), preferred_element_type=f32)
            s_ref[slot, :, j * tk:(j + 1) * tk] = s
            for l in range(tk // LANES):
                blk = s[:, l * LANES:(l + 1) * LANES]
                pmax = blk if pmax is None else jnp.maximum(pmax, blk)
        pm_ref[slot] = pmax

    def with_ones(vals):
        return jnp.concatenate([vals, jnp.ones_like(vals)], axis=1)

    def absorb(slot, g, m_i, acc):
        pmax = pm_ref[slot]
        if g == 0:
            lane = lax.broadcasted_iota(jnp.int32, (tq, LANES), 1)
            s_meta = jnp.dot(q_ref[0, 0], kmt_ref[0, 0], preferred_element_type=f32)
            s_meta = jnp.where(lane < N_META, s_meta, -jnp.inf)
            pmax = jnp.maximum(pmax, s_meta)
        m_new = jnp.max(pmax, axis=1, keepdims=True)
        if g > 0:
            m_new = jnp.maximum(m_i, m_new)
        p = jnp.exp2((s_ref[slot] - m_new).astype(bf))
        pv = jnp.dot(p, with_ones(v_ref[0, 0, g * group * tk:(g + 1) * group * tk, :]), preferred_element_type=f32)
        if g == 0:
            p_meta = jnp.exp2((s_meta - m_new).astype(bf))
            return m_new, pv + jnp.dot(p_meta, with_ones(vm_ref[0, 0]), preferred_element_type=f32)
        return m_new, jnp.exp2(m_i - m_new) * acc + pv

    first_step = (pl.program_id(0) == 0) & (pl.program_id(1) == 0) & (pl.program_id(2) == 0)

    @pl.when(first_step)
    def _():
        produce(0, q_ref[0, 0], lambda j: kt_ref[0, 0, j])

    m = acc = None
    for g in range(n_groups):
        if g + 1 < n_groups:
            produce((g + 1) % 2, q_ref[0, 0], lambda j, g=g: kt_ref[0, 0, (g + 1) * group + j])
        else:
            produce(0, qn_ref[0, 0], lambda j: ktn_ref[0, 0, j])
        m, acc = absorb(g % 2, g, m, acc)
    o_ref[0] = (acc[:, :V_DIM] / acc[:, V_DIM:]).astype(bf)


def _attention(q, kt, v, kmt, vm, tq, tg):
    b, nh, s, _ = q.shape
    n, _, tk = kt.shape[2:]
    nq = s // tq
    assert n * tk == s and tg % tk == 0 and (s // tg) % 2 == 0

    def next_step(bi, h, qi):
        wrap_q = qi + 1 == nq
        wrap_h = wrap_q & (h + 1 == nh)
        bn = jnp.where(wrap_h, jnp.where(bi + 1 == b, 0, bi + 1), bi)
        hn = jnp.where(wrap_q, jnp.where(h + 1 == nh, 0, h + 1), h)
        return bn, hn, jnp.where(wrap_q, 0, qi + 1)

    def next_q(bi, h, qi):
        bn, hn, qn = next_step(bi, h, qi)
        return bn, hn, qn, 0

    def next_keys(bi, h, qi):
        bn, hn, _ = next_step(bi, h, qi)
        return bn, hn, 0, 0, 0

    return pl.pallas_call(
        _attn_kernel,
        grid=(b, nh, nq),
        in_specs=[
            pl.BlockSpec((1, 1, tq, QK_DIM), lambda bi, h, qi: (bi, h, qi, 0)),
            pl.BlockSpec((1, 1, tq, QK_DIM), next_q),
            pl.BlockSpec((1, 1, n, QK_DIM, tk), lambda bi, h, qi: (bi, h, 0, 0, 0)),
            pl.BlockSpec((1, 1, tg // tk, QK_DIM, tk), next_keys),
            pl.BlockSpec((1, 1, s, V_DIM), lambda bi, h, qi: (bi, h, 0, 0)),
            pl.BlockSpec((1, 1, QK_DIM, LANES), lambda bi, h, qi: (0, h, 0, 0)),
            pl.BlockSpec((1, 1, LANES, V_DIM), lambda bi, h, qi: (0, h, 0, 0)),
        ],
        out_specs=pl.BlockSpec((1, tq, V_DIM), lambda bi, h, qi: (bi, qi, h)),
        out_shape=jax.ShapeDtypeStruct((b, s, ATTN_DIM), jnp.bfloat16),
        scratch_shapes=[pltpu.VMEM((2, tq, tg), jnp.float32), pltpu.VMEM((2, tq, LANES), jnp.float32)],
        compiler_params=pltpu.CompilerParams(
            dimension_semantics=("arbitrary", "arbitrary", "arbitrary"), vmem_limit_bytes=VMEM_LIMIT_BYTES),
        name="attention",
    )(q, q, kt, kt, v, kmt, vm)


def _tail_kernel(x_ref, y_ref, o_ref, sa_ref, sb_ref, wfo_ref, wao_ref, wo_ref, g2_ref, wg_ref, wu_ref, wd_ref,
                 gf_ref, out_ref):
    bf = jnp.bfloat16
    f32 = jnp.float32
    yf = jnp.concatenate([y_ref[0, g] for g in range(N_GROUPS)], axis=1).astype(bf)
    ya = jnp.dot(yf, wfo_ref[...], preferred_element_type=f32)
    yb = jnp.dot(o_ref[0], wao_ref[...], preferred_element_type=f32)
    merged = sa_ref[0].astype(f32) * ya + sb_ref[0].astype(f32) * yb
    x1 = x_ref[0] + jnp.dot(merged.astype(bf), wo_ref[...], preferred_element_type=f32)
    h2 = _rms(x1, g2_ref[...]).astype(bf)
    x2 = x1
    for c0 in range(0, D_FF, FF_CHUNK):
        c1 = min(c0 + FF_CHUNK, D_FF)
        gate = jnp.dot(h2, wg_ref[:, c0:c1], preferred_element_type=f32)
        up = jnp.dot(h2, wu_ref[:, c0:c1], preferred_element_type=f32)
        act = (gate * jax.nn.sigmoid(gate) * up).astype(bf)
        x2 = x2 + jnp.dot(act, wd_ref[c0:c1, :], preferred_element_type=f32)
    out_ref[0] = _rms(x2, gf_ref[...])


def _tail(x, y, o, sa, sb, wfo, wao, wo, g2, wg, wu, wd, gf, tm):
    b, s, _ = x.shape
    tok = lambda bi, si: (bi, si, 0)
    const = lambda bi, si: (0, 0)

    def weight(shape):
        return pl.BlockSpec(shape, const, pipeline_mode=pl.Buffered(1))

    return pl.pallas_call(
        _tail_kernel,
        grid=(b, s // tm),
        in_specs=[
            pl.BlockSpec((1, tm, D_MODEL), tok),
            pl.BlockSpec((1, N_GROUPS, tm, GROUP_DIM), lambda bi, si: (bi, 0, si, 0)),
            pl.BlockSpec((1, tm, ATTN_DIM), tok),
            pl.BlockSpec((1, tm, D_MODEL), tok),
            pl.BlockSpec((1, tm, D_MODEL), tok),
            weight((FOURIER_DIM, D_MODEL)),
            weight((ATTN_DIM, D_MODEL)),
            weight((D_MODEL, D_MODEL)),
            pl.BlockSpec((1, D_MODEL), const),
            weight((D_MODEL, D_FF)),
            weight((D_MODEL, D_FF)),
            weight((D_FF, D_MODEL)),
            pl.BlockSpec((1, D_MODEL), const),
        ],
        out_specs=pl.BlockSpec((1, tm, D_MODEL), tok),
        out_shape=jax.ShapeDtypeStruct((b, s, D_MODEL), jnp.float32),
        compiler_params=pltpu.CompilerParams(
            dimension_semantics=("parallel", "parallel"), vmem_limit_bytes=VMEM_LIMIT_BYTES),
        name="tail",
    )(x, y, o, sa, sb, wfo, wao, wo, g2, wg, wu, wd, gf)


def _rope_tables(start, length):
    half = ROPE_DIM // 2
    lane = np.arange(LANES)
    inv = 1.0 / (ROPE_THETA ** (jnp.arange(0, ROPE_DIM, 2, dtype=jnp.float32) / ROPE_DIM))
    inv = inv[lane % half]
    sign = jnp.asarray(np.where((lane // half) % 2 == 0, -1.0, 1.0), jnp.float32)
    ang = jnp.arange(start, start + length, dtype=jnp.float32)[:, None] * inv[None, :]
    return jnp.cos(ang), jnp.sin(ang) * sign


def _prepare_weights(norm1_g, w_in, q_norm_g, kv_norm_g, w_uq, w_ukv, w_fourier_out, w_attn_out, w_o, norm2_g,
                     w_ffn_gate, w_ffn_up, w_ffn_down, final_norm_g):
    bf = jnp.bfloat16
    w = w_in[0]
    s_uf, s_cq, s_ckv, s_kr = FOURIER_DIM, FOURIER_DIM + Q_RANK, FOURIER_DIM + Q_RANK + KV_RANK, \
        FOURIER_DIM + Q_RANK + KV_RANK + ROPE_DIM
    win = jnp.concatenate(
        [w[:, :s_ckv], w[:, s_kr:], w[:, s_ckv:s_kr], jnp.zeros((D_MODEL, LANES - ROPE_DIM), w.dtype)], axis=1).astype(bf)
    wq = w_uq[0].reshape(Q_RANK, N_HEADS, QK_DIM)
    wuq = jnp.concatenate([wq[:, :, :NOPE_DIM].reshape(Q_RANK, -1), wq[:, :, NOPE_DIM:].reshape(Q_RANK, -1)], axis=1).astype(bf)
    wkv = w_ukv[0].reshape(KV_RANK, N_HEADS, NOPE_DIM + V_DIM)
    wkt = wkv[:, :, :NOPE_DIM].reshape(KV_RANK, -1).T.astype(bf)
    wv = wkv[:, :, NOPE_DIM:].reshape(KV_RANK, -1).astype(bf)
    row = lambda g: g.reshape(1, -1).astype(jnp.float32)
    return dict(
        g1=row(norm1_g[0]), win=win, qg=row(q_norm_g[0]), kvg=row(kv_norm_g[0]), wuq=wuq, wkt=wkt, wv=wv,
        wfo=w_fourier_out[0].astype(bf), wao=w_attn_out[0].astype(bf), wo=w_o[0].astype(bf), g2=row(norm2_g[0]),
        wg=w_ffn_gate[0].astype(bf), wu=w_ffn_up[0].astype(bf), wd=w_ffn_down[0].astype(bf), gf=row(final_norm_g))


def _project(p, x, cos2, sin2, tm):
    return _inproj(x, p["g1"], p["win"], p["qg"], p["kvg"], p["wuq"], p["wkt"], p["wv"], cos2, sin2, tm)


def _meta_projection(meta, p):
    meta_pad = jnp.zeros((1, LANES, D_MODEL), meta.dtype).at[0, :N_META].set(meta)
    uf_m, _, kt_m, v_m, _, _ = _project(p, meta_pad, *_rope_tables(0, LANES), LANES)
    return uf_m[0, :N_META], kt_m[:, :, 0], v_m


def _trunk(x, meta_proj, rope, p, tk, tq, tg, tm_tail):
    s = x.shape[1]
    uf_m, kt_m, v_m = meta_proj
    cos2, sin2 = rope
    uf, q, kt, v, sa, sb = _project(p, x, cos2, sin2, tk)
    y = _fourier(uf, uf_m, *_fourier_tables(s))
    o = _attention(q, kt, v, kt_m, v_m, tq, tg)
    return _tail(x, y, o, sa, sb, p["wfo"], p["wao"], p["wo"], p["g2"], p["wg"], p["wu"], p["wd"], p["gf"], tm_tail)


def kernel(x_prompt, x_sample, meta_tokens, norm1_g, w_in, q_norm_g, kv_norm_g, w_uq, w_ukv, w_fourier_out,
           w_attn_out, w_o, norm2_g, w_ffn_gate, w_ffn_up, w_ffn_down, final_norm_g):
    p = _prepare_weights(norm1_g, w_in, q_norm_g, kv_norm_g, w_uq, w_ukv, w_fourier_out, w_attn_out, w_o, norm2_g,
                         w_ffn_gate, w_ffn_up, w_ffn_down, final_norm_g)
    rope = _rope_tables(N_META, max(x_prompt.shape[1], x_sample.shape[1]))
    meta_proj = _meta_projection(meta_tokens, p)
    cfg = dict(tk=512, tq=1024, tg=1024, tm_tail=512)
    y_prompt = _trunk(x_prompt, meta_proj, rope, p, **cfg)
    y_sample = _trunk(x_sample, meta_proj, rope, p, **cfg)
    return (y_prompt, y_sample)
```

```python
import functools
import math

import jax
import jax.numpy as jnp
import numpy as np
from jax import lax
from jax.experimental import pallas as pl
from jax.experimental.pallas import tpu as pltpu

D_MODEL = 1024
N_META = 16
N_GROUPS = 4
GROUP_DIM = 128
FOURIER_DIM = N_GROUPS * GROUP_DIM
N_HEADS = 8
NOPE_DIM = 128
ROPE_DIM = 64
QK_DIM = NOPE_DIM + ROPE_DIM
V_DIM = 128
Q_RANK = 512
KV_RANK = 256
ATTN_DIM = N_HEADS * V_DIM
D_FF = 2816
ROPE_THETA = 10000.0
NORM_EPS = 1e-6
ATTN_SCALE = QK_DIM ** -0.5
LOG2_E = math.log2(math.e)
DFT_RADIX = 16

LANES = 128
VMEM_LIMIT_BYTES = 56 * 1024 * 1024
DFT_K1_BATCH = 4
FF_CHUNK = 1024

_C_UF = 0
_C_CQ = _C_UF + FOURIER_DIM
_C_CKV = _C_CQ + Q_RANK
_C_GA = _C_CKV + KV_RANK
_C_GB = _C_GA + D_MODEL
_C_KR = _C_GB + D_MODEL
IN_COLS = _C_KR + LANES


def _rms(x, g):
    return x * lax.rsqrt(jnp.mean(x * x, axis=-1, keepdims=True) + NORM_EPS) * g


def _rope128(x, cos2, sin2):
    lane = lax.broadcasted_iota(jnp.int32, x.shape, 1)
    first_half = (lane % ROPE_DIM) < (ROPE_DIM // 2)
    partner = jnp.where(first_half, pltpu.roll(x, LANES - ROPE_DIM // 2, 1), pltpu.roll(x, ROPE_DIM // 2, 1))
    return x * cos2 + partner * sin2


def _inproj_kernel(x_ref, g1_ref, win_ref, qg_ref, kvg_ref, wuq_ref, wkt_ref, wv_ref, cos_ref, sin_ref,
                   uf_ref, q_ref, kt_ref, v_ref, sa_ref, sb_ref):
    bf = jnp.bfloat16
    f32 = jnp.float32
    h = _rms(x_ref[0], g1_ref[...]).astype(bf)

    def proj(c0, width):
        return jnp.dot(h, win_ref[:, c0:c0 + width], preferred_element_type=f32)

    cos2 = cos_ref[...]
    sin2 = sin_ref[...]

    ckv_raw = proj(_C_CKV, KV_RANK)
    kr_raw = proj(_C_KR, LANES)
    cq_raw = proj(_C_CQ, Q_RANK)

    ckv = _rms(ckv_raw, kvg_ref[...])
    ckv_t = ckv.T.astype(bf)
    k_t = jnp.dot(wkt_ref[...], ckv_t, preferred_element_type=f32)
    vals = jnp.dot(ckv.astype(bf), wv_ref[...], preferred_element_type=f32)
    cq = _rms(cq_raw, qg_ref[...]).astype(bf)
    q = jnp.dot(cq, wuq_ref[...], preferred_element_type=f32) * (ATTN_SCALE * LOG2_E)

    kr_t = _rope128(kr_raw, cos2, sin2).T[:ROPE_DIM].astype(bf)
    for hd in range(N_HEADS):
        kt_ref[0, hd, 0, :NOPE_DIM, :] = k_t[hd * NOPE_DIM:(hd + 1) * NOPE_DIM].astype(bf)
        kt_ref[0, hd, 0, NOPE_DIM:, :] = kr_t
        v_ref[0, hd] = vals[:, hd * V_DIM:(hd + 1) * V_DIM].astype(bf)

    rope0 = N_HEADS * NOPE_DIM
    for pair in range(N_HEADS // 2):
        qr = _rope128(q[:, rope0 + pair * LANES:rope0 + (pair + 1) * LANES], cos2, sin2).astype(bf)
        for sub in range(2):
            hd = 2 * pair + sub
            q_ref[0, hd, :, :NOPE_DIM] = q[:, hd * NOPE_DIM:(hd + 1) * NOPE_DIM].astype(bf)
            q_ref[0, hd, :, NOPE_DIM:] = qr[:, sub * ROPE_DIM:(sub + 1) * ROPE_DIM]

    uf_ref[0] = proj(_C_UF, FOURIER_DIM)
    sa_ref[0] = jax.nn.sigmoid(proj(_C_GA, D_MODEL)).astype(bf)
    sb_ref[0] = jax.nn.sigmoid(proj(_C_GB, D_MODEL)).astype(bf)


def _const_spec(shape):
    return pl.BlockSpec(shape, lambda *_: (0,) * len(shape), pipeline_mode=pl.Buffered(1))


def _inproj(x, g1, win, qg, kvg, wuq, wkt, wv, cos2, sin2, tm):
    b, s, _ = x.shape
    tok = lambda bi, si: (bi, si, 0)
    head = lambda bi, si: (bi, 0, si, 0)
    bf = jnp.bfloat16
    return pl.pallas_call(
        _inproj_kernel,
        grid=(b, s // tm),
        in_specs=[
            pl.BlockSpec((1, tm, D_MODEL), tok),
            _const_spec((1, D_MODEL)),
            _const_spec((D_MODEL, IN_COLS)),
            _const_spec((1, Q_RANK)),
            _const_spec((1, KV_RANK)),
            _const_spec((Q_RANK, N_HEADS * QK_DIM)),
            _const_spec((N_HEADS * NOPE_DIM, KV_RANK)),
            _const_spec((KV_RANK, N_HEADS * V_DIM)),
            pl.BlockSpec((tm, LANES), lambda bi, si: (si, 0)),
            pl.BlockSpec((tm, LANES), lambda bi, si: (si, 0)),
        ],
        out_specs=[
            pl.BlockSpec((1, tm, FOURIER_DIM), tok),
            pl.BlockSpec((1, N_HEADS, tm, QK_DIM), head),
            pl.BlockSpec((1, N_HEADS, 1, QK_DIM, tm), lambda bi, si: (bi, 0, si, 0, 0)),
            pl.BlockSpec((1, N_HEADS, tm, V_DIM), head),
            pl.BlockSpec((1, tm, D_MODEL), tok),
            pl.BlockSpec((1, tm, D_MODEL), tok),
        ],
        out_shape=[
            jax.ShapeDtypeStruct((b, s, FOURIER_DIM), jnp.float32),
            jax.ShapeDtypeStruct((b, N_HEADS, s, QK_DIM), bf),
            jax.ShapeDtypeStruct((b, N_HEADS, s // tm, QK_DIM, tm), bf),
            jax.ShapeDtypeStruct((b, N_HEADS, s, V_DIM), bf),
            jax.ShapeDtypeStruct((b, s, D_MODEL), bf),
            jax.ShapeDtypeStruct((b, s, D_MODEL), bf),
        ],
        compiler_params=pltpu.CompilerParams(
            dimension_semantics=("parallel", "parallel"), vmem_limit_bytes=VMEM_LIMIT_BYTES),
        name="inproj",
    )(x, g1, win, qg, kvg, wuq, wkt, wv, cos2, sin2)


def _cneg(v):
    return None if v is None else -v


def _cadd(a, b):
    if a is None:
        return b
    if b is None:
        return a
    return a + b


def _csub(a, b):
    if b is None:
        return a
    if a is None:
        return -b
    return a - b


def _fft(xs):
    n = len(xs)
    if n == 1:
        return xs
    ev = _fft(xs[0::2])
    od = _fft(xs[1::2])
    out = [None] * n
    for k in range(n // 2):
        re, im = od[k]
        if k == 0:
            tr, ti = re, im
        elif 4 * k == n:
            tr, ti = im, _cneg(re)
        else:
            c = math.cos(2 * math.pi * k / n)
            s = math.sin(2 * math.pi * k / n)
            tr = _cadd(None if re is None else re * c, None if im is None else im * s)
            ti = _csub(None if im is None else im * c, None if re is None else re * s)
        out[k] = (_cadd(ev[k][0], tr), _cadd(ev[k][1], ti))
        out[k + n // 2] = (_csub(ev[k][0], tr), _csub(ev[k][1], ti))
    return out


def _dft16_real(rows):
    out = _fft([(r, None) for r in rows])
    zero = jnp.zeros_like(rows[0])
    out = [(zero if re is None else re, zero if im is None else im) for re, im in out]
    for k in range(DFT_RADIX // 2 + 1, DFT_RADIX):
        re, im = out[DFT_RADIX - k]
        out[k] = (re, -im)
    return out


def _fourier_kernel(n2, xr_ref, xm_ref, twc_ref, tws_ref, dmat_ref, cs_ref, y_ref, x0_ref, z_ref, z0_ref):
    m = n2 - 1
    bf = jnp.bfloat16
    x0_ref[0:N_META, :] = xm_ref[...]
    x0_ref[N_META:N_META + m, :] = xr_ref[0, 0:m, :]

    def run_start(j1):
        return j1 * n2 - N_META

    dc = _dft16_real([x0_ref[0:1, :]] + [xr_ref[0, run_start(j1):run_start(j1) + 1, :] for j1 in range(1, DFT_RADIX)])
    zw = 2 * GROUP_DIM
    for k1 in range(DFT_RADIX):
        z0_ref[0:1, k1 * zw:k1 * zw + GROUP_DIM] = dc[k1][0]
        z0_ref[0:1, k1 * zw + GROUP_DIM:(k1 + 1) * zw] = dc[k1][1]

    def chunk(c, carry):
        r0 = pl.multiple_of(c * 8, 8)
        rows = [x0_ref[pl.ds(r0 + 1, 8), :]]
        rows += [xr_ref[0, pl.ds(r0 + (run_start(j1) + 1), 8), :] for j1 in range(1, DFT_RADIX)]
        a = _dft16_real(rows)
        for k1 in range(DFT_RADIX):
            ar, ai = a[k1]
            if k1 == 0:
                zr, zi = ar, ai
            else:
                tc = twc_ref[k1, pl.ds(r0, 8), :]
                ts = tws_ref[k1, pl.ds(r0, 8), :]
                zr = tc * ar + ts * ai
                zi = tc * ai - ts * ar
            z_ref[pl.ds(r0, 8), k1 * zw:k1 * zw + GROUP_DIM] = zr
            z_ref[pl.ds(r0, 8), k1 * zw + GROUP_DIM:(k1 + 1) * zw] = zi
            z_ref[pl.ds(r0 + m, 8), k1 * zw:k1 * zw + GROUP_DIM] = zi
            z_ref[pl.ds(r0 + m, 8), k1 * zw + GROUP_DIM:(k1 + 1) * zw] = -zr
        return carry

    lax.fori_loop(0, m // 8, chunk, 0)

    scale = 1.0 / math.sqrt(GROUP_DIM * DFT_RADIX * n2)
    for k0 in range(0, DFT_RADIX, DFT_K1_BATCH):
        cols = slice(k0 * zw, (k0 + DFT_K1_BATCH) * zw)
        p = jnp.dot(dmat_ref[...], z_ref[:, cols].astype(bf), preferred_element_type=jnp.float32)
        p = (p + z0_ref[0:1, cols]).astype(bf)
        stacked = jnp.concatenate([p[:, i * zw:(i + 1) * zw] for i in range(DFT_K1_BATCH)], axis=0)
        y = jnp.dot(stacked, cs_ref[...], preferred_element_type=jnp.float32) * scale
        for i in range(DFT_K1_BATCH):
            y_ref[0, 0, pl.ds(k0 + i, m, stride=DFT_RADIX), :] = y[i * m:(i + 1) * m]


def _fourier(uf, uf_meta, twc, tws, dmat, cs):
    b, s, _ = uf.shape
    n2 = (s + N_META) // DFT_RADIX
    m = n2 - 1
    out = pl.pallas_call(
        functools.partial(_fourier_kernel, n2),
        grid=(b, N_GROUPS),
        in_specs=[
            pl.BlockSpec((1, s, GROUP_DIM), lambda bi, g: (bi, 0, g)),
            pl.BlockSpec((N_META, GROUP_DIM), lambda bi, g: (0, g)),
            pl.BlockSpec((DFT_RADIX, m, GROUP_DIM), lambda bi, g: (0, 0, 0)),
            pl.BlockSpec((DFT_RADIX, m, GROUP_DIM), lambda bi, g: (0, 0, 0)),
            pl.BlockSpec((m, 2 * m), lambda bi, g: (0, 0)),
            pl.BlockSpec((2 * GROUP_DIM, GROUP_DIM), lambda bi, g: (0, 0)),
        ],
        out_specs=pl.BlockSpec((1, 1, s, GROUP_DIM), lambda bi, g: (bi, g, 0, 0)),
        out_shape=jax.ShapeDtypeStruct((b, N_GROUPS, s, GROUP_DIM), jnp.float32),
        scratch_shapes=[
            pltpu.VMEM((N_META + m, GROUP_DIM), jnp.float32),
            pltpu.VMEM((2 * m, DFT_RADIX * 2 * GROUP_DIM), jnp.float32),
            pltpu.VMEM((8, DFT_RADIX * 2 * GROUP_DIM), jnp.float32),
        ],
        compiler_params=pltpu.CompilerParams(
            dimension_semantics=("parallel", "parallel"), vmem_limit_bytes=VMEM_LIMIT_BYTES),
        name="fourier",
    )(uf, uf_meta, twc, tws, dmat, cs)
    return out


def _fourier_tables(s):
    n2 = (s + N_META) // DFT_RADIX
    length = DFT_RADIX * n2
    j2 = np.arange(1, n2, dtype=np.int64)
    k1 = np.arange(DFT_RADIX, dtype=np.int64)
    phi = 2.0 * np.pi * ((k1[:, None] * j2[None, :]) % length) / length
    twc = jnp.broadcast_to(jnp.asarray(np.cos(phi), jnp.float32)[:, :, None], (DFT_RADIX, n2 - 1, GROUP_DIM))
    tws = jnp.broadcast_to(jnp.asarray(np.sin(phi), jnp.float32)[:, :, None], (DFT_RADIX, n2 - 1, GROUP_DIM))
    theta = 2.0 * np.pi * ((j2[:, None] * j2[None, :]) % n2) / n2
    dmat = jnp.asarray(np.concatenate([np.cos(theta), np.sin(theta)], axis=1), jnp.float32)
    c = np.arange(GROUP_DIM, dtype=np.int64)
    psi = 2.0 * np.pi * ((c[:, None] * c[None, :]) % GROUP_DIM) / GROUP_DIM
    cs = jnp.asarray(np.concatenate([np.cos(psi), np.sin(psi)], axis=0), jnp.float32)
    return twc, tws, dmat.astype(jnp.bfloat16), cs.astype(jnp.bfloat16)


def _attn_kernel(q_ref, qn_ref, kt_ref, ktn_ref, v_ref, kmt_ref, vm_ref, o_ref, s_ref, pm_ref):
    bf = jnp.bfloat16
    f32 = jnp.float32
    tq = o_ref.shape[1]
    n, _, tk = kt_ref.shape[2:]
    group = s_ref.shape[2] // tk
    n_groups = n // group

    def produce(slot, qt, key_chunk):
        pmax = None
        for j in range(group):
            s = jnp.dot(qt, key_chunk(j), preferred_element_type=f32)
            s_ref[slot, :, j * tk:(j + 1) * tk] = s
            for l in range(tk // LANES):
                blk = s[:, l * LANES:(l + 1) * LANES]
                pmax = blk if pmax is None else jnp.maximum(pmax, blk)
        pm_ref[slot] = pmax

    def with_ones(vals):
        return jnp.concatenate([vals, jnp.ones_like(vals)], axis=1)

    def absorb(slot, g, m_i, acc):
        pmax = pm_ref[slot]
        if g == 0:
            lane = lax.broadcasted_iota(jnp.int32, (tq, LANES), 1)
            s_meta = jnp.dot(q_ref[0, 0], kmt_ref[0, 0], preferred_element_type=f32)
            s_meta = jnp.where(lane < N_META, s_meta, -jnp.inf)
            pmax = jnp.maximum(pmax, s_meta)
        m_new = jnp.max(pmax, axis=1, keepdims=True)
        if g > 0:
            m_new = jnp.maximum(m_i, m_new)
        p = jnp.exp2((s_ref[slot] - m_new).astype(bf))
        pv = jnp.dot(p, with_ones(v_ref[0, 0, g * group * tk:(g + 1) * group * tk, :]), preferred_element_type=f32)
        if g == 0:
            p_meta = jnp.exp2((s_meta - m_new).astype(bf))
            return m_new, pv + jnp.dot(p_meta, with_ones(vm_ref[0, 0]), preferred_element_type=f32)
        return m_new, jnp.exp2(m_i - m_new) * acc + pv

    first_step = (pl.program_id(0) == 0) & (pl.program_id(1) == 0) & (pl.program_id(2) == 0)

    @pl.when(first_step)
    def _():
        produce(0, q_ref[0, 0], lambda j: kt_ref[0, 0, j])

    m = acc = None
    for g in range(n_groups):
        if g + 1 < n_groups:
            produce((g + 1) % 2, q_ref[0, 0], lambda j, g=g: kt_ref[0, 0, (g + 1) * group + j])
        else:
            produce(0, qn_ref[0, 0], lambda j: ktn_ref[0, 0, j])
        m, acc = absorb(g % 2, g, m, acc)
    o_ref[0] = (acc[:, :V_DIM] / acc[:, V_DIM:]).astype(bf)


def _attention(q, kt, v, kmt, vm, tq, tg):
    b, nh, s, _ = q.shape
    n, _, tk = kt.shape[2:]
    nq = s // tq
    assert n * tk == s and tg % tk == 0 and (s // tg) % 2 == 0

    def next_step(bi, h, qi):
        wrap_q = qi + 1 == nq
        wrap_h = wrap_q & (h + 1 == nh)
        bn = jnp.where(wrap_h, jnp.where(bi + 1 == b, 0, bi + 1), bi)
        hn = jnp.where(wrap_q, jnp.where(h + 1 == nh, 0, h + 1), h)
        return bn, hn, jnp.where(wrap_q, 0, qi + 1)

    def next_q(bi, h, qi):
        bn, hn, qn = next_step(bi, h, qi)
        return bn, hn, qn, 0

    def next_keys(bi, h, qi):
        bn, hn, _ = next_step(bi, h, qi)
        return bn, hn, 0, 0, 0

    return pl.pallas_call(
        _attn_kernel,
        grid=(b, nh, nq),
        in_specs=[
            pl.BlockSpec((1, 1, tq, QK_DIM), lambda bi, h, qi: (bi, h, qi, 0)),
            pl.BlockSpec((1, 1, tq, QK_DIM), next_q),
            pl.BlockSpec((1, 1, n, QK_DIM, tk), lambda bi, h, qi: (bi, h, 0, 0, 0)),
            pl.BlockSpec((1, 1, tg // tk, QK_DIM, tk), next_keys),
            pl.BlockSpec((1, 1, s, V_DIM), lambda bi, h, qi: (bi, h, 0, 0)),
            pl.BlockSpec((1, 1, QK_DIM, LANES), lambda bi, h, qi: (0, h, 0, 0)),
            pl.BlockSpec((1, 1, LANES, V_DIM), lambda bi, h, qi: (0, h, 0, 0)),
        ],
        out_specs=pl.BlockSpec((1, tq, V_DIM), lambda bi, h, qi: (bi, qi, h)),
        out_shape=jax.ShapeDtypeStruct((b, s, ATTN_DIM), jnp.bfloat16),
        scratch_shapes=[pltpu.VMEM((2, tq, tg), jnp.float32), pltpu.VMEM((2, tq, LANES), jnp.float32)],
        compiler_params=pltpu.CompilerParams(
            dimension_semantics=("arbitrary", "arbitrary", "arbitrary"), vmem_limit_bytes=VMEM_LIMIT_BYTES),
        name="attention",
    )(q, q, kt, kt, v, kmt, vm)


def _tail_kernel(x_ref, y_ref, o_ref, sa_ref, sb_ref, wfo_ref, wao_ref, wo_ref, g2_ref, wg_ref, wu_ref, wd_ref,
                 gf_ref, out_ref):
    bf = jnp.bfloat16
    f32 = jnp.float32
    yf = jnp.concatenate([y_ref[0, g] for g in range(N_GROUPS)], axis=1).astype(bf)
    ya = jnp.dot(yf, wfo_ref[...], preferred_element_type=f32)
    yb = jnp.dot(o_ref[0], wao_ref[...], preferred_element_type=f32)
    merged = sa_ref[0].astype(f32) * ya + sb_ref[0].astype(f32) * yb
    x1 = x_ref[0] + jnp.dot(merged.astype(bf), wo_ref[...], preferred_element_type=f32)
    h2 = _rms(x1, g2_ref[...]).astype(bf)
    x2 = x1
    for c0 in range(0, D_FF, FF_CHUNK):
        c1 = min(c0 + FF_CHUNK, D_FF)
        gate = jnp.dot(h2, wg_ref[:, c0:c1], preferred_element_type=f32)
        up = jnp.dot(h2, wu_ref[:, c0:c1], preferred_element_type=f32)
        act = (gate * jax.nn.sigmoid(gate) * up).astype(bf)
        x2 = x2 + jnp.dot(act, wd_ref[c0:c1, :], preferred_element_type=f32)
    out_ref[0] = _rms(x2, gf_ref[...])


def _tail(x, y, o, sa, sb, wfo, wao, wo, g2, wg, wu, wd, gf, tm):
    b, s, _ = x.shape
    tok = lambda bi, si: (bi, si, 0)
    const = lambda bi, si: (0, 0)

    def weight(shape):
        return pl.BlockSpec(shape, const, pipeline_mode=pl.Buffered(1))

    return pl.pallas_call(
        _tail_kernel,
        grid=(b, s // tm),
        in_specs=[
            pl.BlockSpec((1, tm, D_MODEL), tok),
            pl.BlockSpec((1, N_GROUPS, tm, GROUP_DIM), lambda bi, si: (bi, 0, si, 0)),
            pl.BlockSpec((1, tm, ATTN_DIM), tok),
            pl.BlockSpec((1, tm, D_MODEL), tok),
            pl.BlockSpec((1, tm, D_MODEL), tok),
            weight((FOURIER_DIM, D_MODEL)),
            weight((ATTN_DIM, D_MODEL)),
            weight((D_MODEL, D_MODEL)),
            pl.BlockSpec((1, D_MODEL), const),
            weight((D_MODEL, D_FF)),
            weight((D_MODEL, D_FF)),
            weight((D_FF, D_MODEL)),
            pl.BlockSpec((1, D_MODEL), const),
        ],
        out_specs=pl.BlockSpec((1, tm, D_MODEL), tok),
        out_shape=jax.ShapeDtypeStruct((b, s, D_MODEL), jnp.float32),
        compiler_params=pltpu.CompilerParams(
            dimension_semantics=("parallel", "parallel"), vmem_limit_bytes=VMEM_LIMIT_BYTES),
        name="tail",
    )(x, y, o, sa, sb, wfo, wao, wo, g2, wg, wu, wd, gf)


def _rope_tables(start, length):
    half = ROPE_DIM // 2
    lane = np.arange(LANES)
    inv = 1.0 / (ROPE_THETA ** (jnp.arange(0, ROPE_DIM, 2, dtype=jnp.float32) / ROPE_DIM))
    inv = inv[lane % half]
    sign = jnp.asarray(np.where((lane // half) % 2 == 0, -1.0, 1.0), jnp.float32)
    ang = jnp.arange(start, start + length, dtype=jnp.float32)[:, None] * inv[None, :]
    return jnp.cos(ang), jnp.sin(ang) * sign


def _prepare_weights(norm1_g, w_in, q_norm_g, kv_norm_g, w_uq, w_ukv, w_fourier_out, w_attn_out, w_o, norm2_g,
                     w_ffn_gate, w_ffn_up, w_ffn_down, final_norm_g):
    bf = jnp.bfloat16
    w = w_in[0]
    s_uf, s_cq, s_ckv, s_kr = FOURIER_DIM, FOURIER_DIM + Q_RANK, FOURIER_DIM + Q_RANK + KV_RANK, \
        FOURIER_DIM + Q_RANK + KV_RANK + ROPE_DIM
    win = jnp.concatenate(
        [w[:, :s_ckv], w[:, s_kr:], w[:, s_ckv:s_kr], jnp.zeros((D_MODEL, LANES - ROPE_DIM), w.dtype)], axis=1).astype(bf)
    wq = w_uq[0].reshape(Q_RANK, N_HEADS, QK_DIM)
    wuq = jnp.concatenate([wq[:, :, :NOPE_DIM].reshape(Q_RANK, -1), wq[:, :, NOPE_DIM:].reshape(Q_RANK, -1)], axis=1).astype(bf)
    wkv = w_ukv[0].reshape(KV_RANK, N_HEADS, NOPE_DIM + V_DIM)
    wkt = wkv[:, :, :NOPE_DIM].reshape(KV_RANK, -1).T.astype(bf)
    wv = wkv[:, :, NOPE_DIM:].reshape(KV_RANK, -1).astype(bf)
    row = lambda g: g.reshape(1, -1).astype(jnp.float32)
    return dict(
        g1=row(norm1_g[0]), win=win, qg=row(q_norm_g[0]), kvg=row(kv_norm_g[0]), wuq=wuq, wkt=wkt, wv=wv,
        wfo=w_fourier_out[0].astype(bf), wao=w_attn_out[0].astype(bf), wo=w_o[0].astype(bf), g2=row(norm2_g[0]),
        wg=w_ffn_gate[0].astype(bf), wu=w_ffn_up[0].astype(bf), wd=w_ffn_down[0].astype(bf), gf=row(final_norm_g))


def _project(p, x, cos2, sin2, tm):
    return _inproj(x, p["g1"], p["win"], p["qg"], p["kvg"], p["wuq"], p["wkt"], p["wv"], cos2, sin2, tm)


def _meta_projection(meta, p):
    meta_pad = jnp.zeros((1, LANES, D_MODEL), meta.dtype).at[0, :N_META].set(meta)
    uf_m, _, kt_m, v_m, _, _ = _project(p, meta_pad, *_rope_tables(0, LANES), LANES)
    return uf_m[0, :N_META], kt_m[:, :, 0], v_m


def _trunk(x, meta_proj, rope, p, tk, tq, tg, tm_tail):
    s = x.shape[1]
    uf_m, kt_m, v_m = meta_proj
    cos2, sin2 = rope
    uf, q, kt, v, sa, sb = _project(p, x, cos2, sin2, tk)
    y = _fourier(uf, uf_m, *_fourier_tables(s))
    o = _attention(q, kt, v, kt_m, v_m, tq, tg)
    return _tail(x, y, o, sa, sb, p["wfo"], p["wao"], p["wo"], p["g2"], p["wg"], p["wu"], p["wd"], p["gf"], tm_tail)


def kernel(x_prompt, x_sample, meta_tokens, norm1_g, w_in, q_norm_g, kv_norm_g, w_uq, w_ukv, w_fourier_out,
           w_attn_out, w_o, norm2_g, w_ffn_gate, w_ffn_up, w_ffn_down, final_norm_g):
    p = _prepare_weights(norm1_g, w_in, q_norm_g, kv_norm_g, w_uq, w_ukv, w_fourier_out, w_attn_out, w_o, norm2_g,
                         w_ffn_gate, w_ffn_up, w_ffn_down, final_norm_g)
    rope = _rope_tables(N_META, max(x_prompt.shape[1], x_sample.shape[1]))
    meta_proj = _meta_projection(meta_tokens, p)
    cfg = dict(tk=512, tq=1024, tg=2048, tm_tail=512)
    y_prompt = _trunk(x_prompt, meta_proj, rope, p, **cfg)
    y_sample = _trunk(x_sample, meta_proj, rope, p, **cfg)
    return (y_prompt, y_sample)
```

```python
import functools
import math

import jax
import jax.numpy as jnp
import numpy as np
from jax import lax
from jax.experimental import pallas as pl
from jax.experimental.pallas import tpu as pltpu

D_MODEL = 1024
N_META = 16
N_GROUPS = 4
GROUP_DIM = 128
FOURIER_DIM = N_GROUPS * GROUP_DIM
N_HEADS = 8
NOPE_DIM = 128
ROPE_DIM = 64
QK_DIM = NOPE_DIM + ROPE_DIM
V_DIM = 128
Q_RANK = 512
KV_RANK = 256
ATTN_DIM = N_HEADS * V_DIM
D_FF = 2816
ROPE_THETA = 10000.0
NORM_EPS = 1e-6
ATTN_SCALE = QK_DIM ** -0.5
LOG2_E = math.log2(math.e)
DFT_RADIX = 16

LANES = 128
BF16_SUBLANES = 16
VMEM_LIMIT_BYTES = 56 * 1024 * 1024
DFT_K1_BATCH = 4
FF_CHUNK = 1024

_C_UF = 0
_C_CQ = _C_UF + FOURIER_DIM
_C_CKV = _C_CQ + Q_RANK
_C_GA = _C_CKV + KV_RANK
_C_GB = _C_GA + D_MODEL
_C_KR = _C_GB + D_MODEL
IN_COLS = _C_KR + LANES


def _rms(x, g):
    return x * lax.rsqrt(jnp.mean(x * x, axis=-1, keepdims=True) + NORM_EPS) * g


def _rope128(x, cos2, sin2):
    lane = lax.broadcasted_iota(jnp.int32, x.shape, 1)
    first_half = (lane % ROPE_DIM) < (ROPE_DIM // 2)
    partner = jnp.where(first_half, pltpu.roll(x, LANES - ROPE_DIM // 2, 1), pltpu.roll(x, ROPE_DIM // 2, 1))
    return x * cos2 + partner * sin2


def _inproj_kernel(x_ref, g1_ref, win_ref, qg_ref, kvg_ref, wuq_ref, wkt_ref, wv_ref, cos_ref, sin_ref,
                   uf_ref, q_ref, kt_ref, v_ref, sa_ref, sb_ref):
    bf = jnp.bfloat16
    f32 = jnp.float32
    h = _rms(x_ref[0], g1_ref[...]).astype(bf)

    def proj(c0, width):
        return jnp.dot(h, win_ref[:, c0:c0 + width], preferred_element_type=f32)

    cos2 = cos_ref[...]
    sin2 = sin_ref[...]

    ckv_raw = proj(_C_CKV, KV_RANK)
    kr_raw = proj(_C_KR, LANES)
    cq_raw = proj(_C_CQ, Q_RANK)

    ckv = _rms(ckv_raw, kvg_ref[...])
    ckv_t = ckv.T.astype(bf)
    k_t = jnp.dot(wkt_ref[...], ckv_t, preferred_element_type=f32)
    vals = jnp.dot(ckv.astype(bf), wv_ref[...], preferred_element_type=f32)
    cq = _rms(cq_raw, qg_ref[...]).astype(bf)
    q = jnp.dot(cq, wuq_ref[...], preferred_element_type=f32) * (ATTN_SCALE * LOG2_E)

    kr_t = _rope128(kr_raw, cos2, sin2).T[:ROPE_DIM].astype(bf)
    for hd in range(N_HEADS):
        kt_ref[0, hd, 0, :NOPE_DIM, :] = k_t[hd * NOPE_DIM:(hd + 1) * NOPE_DIM].astype(bf)
        kt_ref[0, hd, 0, NOPE_DIM:, :] = kr_t
        v_ref[0, hd] = vals[:, hd * V_DIM:(hd + 1) * V_DIM].astype(bf)

    rope0 = N_HEADS * NOPE_DIM
    for pair in range(N_HEADS // 2):
        qr = _rope128(q[:, rope0 + pair * LANES:rope0 + (pair + 1) * LANES], cos2, sin2).astype(bf)
        for sub in range(2):
            hd = 2 * pair + sub
            q_ref[0, hd, :, :NOPE_DIM] = q[:, hd * NOPE_DIM:(hd + 1) * NOPE_DIM].astype(bf)
            q_ref[0, hd, :, NOPE_DIM:] = qr[:, sub * ROPE_DIM:(sub + 1) * ROPE_DIM]

    uf_ref[0] = proj(_C_UF, FOURIER_DIM)
    sa_ref[0] = jax.nn.sigmoid(proj(_C_GA, D_MODEL)).astype(bf)
    sb_ref[0] = jax.nn.sigmoid(proj(_C_GB, D_MODEL)).astype(bf)


def _const_spec(shape):
    return pl.BlockSpec(shape, lambda *_: (0,) * len(shape), pipeline_mode=pl.Buffered(1))


def _inproj(x, g1, win, qg, kvg, wuq, wkt, wv, cos2, sin2, tm):
    b, s, _ = x.shape
    tok = lambda bi, si: (bi, si, 0)
    head = lambda bi, si: (bi, 0, si, 0)
    bf = jnp.bfloat16
    return pl.pallas_call(
        _inproj_kernel,
        grid=(b, s // tm),
        in_specs=[
            pl.BlockSpec((1, tm, D_MODEL), tok),
            _const_spec((1, D_MODEL)),
            _const_spec((D_MODEL, IN_COLS)),
            _const_spec((1, Q_RANK)),
            _const_spec((1, KV_RANK)),
            _const_spec((Q_RANK, N_HEADS * QK_DIM)),
            _const_spec((N_HEADS * NOPE_DIM, KV_RANK)),
            _const_spec((KV_RANK, N_HEADS * V_DIM)),
            pl.BlockSpec((tm, LANES), lambda bi, si: (si, 0)),
            pl.BlockSpec((tm, LANES), lambda bi, si: (si, 0)),
        ],
        out_specs=[
            pl.BlockSpec((1, tm, FOURIER_DIM), tok),
            pl.BlockSpec((1, N_HEADS, tm, QK_DIM), head),
            pl.BlockSpec((1, N_HEADS, 1, QK_DIM, tm), lambda bi, si: (bi, 0, si, 0, 0)),
            pl.BlockSpec((1, N_HEADS, tm, V_DIM), head),
            pl.BlockSpec((1, tm, D_MODEL), tok),
            pl.BlockSpec((1, tm, D_MODEL), tok),
        ],
        out_shape=[
            jax.ShapeDtypeStruct((b, s, FOURIER_DIM), jnp.float32),
            jax.ShapeDtypeStruct((b, N_HEADS, s, QK_DIM), bf),
            jax.ShapeDtypeStruct((b, N_HEADS, s // tm, QK_DIM, tm), bf),
            jax.ShapeDtypeStruct((b, N_HEADS, s, V_DIM), bf),
            jax.ShapeDtypeStruct((b, s, D_MODEL), bf),
            jax.ShapeDtypeStruct((b, s, D_MODEL), bf),
        ],
        compiler_params=pltpu.CompilerParams(
            dimension_semantics=("parallel", "parallel"), vmem_limit_bytes=VMEM_LIMIT_BYTES),
        name="inproj",
    )(x, g1, win, qg, kvg, wuq, wkt, wv, cos2, sin2)


def _cneg(v):
    return None if v is None else -v


def _cadd(a, b):
    if a is None:
        return b
    if b is None:
        return a
    return a + b


def _csub(a, b):
    if b is None:
        return a
    if a is None:
        return -b
    return a - b


def _fft(xs):
    n = len(xs)
    if n == 1:
        return xs
    ev = _fft(xs[0::2])
    od = _fft(xs[1::2])
    out = [None] * n
    for k in range(n // 2):
        re, im = od[k]
        if k == 0:
            tr, ti = re, im
        elif 4 * k == n:
            tr, ti = im, _cneg(re)
        else:
            c = math.cos(2 * math.pi * k / n)
            s = math.sin(2 * math.pi * k / n)
            tr = _cadd(None if re is None else re * c, None if im is None else im * s)
            ti = _csub(None if im is None else im * c, None if re is None else re * s)
        out[k] = (_cadd(ev[k][0], tr), _cadd(ev[k][1], ti))
        out[k + n // 2] = (_csub(ev[k][0], tr), _csub(ev[k][1], ti))
    return out


def _dft16_real(rows):
    out = _fft([(r, None) for r in rows])
    zero = jnp.zeros_like(rows[0])
    out = [(zero if re is None else re, zero if im is None else im) for re, im in out]
    for k in range(DFT_RADIX // 2 + 1, DFT_RADIX):
        re, im = out[DFT_RADIX - k]
        out[k] = (re, -im)
    return out


def _fourier_kernel(n2, x0_ref, xm0_ref, xn_ref, xmn_ref, twc_ref, tws_ref, dmat_ref, cs_ref, y_ref,
                    xrun_ref, z_ref, z0_ref):
    m = n2 - 1
    bf = jnp.bfloat16
    zw = 2 * GROUP_DIM
    slot = pl.program_id(1) % 2

    def vpu_stage(x_ref, xm_ref, dst):
        xrun_ref[0:N_META, :] = xm_ref[...]
        xrun_ref[N_META:N_META + m, :] = x_ref[0, 0:m, :]

        def run_start(j1):
            return j1 * n2 - N_META

        dc = _dft16_real([xrun_ref[0:1, :]] + [x_ref[0, run_start(j1):run_start(j1) + 1, :]
                                              for j1 in range(1, DFT_RADIX)])
        for k1 in range(DFT_RADIX):
            z0_ref[dst, 0:1, k1 * zw:k1 * zw + GROUP_DIM] = dc[k1][0]
            z0_ref[dst, 0:1, k1 * zw + GROUP_DIM:(k1 + 1) * zw] = dc[k1][1]
        rc = BF16_SUBLANES
        for r0 in range(0, m, rc):
            rows = [xrun_ref[r0 + 1:r0 + 1 + rc, :]]
            rows += [x_ref[0, r0 + run_start(j1) + 1:r0 + run_start(j1) + 1 + rc, :] for j1 in range(1, DFT_RADIX)]
            a = _dft16_real(rows)
            for k1 in range(DFT_RADIX):
                ar, ai = a[k1]
                if k1 == 0:
                    zr, zi = ar, ai
                else:
                    tc = twc_ref[k1, r0:r0 + rc, :]
                    ts = tws_ref[k1, r0:r0 + rc, :]
                    zr = tc * ar + ts * ai
                    zi = tc * ai - ts * ar
                z_ref[dst, r0:r0 + rc, k1 * zw:k1 * zw + GROUP_DIM] = zr.astype(bf)
                z_ref[dst, r0:r0 + rc, k1 * zw + GROUP_DIM:(k1 + 1) * zw] = zi.astype(bf)

    @pl.when((pl.program_id(0) == 0) & (pl.program_id(1) == 0))
    def _():
        vpu_stage(x0_ref, xm0_ref, 0)

    vpu_stage(xn_ref, xmn_ref, 1 - slot)

    scale = 1.0 / math.sqrt(GROUP_DIM * DFT_RADIX * n2)
    for k0 in range(0, DFT_RADIX, DFT_K1_BATCH):
        cols = slice(k0 * zw, (k0 + DFT_K1_BATCH) * zw)
        z = z_ref[slot, :, cols]
        swapped = jnp.concatenate(
            [blk for i in range(DFT_K1_BATCH)
             for blk in (z[:, i * zw + GROUP_DIM:(i + 1) * zw], -z[:, i * zw:i * zw + GROUP_DIM])], axis=1)
        rhs = jnp.concatenate([z, swapped], axis=0)
        p = jnp.dot(dmat_ref[...], rhs, preferred_element_type=jnp.float32)
        p = (p + z0_ref[slot, 0:1, cols]).astype(bf)
        stacked = jnp.concatenate([p[:, i * zw:(i + 1) * zw] for i in range(DFT_K1_BATCH)], axis=0)
        y = jnp.dot(stacked, cs_ref[...], preferred_element_type=jnp.float32) * scale
        for i in range(DFT_K1_BATCH):
            y_ref[0, 0, pl.ds(k0 + i, m, stride=DFT_RADIX), :] = y[i * m:(i + 1) * m]


def _fourier(uf, uf_meta, twc, tws, dmat, cs):
    b, s, _ = uf.shape
    n2 = (s + N_META) // DFT_RADIX
    m = n2 - 1
    assert N_GROUPS % 2 == 0

    def next_step(bi, g):
        wrap = g + 1 == N_GROUPS
        return jnp.where(wrap, jnp.where(bi + 1 == b, 0, bi + 1), bi), jnp.where(wrap, 0, g + 1)

    def next_x(bi, g):
        bn, gn = next_step(bi, g)
        return bn, 0, gn

    def next_meta(bi, g):
        return 0, next_step(bi, g)[1]

    return pl.pallas_call(
        functools.partial(_fourier_kernel, n2),
        grid=(b, N_GROUPS),
        in_specs=[
            _const_spec((1, s, GROUP_DIM)),
            _const_spec((N_META, GROUP_DIM)),
            pl.BlockSpec((1, s, GROUP_DIM), next_x),
            pl.BlockSpec((N_META, GROUP_DIM), next_meta),
            _const_spec((DFT_RADIX, m, GROUP_DIM)),
            _const_spec((DFT_RADIX, m, GROUP_DIM)),
            _const_spec((m, 2 * m)),
            _const_spec((2 * GROUP_DIM, GROUP_DIM)),
        ],
        out_specs=pl.BlockSpec((1, 1, s, GROUP_DIM), lambda bi, g: (bi, g, 0, 0)),
        out_shape=jax.ShapeDtypeStruct((b, N_GROUPS, s, GROUP_DIM), jnp.float32),
        scratch_shapes=[
            pltpu.VMEM((N_META + m, GROUP_DIM), jnp.float32),
            pltpu.VMEM((2, m, DFT_RADIX * 2 * GROUP_DIM), jnp.bfloat16),
            pltpu.VMEM((2, 8, DFT_RADIX * 2 * GROUP_DIM), jnp.float32),
        ],
        compiler_params=pltpu.CompilerParams(
            dimension_semantics=("arbitrary", "arbitrary"), vmem_limit_bytes=VMEM_LIMIT_BYTES),
        name="fourier",
    )(uf, uf_meta, uf, uf_meta, twc, tws, dmat, cs)


def _fourier_tables(s):
    n2 = (s + N_META) // DFT_RADIX
    length = DFT_RADIX * n2
    j2 = np.arange(1, n2, dtype=np.int64)
    k1 = np.arange(DFT_RADIX, dtype=np.int64)
    phi = 2.0 * np.pi * ((k1[:, None] * j2[None, :]) % length) / length
    twc = jnp.broadcast_to(jnp.asarray(np.cos(phi), jnp.float32)[:, :, None], (DFT_RADIX, n2 - 1, GROUP_DIM))
    tws = jnp.broadcast_to(jnp.asarray(np.sin(phi), jnp.float32)[:, :, None], (DFT_RADIX, n2 - 1, GROUP_DIM))
    theta = 2.0 * np.pi * ((j2[:, None] * j2[None, :]) % n2) / n2
    dmat = jnp.asarray(np.concatenate([np.cos(theta), np.sin(theta)], axis=1), jnp.float32)
    c = np.arange(GROUP_DIM, dtype=np.int64)
    psi = 2.0 * np.pi * ((c[:, None] * c[None, :]) % GROUP_DIM) / GROUP_DIM
    cs = jnp.asarray(np.concatenate([np.cos(psi), np.sin(psi)], axis=0), jnp.float32)
    return twc, tws, dmat.astype(jnp.bfloat16), cs.astype(jnp.bfloat16)


def _attn_kernel(q_ref, qn_ref, kt_ref, ktn_ref, v_ref, kmt_ref, vm_ref, o_ref, s_ref, pm_ref):
    bf = jnp.bfloat16
    f32 = jnp.float32
    tq = o_ref.shape[1]
    n, _, tk = kt_ref.shape[2:]
    group = s_ref.shape[2] // tk
    n_groups = n // group

    def produce(slot, qt, key_chunk):
        pmax = None
        for j in range(group):
            s = jnp.dot(qt, key_chunk(j), preferred_element_type=f32)
            s_ref[slot, :, j * tk:(j + 1) * tk] = s
            for l in range(tk // LANES):
                blk = s[:, l * LANES:(l + 1) * LANES]
                pmax = blk if pmax is None else jnp.maximum(pmax, blk)
        pm_ref[slot] = pmax

    def with_ones(vals):
        return jnp.concatenate([vals, jnp.ones_like(vals)], axis=1)

    def absorb(slot, g, m_i, acc):
        pmax = pm_ref[slot]
        if g == 0:
            lane = lax.broadcasted_iota(jnp.int32, (tq, LANES), 1)
            s_meta = jnp.dot(q_ref[0, 0], kmt_ref[0, 0], preferred_element_type=f32)
            s_meta = jnp.where(lane < N_META, s_meta, -jnp.inf)
            pmax = jnp.maximum(pmax, s_meta)
        m_new = jnp.max(pmax, axis=1, keepdims=True)
        if g > 0:
            m_new = jnp.maximum(m_i, m_new)
        p = jnp.exp2((s_ref[slot] - m_new).astype(bf))
        pv = jnp.dot(p, with_ones(v_ref[0, 0, g * group * tk:(g + 1) * group * tk, :]), preferred_element_type=f32)
        if g == 0:
            p_meta = jnp.exp2((s_meta - m_new).astype(bf))
            return m_new, pv + jnp.dot(p_meta, with_ones(vm_ref[0, 0]), preferred_element_type=f32)
        return m_new, jnp.exp2(m_i - m_new) * acc + pv

    first_step = (pl.program_id(0) == 0) & (pl.program_id(1) == 0) & (pl.program_id(2) == 0)

    @pl.when(first_step)
    def _():
        produce(0, q_ref[0, 0], lambda j: kt_ref[0, 0, j])

    m = acc = None
    for g in range(n_groups):
        if g + 1 < n_groups:
            produce((g + 1) % 2, q_ref[0, 0], lambda j, g=g: kt_ref[0, 0, (g + 1) * group + j])
        else:
            produce(0, qn_ref[0, 0], lambda j: ktn_ref[0, 0, j])
        m, acc = absorb(g % 2, g, m, acc)
    o_ref[0] = (acc[:, :V_DIM] / acc[:, V_DIM:]).astype(bf)


def _attention(q, kt, v, kmt, vm, tq, tg):
    b, nh, s, _ = q.shape
    n, _, tk = kt.shape[2:]
    nq = s // tq
    assert n * tk == s and tg % tk == 0 and (s // tg) % 2 == 0

    def next_step(bi, h, qi):
        wrap_q = qi + 1 == nq
        wrap_h = wrap_q & (h + 1 == nh)
        bn = jnp.where(wrap_h, jnp.where(bi + 1 == b, 0, bi + 1), bi)
        hn = jnp.where(wrap_q, jnp.where(h + 1 == nh, 0, h + 1), h)
        return bn, hn, jnp.where(wrap_q, 0, qi + 1)

    def next_q(bi, h, qi):
        bn, hn, qn = next_step(bi, h, qi)
        return bn, hn, qn, 0

    def next_keys(bi, h, qi):
        bn, hn, _ = next_step(bi, h, qi)
        return bn, hn, 0, 0, 0

    return pl.pallas_call(
        _attn_kernel,
        grid=(b, nh, nq),
        in_specs=[
            pl.BlockSpec((1, 1, tq, QK_DIM), lambda bi, h, qi: (bi, h, qi, 0)),
            pl.BlockSpec((1, 1, tq, QK_DIM), next_q),
            pl.BlockSpec((1, 1, n, QK_DIM, tk), lambda bi, h, qi: (bi, h, 0, 0, 0)),
            pl.BlockSpec((1, 1, tg // tk, QK_DIM, tk), next_keys),
            pl.BlockSpec((1, 1, s, V_DIM), lambda bi, h, qi: (bi, h, 0, 0)),
            pl.BlockSpec((1, 1, QK_DIM, LANES), lambda bi, h, qi: (0, h, 0, 0)),
            pl.BlockSpec((1, 1, LANES, V_DIM), lambda bi, h, qi: (0, h, 0, 0)),
        ],
        out_specs=pl.BlockSpec((1, tq, V_DIM), lambda bi, h, qi: (bi, qi, h)),
        out_shape=jax.ShapeDtypeStruct((b, s, ATTN_DIM), jnp.bfloat16),
        scratch_shapes=[pltpu.VMEM((2, tq, tg), jnp.float32), pltpu.VMEM((2, tq, LANES), jnp.float32)],
        compiler_params=pltpu.CompilerParams(
            dimension_semantics=("arbitrary", "arbitrary", "arbitrary"), vmem_limit_bytes=VMEM_LIMIT_BYTES),
        name="attention",
    )(q, q, kt, kt, v, kmt, vm)


def _tail_kernel(x_ref, y_ref, o_ref, sa_ref, sb_ref, wfo_ref, wao_ref, wo_ref, g2_ref, wg_ref, wu_ref, wd_ref,
                 gf_ref, out_ref):
    bf = jnp.bfloat16
    f32 = jnp.float32
    yf = jnp.concatenate([y_ref[0, g] for g in range(N_GROUPS)], axis=1).astype(bf)
    ya = jnp.dot(yf, wfo_ref[...], preferred_element_type=f32)
    yb = jnp.dot(o_ref[0], wao_ref[...], preferred_element_type=f32)
    merged = sa_ref[0].astype(f32) * ya + sb_ref[0].astype(f32) * yb
    x1 = x_ref[0] + jnp.dot(merged.astype(bf), wo_ref[...], preferred_element_type=f32)
    h2 = _rms(x1, g2_ref[...]).astype(bf)
    x2 = x1
    for c0 in range(0, D_FF, FF_CHUNK):
        c1 = min(c0 + FF_CHUNK, D_FF)
        gate = jnp.dot(h2, wg_ref[:, c0:c1], preferred_element_type=f32)
        up = jnp.dot(h2, wu_ref[:, c0:c1], preferred_element_type=f32)
        act = (gate * jax.nn.sigmoid(gate) * up).astype(bf)
        x2 = x2 + jnp.dot(act, wd_ref[c0:c1, :], preferred_element_type=f32)
    out_ref[0] = _rms(x2, gf_ref[...])


def _tail(x, y, o, sa, sb, wfo, wao, wo, g2, wg, wu, wd, gf, tm):
    b, s, _ = x.shape
    tok = lambda bi, si: (bi, si, 0)
    const = lambda bi, si: (0, 0)

    def weight(shape):
        return pl.BlockSpec(shape, const, pipeline_mode=pl.Buffered(1))

    return pl.pallas_call(
        _tail_kernel,
        grid=(b, s // tm),
        in_specs=[
            pl.BlockSpec((1, tm, D_MODEL), tok),
            pl.BlockSpec((1, N_GROUPS, tm, GROUP_DIM), lambda bi, si: (bi, 0, si, 0)),
            pl.BlockSpec((1, tm, ATTN_DIM), tok),
            pl.BlockSpec((1, tm, D_MODEL), tok),
            pl.BlockSpec((1, tm, D_MODEL), tok),
            weight((FOURIER_DIM, D_MODEL)),
            weight((ATTN_DIM, D_MODEL)),
            weight((D_MODEL, D_MODEL)),
            pl.BlockSpec((1, D_MODEL), const),
            weight((D_MODEL, D_FF)),
            weight((D_MODEL, D_FF)),
            weight((D_FF, D_MODEL)),
            pl.BlockSpec((1, D_MODEL), const),
        ],
        out_specs=pl.BlockSpec((1, tm, D_MODEL), tok),
        out_shape=jax.ShapeDtypeStruct((b, s, D_MODEL), jnp.float32),
        compiler_params=pltpu.CompilerParams(
            dimension_semantics=("parallel", "parallel"), vmem_limit_bytes=VMEM_LIMIT_BYTES),
        name="tail",
    )(x, y, o, sa, sb, wfo, wao, wo, g2, wg, wu, wd, gf)


def _rope_tables(start, length):
    half = ROPE_DIM // 2
    lane = np.arange(LANES)
    inv = 1.0 / (ROPE_THETA ** (jnp.arange(0, ROPE_DIM, 2, dtype=jnp.float32) / ROPE_DIM))
    inv = inv[lane % half]
    sign = jnp.asarray(np.where((lane // half) % 2 == 0, -1.0, 1.0), jnp.float32)
    ang = jnp.arange(start, start + length, dtype=jnp.float32)[:, None] * inv[None, :]
    return jnp.cos(ang), jnp.sin(ang) * sign


def _prepare_weights(norm1_g, w_in, q_norm_g, kv_norm_g, w_uq, w_ukv, w_fourier_out, w_attn_out, w_o, norm2_g,
                     w_ffn_gate, w_ffn_up, w_ffn_down, final_norm_g):
    bf = jnp.bfloat16
    w = w_in[0]
    s_uf, s_cq, s_ckv, s_kr = FOURIER_DIM, FOURIER_DIM + Q_RANK, FOURIER_DIM + Q_RANK + KV_RANK, \
        FOURIER_DIM + Q_RANK + KV_RANK + ROPE_DIM
    win = jnp.concatenate(
        [w[:, :s_ckv], w[:, s_kr:], w[:, s_ckv:s_kr], jnp.zeros((D_MODEL, LANES - ROPE_DIM), w.dtype)], axis=1).astype(bf)
    wq = w_uq[0].reshape(Q_RANK, N_HEADS, QK_DIM)
    wuq = jnp.concatenate([wq[:, :, :NOPE_DIM].reshape(Q_RANK, -1), wq[:, :, NOPE_DIM:].reshape(Q_RANK, -1)], axis=1).astype(bf)
    wkv = w_ukv[0].reshape(KV_RANK, N_HEADS, NOPE_DIM + V_DIM)
    wkt = wkv[:, :, :NOPE_DIM].reshape(KV_RANK, -1).T.astype(bf)
    wv = wkv[:, :, NOPE_DIM:].reshape(KV_RANK, -1).astype(bf)
    row = lambda g: g.reshape(1, -1).astype(jnp.float32)
    return dict(
        g1=row(norm1_g[0]), win=win, qg=row(q_norm_g[0]), kvg=row(kv_norm_g[0]), wuq=wuq, wkt=wkt, wv=wv,
        wfo=w_fourier_out[0].astype(bf), wao=w_attn_out[0].astype(bf), wo=w_o[0].astype(bf), g2=row(norm2_g[0]),
        wg=w_ffn_gate[0].astype(bf), wu=w_ffn_up[0].astype(bf), wd=w_ffn_down[0].astype(bf), gf=row(final_norm_g))


def _project(p, x, cos2, sin2, tm):
    return _inproj(x, p["g1"], p["win"], p["qg"], p["kvg"], p["wuq"], p["wkt"], p["wv"], cos2, sin2, tm)


def _meta_projection(meta, p):
    meta_pad = jnp.zeros((1, LANES, D_MODEL), meta.dtype).at[0, :N_META].set(meta)
    uf_m, _, kt_m, v_m, _, _ = _project(p, meta_pad, *_rope_tables(0, LANES), LANES)
    return uf_m[0, :N_META], kt_m[:, :, 0], v_m


def _trunk(x, meta_proj, rope, p, tk, tq, attn_groups, tm_tail):
    s = x.shape[1]
    tg = s // attn_groups
    uf_m, kt_m, v_m = meta_proj
    cos2, sin2 = rope
    uf, q, kt, v, sa, sb = _project(p, x, cos2, sin2, tk)
    y = _fourier(uf, uf_m, *_fourier_tables(s))
    o = _attention(q, kt, v, kt_m, v_m, tq, tg)
    return _tail(x, y, o, sa, sb, p["wfo"], p["wao"], p["wo"], p["g2"], p["wg"], p["wu"], p["wd"], p["gf"], tm_tail)


def kernel(x_prompt, x_sample, meta_tokens, norm1_g, w_in, q_norm_g, kv_norm_g, w_uq, w_ukv, w_fourier_out,
           w_attn_out, w_o, norm2_g, w_ffn_gate, w_ffn_up, w_ffn_down, final_norm_g):
    p = _prepare_weights(norm1_g, w_in, q_norm_g, kv_norm_g, w_uq, w_ukv, w_fourier_out, w_attn_out, w_o, norm2_g,
                         w_ffn_gate, w_ffn_up, w_ffn_down, final_norm_g)
    rope = _rope_tables(N_META, max(x_prompt.shape[1], x_sample.shape[1]))
    meta_proj = _meta_projection(meta_tokens, p)
    cfg = dict(tk=512, tq=1024, attn_groups=4, tm_tail=512)
    y_prompt = _trunk(x_prompt, meta_proj, rope, p, **cfg)
    y_sample = _trunk(x_sample, meta_proj, rope, p, **cfg)
    return (y_prompt, y_sample)
```

```python
import functools
import math

import jax
import jax.numpy as jnp
import numpy as np
from jax import lax
from jax.experimental import pallas as pl
from jax.experimental.pallas import tpu as pltpu

D_MODEL = 1024
N_META = 16
N_GROUPS = 4
GROUP_DIM = 128
FOURIER_DIM = N_GROUPS * GROUP_DIM
N_HEADS = 8
NOPE_DIM = 128
ROPE_DIM = 64
QK_DIM = NOPE_DIM + ROPE_DIM
V_DIM = 128
Q_RANK = 512
KV_RANK = 256
ATTN_DIM = N_HEADS * V_DIM
D_FF = 2816
ROPE_THETA = 10000.0
NORM_EPS = 1e-6
ATTN_SCALE = QK_DIM ** -0.5
LOG2_E = math.log2(math.e)
DFT_RADIX = 16

LANES = 128
BF16_SUBLANES = 16
VMEM_LIMIT_BYTES = 56 * 1024 * 1024
DFT_K1_BATCH = 4
FF_CHUNK = 1024

_C_UF = 0
_C_CQ = _C_UF + FOURIER_DIM
_C_CKV = _C_CQ + Q_RANK
_C_GA = _C_CKV + KV_RANK
_C_GB = _C_GA + D_MODEL
_C_KR = _C_GB + D_MODEL
IN_COLS = _C_KR + LANES


def _rms(x, g):
    return x * lax.rsqrt(jnp.mean(x * x, axis=-1, keepdims=True) + NORM_EPS) * g


def _rope128(x, cos2, sin2):
    lane = lax.broadcasted_iota(jnp.int32, x.shape, 1)
    first_half = (lane % ROPE_DIM) < (ROPE_DIM // 2)
    partner = jnp.where(first_half, pltpu.roll(x, LANES - ROPE_DIM // 2, 1), pltpu.roll(x, ROPE_DIM // 2, 1))
    return x * cos2 + partner * sin2


def _inproj_kernel(x_ref, g1_ref, win_ref, qg_ref, kvg_ref, wuq_ref, wkt_ref, wv_ref, cos_ref, sin_ref,
                   uf_ref, q_ref, kt_ref, v_ref, sa_ref, sb_ref):
    bf = jnp.bfloat16
    f32 = jnp.float32
    h = _rms(x_ref[0], g1_ref[...]).astype(bf)

    def proj(c0, width):
        return jnp.dot(h, win_ref[:, c0:c0 + width], preferred_element_type=f32)

    cos2 = cos_ref[...]
    sin2 = sin_ref[...]

    ckv_raw = proj(_C_CKV, KV_RANK)
    kr_raw = proj(_C_KR, LANES)
    cq_raw = proj(_C_CQ, Q_RANK)

    ckv = _rms(ckv_raw, kvg_ref[...])
    ckv_t = ckv.T.astype(bf)
    k_t = jnp.dot(wkt_ref[...], ckv_t, preferred_element_type=f32)
    vals = jnp.dot(ckv.astype(bf), wv_ref[...], preferred_element_type=f32)
    cq = _rms(cq_raw, qg_ref[...]).astype(bf)
    q = jnp.dot(cq, wuq_ref[...], preferred_element_type=f32) * (ATTN_SCALE * LOG2_E)

    kr_t = _rope128(kr_raw, cos2, sin2).T[:ROPE_DIM].astype(bf)
    for hd in range(N_HEADS):
        kt_ref[0, hd, 0, :NOPE_DIM, :] = k_t[hd * NOPE_DIM:(hd + 1) * NOPE_DIM].astype(bf)
        kt_ref[0, hd, 0, NOPE_DIM:, :] = kr_t
        v_ref[0, hd] = vals[:, hd * V_DIM:(hd + 1) * V_DIM].astype(bf)

    rope0 = N_HEADS * NOPE_DIM
    for pair in range(N_HEADS // 2):
        qr = _rope128(q[:, rope0 + pair * LANES:rope0 + (pair + 1) * LANES], cos2, sin2).astype(bf)
        for sub in range(2):
            hd = 2 * pair + sub
            q_ref[0, hd, :, :NOPE_DIM] = q[:, hd * NOPE_DIM:(hd + 1) * NOPE_DIM].astype(bf)
            q_ref[0, hd, :, NOPE_DIM:] = qr[:, sub * ROPE_DIM:(sub + 1) * ROPE_DIM]

    uf = proj(_C_UF, FOURIER_DIM)
    for g in range(N_GROUPS):
        uf_ref[0, g] = uf[:, g * GROUP_DIM:(g + 1) * GROUP_DIM]
    sa_ref[0] = jax.nn.sigmoid(proj(_C_GA, D_MODEL)).astype(bf)
    sb_ref[0] = jax.nn.sigmoid(proj(_C_GB, D_MODEL)).astype(bf)


def _const_spec(shape):
    return pl.BlockSpec(shape, lambda *_: (0,) * len(shape), pipeline_mode=pl.Buffered(1))


def _inproj(x, g1, win, qg, kvg, wuq, wkt, wv, cos2, sin2, tm):
    b, s, _ = x.shape
    tok = lambda bi, si: (bi, si, 0)
    head = lambda bi, si: (bi, 0, si, 0)
    bf = jnp.bfloat16
    return pl.pallas_call(
        _inproj_kernel,
        grid=(b, s // tm),
        in_specs=[
            pl.BlockSpec((1, tm, D_MODEL), tok),
            _const_spec((1, D_MODEL)),
            _const_spec((D_MODEL, IN_COLS)),
            _const_spec((1, Q_RANK)),
            _const_spec((1, KV_RANK)),
            _const_spec((Q_RANK, N_HEADS * QK_DIM)),
            _const_spec((N_HEADS * NOPE_DIM, KV_RANK)),
            _const_spec((KV_RANK, N_HEADS * V_DIM)),
            pl.BlockSpec((tm, LANES), lambda bi, si: (si, 0)),
            pl.BlockSpec((tm, LANES), lambda bi, si: (si, 0)),
        ],
        out_specs=[
            pl.BlockSpec((1, N_GROUPS, tm, GROUP_DIM), head),
            pl.BlockSpec((1, N_HEADS, tm, QK_DIM), head),
            pl.BlockSpec((1, N_HEADS, 1, QK_DIM, tm), lambda bi, si: (bi, 0, si, 0, 0)),
            pl.BlockSpec((1, N_HEADS, tm, V_DIM), head),
            pl.BlockSpec((1, tm, D_MODEL), tok),
            pl.BlockSpec((1, tm, D_MODEL), tok),
        ],
        out_shape=[
            jax.ShapeDtypeStruct((b, N_GROUPS, s, GROUP_DIM), jnp.float32),
            jax.ShapeDtypeStruct((b, N_HEADS, s, QK_DIM), bf),
            jax.ShapeDtypeStruct((b, N_HEADS, s // tm, QK_DIM, tm), bf),
            jax.ShapeDtypeStruct((b, N_HEADS, s, V_DIM), bf),
            jax.ShapeDtypeStruct((b, s, D_MODEL), bf),
            jax.ShapeDtypeStruct((b, s, D_MODEL), bf),
        ],
        compiler_params=pltpu.CompilerParams(
            dimension_semantics=("parallel", "parallel"), vmem_limit_bytes=VMEM_LIMIT_BYTES),
        name="inproj",
    )(x, g1, win, qg, kvg, wuq, wkt, wv, cos2, sin2)


def _cneg(v):
    return None if v is None else -v


def _cadd(a, b):
    if a is None:
        return b
    if b is None:
        return a
    return a + b


def _csub(a, b):
    if b is None:
        return a
    if a is None:
        return -b
    return a - b


def _fft(xs):
    n = len(xs)
    if n == 1:
        return xs
    ev = _fft(xs[0::2])
    od = _fft(xs[1::2])
    out = [None] * n
    for k in range(n // 2):
        re, im = od[k]
        if k == 0:
            tr, ti = re, im
        elif 4 * k == n:
            tr, ti = im, _cneg(re)
        else:
            c = math.cos(2 * math.pi * k / n)
            s = math.sin(2 * math.pi * k / n)
            tr = _cadd(None if re is None else re * c, None if im is None else im * s)
            ti = _csub(None if im is None else im * c, None if re is None else re * s)
        out[k] = (_cadd(ev[k][0], tr), _cadd(ev[k][1], ti))
        out[k + n // 2] = (_csub(ev[k][0], tr), _csub(ev[k][1], ti))
    return out


def _dft16_real(rows):
    out = _fft([(r, None) for r in rows])
    zero = jnp.zeros_like(rows[0])
    out = [(zero if re is None else re, zero if im is None else im) for re, im in out]
    for k in range(DFT_RADIX // 2 + 1, DFT_RADIX):
        re, im = out[DFT_RADIX - k]
        out[k] = (re, -im)
    return out


def _fourier_kernel(n2, x0_ref, xm0_ref, xn_ref, xmn_ref, twc_ref, tws_ref, dmat_ref, cs_ref, y_ref,
                    xrun_ref, z_ref, z0_ref):
    m = n2 - 1
    bf = jnp.bfloat16
    zw = 2 * GROUP_DIM
    slot = pl.program_id(1) % 2

    rc = BF16_SUBLANES

    def run_start(j1):
        return j1 * n2 - N_META

    def vpu_head(x_ref, xm_ref, dst):
        xrun_ref[0:N_META, :] = xm_ref[0]
        xrun_ref[N_META:N_META + m, :] = x_ref[0, 0, 0:m, :]
        dc = _dft16_real([xrun_ref[0:1, :]] + [x_ref[0, 0, run_start(j1):run_start(j1) + 1, :]
                                              for j1 in range(1, DFT_RADIX)])
        for k1 in range(DFT_RADIX):
            z0_ref[dst, 0:1, k1 * zw:k1 * zw + GROUP_DIM] = dc[k1][0]
            z0_ref[dst, 0:1, k1 * zw + GROUP_DIM:(k1 + 1) * zw] = dc[k1][1]

    def vpu_chunk(x_ref, dst, r0):
        rows = [xrun_ref[r0 + 1:r0 + 1 + rc, :]]
        rows += [x_ref[0, 0, r0 + run_start(j1) + 1:r0 + run_start(j1) + 1 + rc, :] for j1 in range(1, DFT_RADIX)]
        a = _dft16_real(rows)
        for k1 in range(DFT_RADIX):
            ar, ai = a[k1]
            if k1 == 0:
                zr, zi = ar, ai
            else:
                tc = twc_ref[k1, r0:r0 + rc, :]
                ts = tws_ref[k1, r0:r0 + rc, :]
                zr = tc * ar + ts * ai
                zi = tc * ai - ts * ar
            z_ref[dst, r0:r0 + rc, k1 * zw:k1 * zw + GROUP_DIM] = zr.astype(bf)
            z_ref[dst, r0:r0 + rc, k1 * zw + GROUP_DIM:(k1 + 1) * zw] = zi.astype(bf)

    scale = 1.0 / math.sqrt(GROUP_DIM * DFT_RADIX * n2)

    def mxu_batch(k0):
        cols = slice(k0 * zw, (k0 + DFT_K1_BATCH) * zw)
        z = z_ref[slot, :, cols]
        swapped = jnp.concatenate(
            [blk for i in range(DFT_K1_BATCH)
             for blk in (z[:, i * zw + GROUP_DIM:(i + 1) * zw], -z[:, i * zw:i * zw + GROUP_DIM])], axis=1)
        rhs = jnp.concatenate([z, swapped], axis=0)
        p = jnp.dot(dmat_ref[...], rhs, preferred_element_type=jnp.float32)
        p = (p + z0_ref[slot, 0:1, cols]).astype(bf)
        stacked = jnp.concatenate([p[:, i * zw:(i + 1) * zw] for i in range(DFT_K1_BATCH)], axis=0)
        y = jnp.dot(stacked, cs_ref[...], preferred_element_type=jnp.float32) * scale
        for i in range(DFT_K1_BATCH):
            y_ref[0, 0, pl.ds(k0 + i, m, stride=DFT_RADIX), :] = y[i * m:(i + 1) * m]

    @pl.when((pl.program_id(0) == 0) & (pl.program_id(1) == 0))
    def _():
        vpu_head(x0_ref, xm0_ref, 0)
        for r0 in range(0, m, rc):
            vpu_chunk(x0_ref, 0, r0)

    vpu_head(xn_ref, xmn_ref, 1 - slot)
    for r0 in range(0, m, rc):
        vpu_chunk(xn_ref, 1 - slot, r0)
    for k0 in range(0, DFT_RADIX, DFT_K1_BATCH):
        mxu_batch(k0)


def _fourier(uf, uf_meta, twc, tws, dmat, cs):
    b, _, s, _ = uf.shape
    n2 = (s + N_META) // DFT_RADIX
    m = n2 - 1
    assert N_GROUPS % 2 == 0

    def next_step(bi, g):
        wrap = g + 1 == N_GROUPS
        return jnp.where(wrap, jnp.where(bi + 1 == b, 0, bi + 1), bi), jnp.where(wrap, 0, g + 1)

    def next_x(bi, g):
        bn, gn = next_step(bi, g)
        return bn, gn, 0, 0

    def next_meta(bi, g):
        return next_step(bi, g)[1], 0, 0

    return pl.pallas_call(
        functools.partial(_fourier_kernel, n2),
        grid=(b, N_GROUPS),
        in_specs=[
            _const_spec((1, 1, s, GROUP_DIM)),
            _const_spec((1, N_META, GROUP_DIM)),
            pl.BlockSpec((1, 1, s, GROUP_DIM), next_x),
            pl.BlockSpec((1, N_META, GROUP_DIM), next_meta),
            _const_spec((DFT_RADIX, m, GROUP_DIM)),
            _const_spec((DFT_RADIX, m, GROUP_DIM)),
            _const_spec((m, 2 * m)),
            _const_spec((2 * GROUP_DIM, GROUP_DIM)),
        ],
        out_specs=pl.BlockSpec((1, 1, s, GROUP_DIM), lambda bi, g: (bi, g, 0, 0)),
        out_shape=jax.ShapeDtypeStruct((b, N_GROUPS, s, GROUP_DIM), jnp.float32),
        scratch_shapes=[
            pltpu.VMEM((N_META + m, GROUP_DIM), jnp.float32),
            pltpu.VMEM((2, m, DFT_RADIX * 2 * GROUP_DIM), jnp.bfloat16),
            pltpu.VMEM((2, 8, DFT_RADIX * 2 * GROUP_DIM), jnp.float32),
        ],
        compiler_params=pltpu.CompilerParams(
            dimension_semantics=("arbitrary", "arbitrary"), vmem_limit_bytes=VMEM_LIMIT_BYTES),
        name="fourier",
    )(uf, uf_meta, uf, uf_meta, twc, tws, dmat, cs)


def _fourier_tables(s):
    n2 = (s + N_META) // DFT_RADIX
    length = DFT_RADIX * n2
    j2 = np.arange(1, n2, dtype=np.int64)
    k1 = np.arange(DFT_RADIX, dtype=np.int64)
    phi = 2.0 * np.pi * ((k1[:, None] * j2[None, :]) % length) / length
    twc = jnp.broadcast_to(jnp.asarray(np.cos(phi), jnp.float32)[:, :, None], (DFT_RADIX, n2 - 1, GROUP_DIM))
    tws = jnp.broadcast_to(jnp.asarray(np.sin(phi), jnp.float32)[:, :, None], (DFT_RADIX, n2 - 1, GROUP_DIM))
    theta = 2.0 * np.pi * ((j2[:, None] * j2[None, :]) % n2) / n2
    dmat = jnp.asarray(np.concatenate([np.cos(theta), np.sin(theta)], axis=1), jnp.float32)
    c = np.arange(GROUP_DIM, dtype=np.int64)
    psi = 2.0 * np.pi * ((c[:, None] * c[None, :]) % GROUP_DIM) / GROUP_DIM
    cs = jnp.asarray(np.concatenate([np.cos(psi), np.sin(psi)], axis=0), jnp.float32)
    return twc, tws, dmat.astype(jnp.bfloat16), cs.astype(jnp.bfloat16)


def _attn_kernel(q_ref, qn_ref, kt_ref, ktn_ref, v_ref, kmt_ref, vm_ref, o_ref, s_ref, pm_ref):
    bf = jnp.bfloat16
    f32 = jnp.float32
    tq = o_ref.shape[1]
    n, _, tk = kt_ref.shape[2:]
    group = s_ref.shape[2] // tk
    n_groups = n // group

    def produce(slot, qt, key_chunk):
        pmax = None
        for j in range(group):
            s = jnp.dot(qt, key_chunk(j), preferred_element_type=f32)
            s_ref[slot, :, j * tk:(j + 1) * tk] = s
            for l in range(tk // LANES):
                blk = s[:, l * LANES:(l + 1) * LANES]
                pmax = blk if pmax is None else jnp.maximum(pmax, blk)
        pm_ref[slot] = pmax

    def with_ones(vals):
        return jnp.concatenate([vals, jnp.ones_like(vals)], axis=1)

    def absorb(slot, g, m_i, acc):
        pmax = pm_ref[slot]
        if g == 0:
            lane = lax.broadcasted_iota(jnp.int32, (tq, LANES), 1)
            s_meta = jnp.dot(q_ref[0, 0], kmt_ref[0, 0], preferred_element_type=f32)
            s_meta = jnp.where(lane < N_META, s_meta, -jnp.inf)
            pmax = jnp.maximum(pmax, s_meta)
        m_new = jnp.max(pmax, axis=1, keepdims=True)
        if g > 0:
            m_new = jnp.maximum(m_i, m_new)
        p = jnp.exp2((s_ref[slot] - m_new).astype(bf))
        pv = jnp.dot(p, with_ones(v_ref[0, 0, g * group * tk:(g + 1) * group * tk, :]), preferred_element_type=f32)
        if g == 0:
            p_meta = jnp.exp2((s_meta - m_new).astype(bf))
            return m_new, pv + jnp.dot(p_meta, with_ones(vm_ref[0, 0]), preferred_element_type=f32)
        return m_new, jnp.exp2(m_i - m_new) * acc + pv

    first_step = (pl.program_id(0) == 0) & (pl.program_id(1) == 0) & (pl.program_id(2) == 0)

    @pl.when(first_step)
    def _():
        produce(0, q_ref[0, 0], lambda j: kt_ref[0, 0, j])

    m = acc = None
    for g in range(n_groups):
        if g + 1 < n_groups:
            produce((g + 1) % 2, q_ref[0, 0], lambda j, g=g: kt_ref[0, 0, (g + 1) * group + j])
        else:
            produce(0, qn_ref[0, 0], lambda j: ktn_ref[0, 0, j])
        m, acc = absorb(g % 2, g, m, acc)
    o_ref[0] = (acc[:, :V_DIM] / acc[:, V_DIM:]).astype(bf)


def _attention(q, kt, v, kmt, vm, tq, tg):
    b, nh, s, _ = q.shape
    n, _, tk = kt.shape[2:]
    nq = s // tq
    assert n * tk == s and tg % tk == 0 and (s // tg) % 2 == 0

    def next_step(bi, h, qi):
        wrap_q = qi + 1 == nq
        wrap_h = wrap_q & (h + 1 == nh)
        bn = jnp.where(wrap_h, jnp.where(bi + 1 == b, 0, bi + 1), bi)
        hn = jnp.where(wrap_q, jnp.where(h + 1 == nh, 0, h + 1), h)
        return bn, hn, jnp.where(wrap_q, 0, qi + 1)

    def next_q(bi, h, qi):
        bn, hn, qn = next_step(bi, h, qi)
        return bn, hn, qn, 0

    def next_keys(bi, h, qi):
        bn, hn, _ = next_step(bi, h, qi)
        return bn, hn, 0, 0, 0

    return pl.pallas_call(
        _attn_kernel,
        grid=(b, nh, nq),
        in_specs=[
            pl.BlockSpec((1, 1, tq, QK_DIM), lambda bi, h, qi: (bi, h, qi, 0)),
            pl.BlockSpec((1, 1, tq, QK_DIM), next_q),
            pl.BlockSpec((1, 1, n, QK_DIM, tk), lambda bi, h, qi: (bi, h, 0, 0, 0)),
            pl.BlockSpec((1, 1, tg // tk, QK_DIM, tk), next_keys),
            pl.BlockSpec((1, 1, s, V_DIM), lambda bi, h, qi: (bi, h, 0, 0)),
            pl.BlockSpec((1, 1, QK_DIM, LANES), lambda bi, h, qi: (0, h, 0, 0)),
            pl.BlockSpec((1, 1, LANES, V_DIM), lambda bi, h, qi: (0, h, 0, 0)),
        ],
        out_specs=pl.BlockSpec((1, tq, V_DIM), lambda bi, h, qi: (bi, qi, h)),
        out_shape=jax.ShapeDtypeStruct((b, s, ATTN_DIM), jnp.bfloat16),
        scratch_shapes=[pltpu.VMEM((2, tq, tg), jnp.float32), pltpu.VMEM((2, tq, LANES), jnp.float32)],
        compiler_params=pltpu.CompilerParams(
            dimension_semantics=("arbitrary", "arbitrary", "arbitrary"), vmem_limit_bytes=VMEM_LIMIT_BYTES),
        name="attention",
    )(q, q, kt, kt, v, kmt, vm)


def _tail_kernel(x_ref, y_ref, o_ref, sa_ref, sb_ref, wfo_ref, wao_ref, wo_ref, g2_ref, wg_ref, wu_ref, wd_ref,
                 gf_ref, out_ref):
    bf = jnp.bfloat16
    f32 = jnp.float32
    yf = jnp.concatenate([y_ref[0, g] for g in range(N_GROUPS)], axis=1).astype(bf)
    ya = jnp.dot(yf, wfo_ref[...], preferred_element_type=f32)
    yb = jnp.dot(o_ref[0], wao_ref[...], preferred_element_type=f32)
    merged = sa_ref[0].astype(f32) * ya + sb_ref[0].astype(f32) * yb
    x1 = x_ref[0] + jnp.dot(merged.astype(bf), wo_ref[...], preferred_element_type=f32)
    h2 = _rms(x1, g2_ref[...]).astype(bf)
    x2 = x1
    for c0 in range(0, D_FF, FF_CHUNK):
        c1 = min(c0 + FF_CHUNK, D_FF)
        gate = jnp.dot(h2, wg_ref[:, c0:c1], preferred_element_type=f32)
        up = jnp.dot(h2, wu_ref[:, c0:c1], preferred_element_type=f32)
        act = (gate * jax.nn.sigmoid(gate) * up).astype(bf)
        x2 = x2 + jnp.dot(act, wd_ref[c0:c1, :], preferred_element_type=f32)
    out_ref[0] = _rms(x2, gf_ref[...])


def _tail(x, y, o, sa, sb, wfo, wao, wo, g2, wg, wu, wd, gf, tm):
    b, s, _ = x.shape
    tok = lambda bi, si: (bi, si, 0)
    const = lambda bi, si: (0, 0)

    def weight(shape):
        return pl.BlockSpec(shape, const, pipeline_mode=pl.Buffered(1))

    return pl.pallas_call(
        _tail_kernel,
        grid=(b, s // tm),
        in_specs=[
            pl.BlockSpec((1, tm, D_MODEL), tok),
            pl.BlockSpec((1, N_GROUPS, tm, GROUP_DIM), lambda bi, si: (bi, 0, si, 0)),
            pl.BlockSpec((1, tm, ATTN_DIM), tok),
            pl.BlockSpec((1, tm, D_MODEL), tok),
            pl.BlockSpec((1, tm, D_MODEL), tok),
            weight((FOURIER_DIM, D_MODEL)),
            weight((ATTN_DIM, D_MODEL)),
            weight((D_MODEL, D_MODEL)),
            pl.BlockSpec((1, D_MODEL), const),
            weight((D_MODEL, D_FF)),
            weight((D_MODEL, D_FF)),
            weight((D_FF, D_MODEL)),
            pl.BlockSpec((1, D_MODEL), const),
        ],
        out_specs=pl.BlockSpec((1, tm, D_MODEL), tok),
        out_shape=jax.ShapeDtypeStruct((b, s, D_MODEL), jnp.float32),
        compiler_params=pltpu.CompilerParams(
            dimension_semantics=("parallel", "parallel"), vmem_limit_bytes=VMEM_LIMIT_BYTES),
        name="tail",
    )(x, y, o, sa, sb, wfo, wao, wo, g2, wg, wu, wd, gf)


def _rope_tables(start, length):
    half = ROPE_DIM // 2
    lane = np.arange(LANES)
    inv = 1.0 / (ROPE_THETA ** (jnp.arange(0, ROPE_DIM, 2, dtype=jnp.float32) / ROPE_DIM))
    inv = inv[lane % half]
    sign = jnp.asarray(np.where((lane // half) % 2 == 0, -1.0, 1.0), jnp.float32)
    ang = jnp.arange(start, start + length, dtype=jnp.float32)[:, None] * inv[None, :]
    return jnp.cos(ang), jnp.sin(ang) * sign


def _prepare_weights(norm1_g, w_in, q_norm_g, kv_norm_g, w_uq, w_ukv, w_fourier_out, w_attn_out, w_o, norm2_g,
                     w_ffn_gate, w_ffn_up, w_ffn_down, final_norm_g):
    bf = jnp.bfloat16
    w = w_in[0]
    s_uf, s_cq, s_ckv, s_kr = FOURIER_DIM, FOURIER_DIM + Q_RANK, FOURIER_DIM + Q_RANK + KV_RANK, \
        FOURIER_DIM + Q_RANK + KV_RANK + ROPE_DIM
    win = jnp.concatenate(
        [w[:, :s_ckv], w[:, s_kr:], w[:, s_ckv:s_kr], jnp.zeros((D_MODEL, LANES - ROPE_DIM), w.dtype)], axis=1).astype(bf)
    wq = w_uq[0].reshape(Q_RANK, N_HEADS, QK_DIM)
    wuq = jnp.concatenate([wq[:, :, :NOPE_DIM].reshape(Q_RANK, -1), wq[:, :, NOPE_DIM:].reshape(Q_RANK, -1)], axis=1).astype(bf)
    wkv = w_ukv[0].reshape(KV_RANK, N_HEADS, NOPE_DIM + V_DIM)
    wkt = wkv[:, :, :NOPE_DIM].reshape(KV_RANK, -1).T.astype(bf)
    wv = wkv[:, :, NOPE_DIM:].reshape(KV_RANK, -1).astype(bf)
    row = lambda g: g.reshape(1, -1).astype(jnp.float32)
    return dict(
        g1=row(norm1_g[0]), win=win, qg=row(q_norm_g[0]), kvg=row(kv_norm_g[0]), wuq=wuq, wkt=wkt, wv=wv,
        wfo=w_fourier_out[0].astype(bf), wao=w_attn_out[0].astype(bf), wo=w_o[0].astype(bf), g2=row(norm2_g[0]),
        wg=w_ffn_gate[0].astype(bf), wu=w_ffn_up[0].astype(bf), wd=w_ffn_down[0].astype(bf), gf=row(final_norm_g))


def _project(p, x, cos2, sin2, tm):
    return _inproj(x, p["g1"], p["win"], p["qg"], p["kvg"], p["wuq"], p["wkt"], p["wv"], cos2, sin2, tm)


def _meta_projection(meta, p):
    meta_pad = jnp.zeros((1, LANES, D_MODEL), meta.dtype).at[0, :N_META].set(meta)
    uf_m, _, kt_m, v_m, _, _ = _project(p, meta_pad, *_rope_tables(0, LANES), LANES)
    return uf_m[0, :, :N_META], kt_m[:, :, 0], v_m


def _trunk(x, meta_proj, rope, p, tk, tq, attn_groups, tm_tail):
    s = x.shape[1]
    tg = s // attn_groups
    uf_m, kt_m, v_m = meta_proj
    cos2, sin2 = rope
    uf, q, kt, v, sa, sb = _project(p, x, cos2, sin2, tk)
    y = _fourier(uf, uf_m, *_fourier_tables(s))
    o = _attention(q, kt, v, kt_m, v_m, tq, tg)
    return _tail(x, y, o, sa, sb, p["wfo"], p["wao"], p["wo"], p["g2"], p["wg"], p["wu"], p["wd"], p["gf"], tm_tail)


def kernel(x_prompt, x_sample, meta_tokens, norm1_g, w_in, q_norm_g, kv_norm_g, w_uq, w_ukv, w_fourier_out,
           w_attn_out, w_o, norm2_g, w_ffn_gate, w_ffn_up, w_ffn_down, final_norm_g):
    p = _prepare_weights(norm1_g, w_in, q_norm_g, kv_norm_g, w_uq, w_ukv, w_fourier_out, w_attn_out, w_o, norm2_g,
                         w_ffn_gate, w_ffn_up, w_ffn_down, final_norm_g)
    rope = _rope_tables(N_META, max(x_prompt.shape[1], x_sample.shape[1]))
    meta_proj = _meta_projection(meta_tokens, p)
    cfg = dict(tk=512, tq=1024, attn_groups=4, tm_tail=512)
    y_prompt = _trunk(x_prompt, meta_proj, rope, p, **cfg)
    y_sample = _trunk(x_sample, meta_proj, rope, p, **cfg)
    return (y_prompt, y_sample)
```

```python
import functools
import math

import jax
import jax.numpy as jnp
import numpy as np
from jax import lax
from jax.experimental import pallas as pl
from jax.experimental.pallas import tpu as pltpu

D_MODEL = 1024
N_META = 16
N_GROUPS = 4
GROUP_DIM = 128
FOURIER_DIM = N_GROUPS * GROUP_DIM
N_HEADS = 8
NOPE_DIM = 128
ROPE_DIM = 64
QK_DIM = NOPE_DIM + ROPE_DIM
V_DIM = 128
Q_RANK = 512
KV_RANK = 256
ATTN_DIM = N_HEADS * V_DIM
D_FF = 2816
ROPE_THETA = 10000.0
NORM_EPS = 1e-6
ATTN_SCALE = QK_DIM ** -0.5
LOG2_E = math.log2(math.e)
DFT_RADIX = 16

LANES = 128
F32_SUBLANES = 8
BF16_SUBLANES = 16
VMEM_LIMIT_BYTES = 56 * 1024 * 1024
DFT_K1_BATCH = 4
FF_CHUNK = 1024

_C_UF = 0
_C_CQ = _C_UF + FOURIER_DIM
_C_CKV = _C_CQ + Q_RANK
_C_GA = _C_CKV + KV_RANK
_C_GB = _C_GA + D_MODEL
_C_KR = _C_GB + D_MODEL
IN_COLS = _C_KR + LANES


def _rms(x, g):
    return x * lax.rsqrt(jnp.mean(x * x, axis=-1, keepdims=True) + NORM_EPS) * g


def _rope128(x, cos2, sin2):
    lane = lax.broadcasted_iota(jnp.int32, x.shape, 1)
    first_half = (lane % ROPE_DIM) < (ROPE_DIM // 2)
    partner = jnp.where(first_half, pltpu.roll(x, LANES - ROPE_DIM // 2, 1), pltpu.roll(x, ROPE_DIM // 2, 1))
    return x * cos2 + partner * sin2


def _inproj_kernel(x_ref, g1_ref, win_ref, qg_ref, kvg_ref, wuq_ref, wkt_ref, wv_ref, cos_ref, sin_ref,
                   uf_ref, q_ref, kt_ref, v_ref, sa_ref, sb_ref):
    bf = jnp.bfloat16
    f32 = jnp.float32
    h = _rms(x_ref[0], g1_ref[...]).astype(bf)

    def proj(c0, width):
        return jnp.dot(h, win_ref[:, c0:c0 + width], preferred_element_type=f32)

    cos2 = cos_ref[...]
    sin2 = sin_ref[...]

    ckv_raw = proj(_C_CKV, KV_RANK)
    kr_raw = proj(_C_KR, LANES)
    cq_raw = proj(_C_CQ, Q_RANK)

    ckv = _rms(ckv_raw, kvg_ref[...])
    ckv_t = ckv.T.astype(bf)
    k_t = jnp.dot(wkt_ref[...], ckv_t, preferred_element_type=f32)
    vals = jnp.dot(ckv.astype(bf), wv_ref[...], preferred_element_type=f32)
    cq = _rms(cq_raw, qg_ref[...]).astype(bf)
    q = jnp.dot(cq, wuq_ref[...], preferred_element_type=f32) * (ATTN_SCALE * LOG2_E)

    kr_t = _rope128(kr_raw, cos2, sin2).T[:ROPE_DIM].astype(bf)
    for hd in range(N_HEADS):
        kt_ref[0, hd, 0, :NOPE_DIM, :] = k_t[hd * NOPE_DIM:(hd + 1) * NOPE_DIM].astype(bf)
        kt_ref[0, hd, 0, NOPE_DIM:, :] = kr_t
        v_ref[0, hd] = vals[:, hd * V_DIM:(hd + 1) * V_DIM].astype(bf)

    rope0 = N_HEADS * NOPE_DIM
    for pair in range(N_HEADS // 2):
        qr = _rope128(q[:, rope0 + pair * LANES:rope0 + (pair + 1) * LANES], cos2, sin2).astype(bf)
        for sub in range(2):
            hd = 2 * pair + sub
            q_ref[0, hd, :, :NOPE_DIM] = q[:, hd * NOPE_DIM:(hd + 1) * NOPE_DIM].astype(bf)
            q_ref[0, hd, :, NOPE_DIM:] = qr[:, sub * ROPE_DIM:(sub + 1) * ROPE_DIM]

    uf = proj(_C_UF, FOURIER_DIM)
    for g in range(N_GROUPS):
        uf_ref[0, g] = uf[:, g * GROUP_DIM:(g + 1) * GROUP_DIM]
    sa_ref[0] = jax.nn.sigmoid(proj(_C_GA, D_MODEL)).astype(bf)
    sb_ref[0] = jax.nn.sigmoid(proj(_C_GB, D_MODEL)).astype(bf)


def _const_spec(shape):
    return pl.BlockSpec(shape, lambda *_: (0,) * len(shape), pipeline_mode=pl.Buffered(1))


def _inproj(x, g1, win, qg, kvg, wuq, wkt, wv, cos2, sin2, tm):
    b, s, _ = x.shape
    tok = lambda bi, si: (bi, si, 0)
    head = lambda bi, si: (bi, 0, si, 0)
    bf = jnp.bfloat16
    return pl.pallas_call(
        _inproj_kernel,
        grid=(b, s // tm),
        in_specs=[
            pl.BlockSpec((1, tm, D_MODEL), tok),
            _const_spec((1, D_MODEL)),
            _const_spec((D_MODEL, IN_COLS)),
            _const_spec((1, Q_RANK)),
            _const_spec((1, KV_RANK)),
            _const_spec((Q_RANK, N_HEADS * QK_DIM)),
            _const_spec((N_HEADS * NOPE_DIM, KV_RANK)),
            _const_spec((KV_RANK, N_HEADS * V_DIM)),
            pl.BlockSpec((tm, LANES), lambda bi, si: (si, 0)),
            pl.BlockSpec((tm, LANES), lambda bi, si: (si, 0)),
        ],
        out_specs=[
            pl.BlockSpec((1, N_GROUPS, tm, GROUP_DIM), head),
            pl.BlockSpec((1, N_HEADS, tm, QK_DIM), head),
            pl.BlockSpec((1, N_HEADS, 1, QK_DIM, tm), lambda bi, si: (bi, 0, si, 0, 0)),
            pl.BlockSpec((1, N_HEADS, tm, V_DIM), head),
            pl.BlockSpec((1, tm, D_MODEL), tok),
            pl.BlockSpec((1, tm, D_MODEL), tok),
        ],
        out_shape=[
            jax.ShapeDtypeStruct((b, N_GROUPS, s, GROUP_DIM), jnp.float32),
            jax.ShapeDtypeStruct((b, N_HEADS, s, QK_DIM), bf),
            jax.ShapeDtypeStruct((b, N_HEADS, s // tm, QK_DIM, tm), bf),
            jax.ShapeDtypeStruct((b, N_HEADS, s, V_DIM), bf),
            jax.ShapeDtypeStruct((b, s, D_MODEL), bf),
            jax.ShapeDtypeStruct((b, s, D_MODEL), bf),
        ],
        compiler_params=pltpu.CompilerParams(
            dimension_semantics=("parallel", "parallel"), vmem_limit_bytes=VMEM_LIMIT_BYTES),
        name="inproj",
    )(x, g1, win, qg, kvg, wuq, wkt, wv, cos2, sin2)


def _cneg(v):
    return None if v is None else -v


def _cadd(a, b):
    if a is None:
        return b
    if b is None:
        return a
    return a + b


def _csub(a, b):
    if b is None:
        return a
    if a is None:
        return -b
    return a - b


def _fft(xs):
    n = len(xs)
    if n == 1:
        return xs
    ev = _fft(xs[0::2])
    od = _fft(xs[1::2])
    out = [None] * n
    for k in range(n // 2):
        re, im = od[k]
        if k == 0:
            tr, ti = re, im
        elif 4 * k == n:
            tr, ti = im, _cneg(re)
        else:
            c = math.cos(2 * math.pi * k / n)
            s = math.sin(2 * math.pi * k / n)
            tr = _cadd(None if re is None else re * c, None if im is None else im * s)
            ti = _csub(None if im is None else im * c, None if re is None else re * s)
        out[k] = (_cadd(ev[k][0], tr), _cadd(ev[k][1], ti))
        out[k + n // 2] = (_csub(ev[k][0], tr), _csub(ev[k][1], ti))
    return out


def _dft16_real(rows):
    out = _fft([(r, None) for r in rows])
    zero = jnp.zeros_like(rows[0])
    out = [(zero if re is None else re, zero if im is None else im) for re, im in out]
    for k in range(DFT_RADIX // 2 + 1, DFT_RADIX):
        re, im = out[DFT_RADIX - k]
        out[k] = (re, -im)
    return out


def _fourier_kernel(n2, x0_ref, xm0_ref, xn_ref, xmn_ref, twc_ref, tws_ref, dmat_ref, cs_ref, y_ref,
                    xrun_ref, z_ref, z0_ref):
    m = n2 - 1
    bf = jnp.bfloat16
    zw = 2 * GROUP_DIM
    slot = pl.program_id(1) % 2

    rc = BF16_SUBLANES

    def run_start(j1):
        return j1 * n2 - N_META

    def vpu_head(x_ref, xm_ref, dst):
        xrun_ref[0:N_META, :] = xm_ref[0]
        xrun_ref[N_META:N_META + m, :] = x_ref[0, 0, 0:m, :]
        dc = _dft16_real([xrun_ref[0:1, :]] + [x_ref[0, 0, run_start(j1):run_start(j1) + 1, :]
                                              for j1 in range(1, DFT_RADIX)])
        for k1 in range(DFT_RADIX):
            z0_ref[dst, 0:1, k1 * zw:k1 * zw + GROUP_DIM] = dc[k1][0]
            z0_ref[dst, 0:1, k1 * zw + GROUP_DIM:(k1 + 1) * zw] = dc[k1][1]

    def vpu_chunk(x_ref, dst, r0):
        rows = [xrun_ref[r0 + 1:r0 + 1 + rc, :]]
        rows += [x_ref[0, 0, r0 + run_start(j1) + 1:r0 + run_start(j1) + 1 + rc, :] for j1 in range(1, DFT_RADIX)]
        a = _dft16_real(rows)
        for k1 in range(DFT_RADIX):
            ar, ai = a[k1]
            if k1 == 0:
                zr, zi = ar, ai
            else:
                tc = twc_ref[k1, r0:r0 + rc, :]
                ts = tws_ref[k1, r0:r0 + rc, :]
                zr = tc * ar + ts * ai
                zi = tc * ai - ts * ar
            z_ref[dst, r0:r0 + rc, k1 * zw:k1 * zw + GROUP_DIM] = zr.astype(bf)
            z_ref[dst, r0:r0 + rc, k1 * zw + GROUP_DIM:(k1 + 1) * zw] = zi.astype(bf)

    scale = 1.0 / math.sqrt(GROUP_DIM * DFT_RADIX * n2)

    def mxu_batch(k0):
        cols = slice(k0 * zw, (k0 + DFT_K1_BATCH) * zw)
        z = z_ref[slot, :, cols]
        swapped = jnp.concatenate(
            [blk for i in range(DFT_K1_BATCH)
             for blk in (z[:, i * zw + GROUP_DIM:(i + 1) * zw], -z[:, i * zw:i * zw + GROUP_DIM])], axis=1)
        rhs = jnp.concatenate([z, swapped], axis=0)
        p = jnp.dot(dmat_ref[...], rhs, preferred_element_type=jnp.float32)
        p = (p + z0_ref[slot, 0:1, cols]).astype(bf)
        stacked = jnp.concatenate([p[:, i * zw:(i + 1) * zw] for i in range(DFT_K1_BATCH)], axis=0)
        y = jnp.dot(stacked, cs_ref[...], preferred_element_type=jnp.float32) * scale
        for i in range(DFT_K1_BATCH):
            y_ref[0, 0, pl.ds(k0 + i, m, stride=DFT_RADIX), :] = y[i * m:(i + 1) * m]

    @pl.when((pl.program_id(0) == 0) & (pl.program_id(1) == 0))
    def _():
        vpu_head(x0_ref, xm0_ref, 0)
        for r0 in range(0, m, rc):
            vpu_chunk(x0_ref, 0, r0)

    vpu_head(xn_ref, xmn_ref, 1 - slot)
    for r0 in range(0, m, rc):
        vpu_chunk(xn_ref, 1 - slot, r0)
    for k0 in range(0, DFT_RADIX, DFT_K1_BATCH):
        mxu_batch(k0)


def _fourier(uf, uf_meta, twc, tws, dmat, cs):
    b, _, s, _ = uf.shape
    n2 = (s + N_META) // DFT_RADIX
    m = n2 - 1
    assert N_GROUPS % 2 == 0

    def next_step(bi, g):
        wrap = g + 1 == N_GROUPS
        return jnp.where(wrap, jnp.where(bi + 1 == b, 0, bi + 1), bi), jnp.where(wrap, 0, g + 1)

    def next_x(bi, g):
        bn, gn = next_step(bi, g)
        return bn, gn, 0, 0

    def next_meta(bi, g):
        return next_step(bi, g)[1], 0, 0

    return pl.pallas_call(
        functools.partial(_fourier_kernel, n2),
        grid=(b, N_GROUPS),
        in_specs=[
            _const_spec((1, 1, s, GROUP_DIM)),
            _const_spec((1, N_META, GROUP_DIM)),
            pl.BlockSpec((1, 1, s, GROUP_DIM), next_x),
            pl.BlockSpec((1, N_META, GROUP_DIM), next_meta),
            _const_spec((DFT_RADIX, m, GROUP_DIM)),
            _const_spec((DFT_RADIX, m, GROUP_DIM)),
            _const_spec((m, 2 * m)),
            _const_spec((2 * GROUP_DIM, GROUP_DIM)),
        ],
        out_specs=pl.BlockSpec((1, 1, s, GROUP_DIM), lambda bi, g: (bi, g, 0, 0)),
        out_shape=jax.ShapeDtypeStruct((b, N_GROUPS, s, GROUP_DIM), jnp.float32),
        scratch_shapes=[
            pltpu.VMEM((N_META + m, GROUP_DIM), jnp.float32),
            pltpu.VMEM((2, m, DFT_RADIX * 2 * GROUP_DIM), jnp.bfloat16),
            pltpu.VMEM((2, F32_SUBLANES, DFT_RADIX * 2 * GROUP_DIM), jnp.float32),
        ],
        compiler_params=pltpu.CompilerParams(
            dimension_semantics=("arbitrary", "arbitrary"), vmem_limit_bytes=VMEM_LIMIT_BYTES),
        name="fourier",
    )(uf, uf_meta, uf, uf_meta, twc, tws, dmat, cs)


def _fourier_tables(s):
    n2 = (s + N_META) // DFT_RADIX
    length = DFT_RADIX * n2
    j2 = np.arange(1, n2, dtype=np.int64)
    k1 = np.arange(DFT_RADIX, dtype=np.int64)
    phi = 2.0 * np.pi * ((k1[:, None] * j2[None, :]) % length) / length
    twc = jnp.broadcast_to(jnp.asarray(np.cos(phi), jnp.float32)[:, :, None], (DFT_RADIX, n2 - 1, GROUP_DIM))
    tws = jnp.broadcast_to(jnp.asarray(np.sin(phi), jnp.float32)[:, :, None], (DFT_RADIX, n2 - 1, GROUP_DIM))
    theta = 2.0 * np.pi * ((j2[:, None] * j2[None, :]) % n2) / n2
    dmat = jnp.asarray(np.concatenate([np.cos(theta), np.sin(theta)], axis=1), jnp.float32)
    c = np.arange(GROUP_DIM, dtype=np.int64)
    psi = 2.0 * np.pi * ((c[:, None] * c[None, :]) % GROUP_DIM) / GROUP_DIM
    cs = jnp.asarray(np.concatenate([np.cos(psi), np.sin(psi)], axis=0), jnp.float32)
    return twc, tws, dmat.astype(jnp.bfloat16), cs.astype(jnp.bfloat16)


def _attn_kernel(q_ref, qn_ref, kt_ref, ktn_ref, v_ref, kmt_ref, vm_ref, o_ref, s_ref, pm_ref):
    bf = jnp.bfloat16
    f32 = jnp.float32
    tq = o_ref.shape[1]
    n, _, tk = kt_ref.shape[2:]
    group = s_ref.shape[2] // tk
    n_groups = n // group

    def produce(slot, qt, key_chunk):
        pmax = None
        for j in range(group):
            s = jnp.dot(qt, key_chunk(j), preferred_element_type=f32)
            s_ref[slot, :, j * tk:(j + 1) * tk] = s
            for l in range(tk // LANES):
                blk = s[:, l * LANES:(l + 1) * LANES]
                pmax = blk if pmax is None else jnp.maximum(pmax, blk)
        pm_ref[slot] = pmax

    def with_ones(vals):
        return jnp.concatenate([vals, jnp.ones_like(vals)], axis=1)

    def absorb(slot, g, m_i, acc):
        pmax = pm_ref[slot]
        if g == 0:
            lane = lax.broadcasted_iota(jnp.int32, (tq, LANES), 1)
            s_meta = jnp.dot(q_ref[0, 0], kmt_ref[0, 0], preferred_element_type=f32)
            s_meta = jnp.where(lane < N_META, s_meta, -jnp.inf)
            pmax = jnp.maximum(pmax, s_meta)
        m_new = jnp.max(pmax, axis=1, keepdims=True)
        if g > 0:
            m_new = jnp.maximum(m_i, m_new)
        p = jnp.exp2((s_ref[slot] - m_new).astype(bf))
        pv = jnp.dot(p, with_ones(v_ref[0, 0, g * group * tk:(g + 1) * group * tk, :]), preferred_element_type=f32)
        if g == 0:
            p_meta = jnp.exp2((s_meta - m_new).astype(bf))
            return m_new, pv + jnp.dot(p_meta, with_ones(vm_ref[0, 0]), preferred_element_type=f32)
        return m_new, jnp.exp2(m_i - m_new) * acc + pv

    first_step = (pl.program_id(0) == 0) & (pl.program_id(1) == 0) & (pl.program_id(2) == 0)

    @pl.when(first_step)
    def _():
        produce(0, q_ref[0, 0], lambda j: kt_ref[0, 0, j])

    m = acc = None
    for g in range(n_groups):
        if g + 1 < n_groups:
            produce((g + 1) % 2, q_ref[0, 0], lambda j, g=g: kt_ref[0, 0, (g + 1) * group + j])
        else:
            produce(0, qn_ref[0, 0], lambda j: ktn_ref[0, 0, j])
        m, acc = absorb(g % 2, g, m, acc)
    o_ref[0] = (acc[:, :V_DIM] / acc[:, V_DIM:]).astype(bf)


def _attention(q, kt, v, kmt, vm, tq, tg):
    b, nh, s, _ = q.shape
    n, _, tk = kt.shape[2:]
    nq = s // tq
    assert n * tk == s and tg % tk == 0 and (s // tg) % 2 == 0

    def next_step(bi, h, qi):
        wrap_q = qi + 1 == nq
        wrap_h = wrap_q & (h + 1 == nh)
        bn = jnp.where(wrap_h, jnp.where(bi + 1 == b, 0, bi + 1), bi)
        hn = jnp.where(wrap_q, jnp.where(h + 1 == nh, 0, h + 1), h)
        return bn, hn, jnp.where(wrap_q, 0, qi + 1)

    def next_q(bi, h, qi):
        bn, hn, qn = next_step(bi, h, qi)
        return bn, hn, qn, 0

    def next_keys(bi, h, qi):
        bn, hn, _ = next_step(bi, h, qi)
        return bn, hn, 0, 0, 0

    return pl.pallas_call(
        _attn_kernel,
        grid=(b, nh, nq),
        in_specs=[
            pl.BlockSpec((1, 1, tq, QK_DIM), lambda bi, h, qi: (bi, h, qi, 0)),
            pl.BlockSpec((1, 1, tq, QK_DIM), next_q),
            pl.BlockSpec((1, 1, n, QK_DIM, tk), lambda bi, h, qi: (bi, h, 0, 0, 0)),
            pl.BlockSpec((1, 1, tg // tk, QK_DIM, tk), next_keys),
            pl.BlockSpec((1, 1, s, V_DIM), lambda bi, h, qi: (bi, h, 0, 0)),
            pl.BlockSpec((1, 1, QK_DIM, LANES), lambda bi, h, qi: (0, h, 0, 0)),
            pl.BlockSpec((1, 1, LANES, V_DIM), lambda bi, h, qi: (0, h, 0, 0)),
        ],
        out_specs=pl.BlockSpec((1, tq, V_DIM), lambda bi, h, qi: (bi, qi, h)),
        out_shape=jax.ShapeDtypeStruct((b, s, ATTN_DIM), jnp.bfloat16),
        scratch_shapes=[pltpu.VMEM((2, tq, tg), jnp.float32), pltpu.VMEM((2, tq, LANES), jnp.float32)],
        compiler_params=pltpu.CompilerParams(
            dimension_semantics=("arbitrary", "arbitrary", "arbitrary"), vmem_limit_bytes=VMEM_LIMIT_BYTES),
        name="attention",
    )(q, q, kt, kt, v, kmt, vm)


def _tail_kernel(x_ref, y_ref, o_ref, sa_ref, sb_ref, wfo_ref, wao_ref, wo_ref, g2_ref, wg_ref, wu_ref, wd_ref,
                 gf_ref, out_ref):
    bf = jnp.bfloat16
    f32 = jnp.float32
    yf = jnp.concatenate([y_ref[0, g] for g in range(N_GROUPS)], axis=1).astype(bf)
    ya = jnp.dot(yf, wfo_ref[...], preferred_element_type=f32)
    yb = jnp.dot(o_ref[0], wao_ref[...], preferred_element_type=f32)
    merged = sa_ref[0].astype(f32) * ya + sb_ref[0].astype(f32) * yb
    x1 = x_ref[0] + jnp.dot(merged.astype(bf), wo_ref[...], preferred_element_type=f32)
    h2 = _rms(x1, g2_ref[...]).astype(bf)
    x2 = x1
    for c0 in range(0, D_FF, FF_CHUNK):
        c1 = min(c0 + FF_CHUNK, D_FF)
        gate = jnp.dot(h2, wg_ref[:, c0:c1], preferred_element_type=f32)
        up = jnp.dot(h2, wu_ref[:, c0:c1], preferred_element_type=f32)
        act = (gate * jax.nn.sigmoid(gate) * up).astype(bf)
        x2 = x2 + jnp.dot(act, wd_ref[c0:c1, :], preferred_element_type=f32)
    out_ref[0] = _rms(x2, gf_ref[...])


def _tail(x, y, o, sa, sb, wfo, wao, wo, g2, wg, wu, wd, gf, tm):
    b, s, _ = x.shape
    tok = lambda bi, si: (bi, si, 0)
    const = lambda bi, si: (0, 0)

    def weight(shape):
        return pl.BlockSpec(shape, const, pipeline_mode=pl.Buffered(1))

    return pl.pallas_call(
        _tail_kernel,
        grid=(b, s // tm),
        in_specs=[
            pl.BlockSpec((1, tm, D_MODEL), tok),
            pl.BlockSpec((1, N_GROUPS, tm, GROUP_DIM), lambda bi, si: (bi, 0, si, 0)),
            pl.BlockSpec((1, tm, ATTN_DIM), tok),
            pl.BlockSpec((1, tm, D_MODEL), tok),
            pl.BlockSpec((1, tm, D_MODEL), tok),
            weight((FOURIER_DIM, D_MODEL)),
            weight((ATTN_DIM, D_MODEL)),
            weight((D_MODEL, D_MODEL)),
            pl.BlockSpec((1, D_MODEL), const),
            weight((D_MODEL, D_FF)),
            weight((D_MODEL, D_FF)),
            weight((D_FF, D_MODEL)),
            pl.BlockSpec((1, D_MODEL), const),
        ],
        out_specs=pl.BlockSpec((1, tm, D_MODEL), tok),
        out_shape=jax.ShapeDtypeStruct((b, s, D_MODEL), jnp.float32),
        compiler_params=pltpu.CompilerParams(
            dimension_semantics=("parallel", "parallel"), vmem_limit_bytes=VMEM_LIMIT_BYTES),
        name="tail",
    )(x, y, o, sa, sb, wfo, wao, wo, g2, wg, wu, wd, gf)


def _rope_tables(start, length):
    half = ROPE_DIM // 2
    lane = np.arange(LANES)
    inv = 1.0 / (ROPE_THETA ** (jnp.arange(0, ROPE_DIM, 2, dtype=jnp.float32) / ROPE_DIM))
    inv = inv[lane % half]
    sign = jnp.asarray(np.where((lane // half) % 2 == 0, -1.0, 1.0), jnp.float32)
    ang = jnp.arange(start, start + length, dtype=jnp.float32)[:, None] * inv[None, :]
    return jnp.cos(ang), jnp.sin(ang) * sign


def _prepare_weights(norm1_g, w_in, q_norm_g, kv_norm_g, w_uq, w_ukv, w_fourier_out, w_attn_out, w_o, norm2_g,
                     w_ffn_gate, w_ffn_up, w_ffn_down, final_norm_g):
    bf = jnp.bfloat16
    w = w_in[0]
    s_uf, s_cq, s_ckv, s_kr = FOURIER_DIM, FOURIER_DIM + Q_RANK, FOURIER_DIM + Q_RANK + KV_RANK, \
        FOURIER_DIM + Q_RANK + KV_RANK + ROPE_DIM
    win = jnp.concatenate(
        [w[:, :s_ckv], w[:, s_kr:], w[:, s_ckv:s_kr], jnp.zeros((D_MODEL, LANES - ROPE_DIM), w.dtype)], axis=1).astype(bf)
    wq = w_uq[0].reshape(Q_RANK, N_HEADS, QK_DIM)
    wuq = jnp.concatenate([wq[:, :, :NOPE_DIM].reshape(Q_RANK, -1), wq[:, :, NOPE_DIM:].reshape(Q_RANK, -1)], axis=1).astype(bf)
    wkv = w_ukv[0].reshape(KV_RANK, N_HEADS, NOPE_DIM + V_DIM)
    wkt = wkv[:, :, :NOPE_DIM].reshape(KV_RANK, -1).T.astype(bf)
    wv = wkv[:, :, NOPE_DIM:].reshape(KV_RANK, -1).astype(bf)
    row = lambda g: g.reshape(1, -1)
    return dict(
        g1=row(norm1_g[0]), win=win, qg=row(q_norm_g[0]), kvg=row(kv_norm_g[0]), wuq=wuq, wkt=wkt, wv=wv,
        wfo=w_fourier_out[0].astype(bf), wao=w_attn_out[0].astype(bf), wo=w_o[0].astype(bf), g2=row(norm2_g[0]),
        wg=w_ffn_gate[0].astype(bf), wu=w_ffn_up[0].astype(bf), wd=w_ffn_down[0].astype(bf), gf=row(final_norm_g))


def _project(p, x, cos2, sin2, tm):
    return _inproj(x, p["g1"], p["win"], p["qg"], p["kvg"], p["wuq"], p["wkt"], p["wv"], cos2, sin2, tm)


def _meta_projection(meta, p):
    meta_pad = jnp.zeros((1, LANES, D_MODEL), meta.dtype).at[0, :N_META].set(meta)
    uf_m, _, kt_m, v_m, _, _ = _project(p, meta_pad, *_rope_tables(0, LANES), LANES)
    return uf_m[0, :, :N_META], kt_m[:, :, 0], v_m


def _trunk(x, meta_proj, rope, p, tk, tq, attn_groups, tm_tail):
    s = x.shape[1]
    tg = s // attn_groups
    uf_m, kt_m, v_m = meta_proj
    cos2, sin2 = rope
    uf, q, kt, v, sa, sb = _project(p, x, cos2, sin2, tk)
    y = _fourier(uf, uf_m, *_fourier_tables(s))
    o = _attention(q, kt, v, kt_m, v_m, tq, tg)
    return _tail(x, y, o, sa, sb, p["wfo"], p["wao"], p["wo"], p["g2"], p["wg"], p["wu"], p["wd"], p["gf"], tm_tail)


def kernel(x_prompt, x_sample, meta_tokens, norm1_g, w_in, q_norm_g, kv_norm_g, w_uq, w_ukv, w_fourier_out,
           w_attn_out, w_o, norm2_g, w_ffn_gate, w_ffn_up, w_ffn_down, final_norm_g):
    p = _prepare_weights(norm1_g, w_in, q_norm_g, kv_norm_g, w_uq, w_ukv, w_fourier_out, w_attn_out, w_o, norm2_g,
                         w_ffn_gate, w_ffn_up, w_ffn_down, final_norm_g)
    rope = _rope_tables(N_META, max(x_prompt.shape[1], x_sample.shape[1]))
    meta_proj = _meta_projection(meta_tokens, p)
    cfg = dict(tk=512, tq=1024, attn_groups=4, tm_tail=512)
    y_prompt = _trunk(x_prompt, meta_proj, rope, p, **cfg)
    y_sample = _trunk(x_sample, meta_proj, rope, p, **cfg)
    return (y_prompt, y_sample)
```

```python
import functools
import math

import jax
import jax.numpy as jnp
import numpy as np
from jax import lax
from jax.experimental import pallas as pl
from jax.experimental.pallas import tpu as pltpu

D_MODEL = 1024
N_META = 16
N_GROUPS = 4
GROUP_DIM = 128
FOURIER_DIM = N_GROUPS * GROUP_DIM
N_HEADS = 8
NOPE_DIM = 128
ROPE_DIM = 64
QK_DIM = NOPE_DIM + ROPE_DIM
V_DIM = 128
Q_RANK = 512
KV_RANK = 256
ATTN_DIM = N_HEADS * V_DIM
D_FF = 2816
ROPE_THETA = 10000.0
NORM_EPS = 1e-6
ATTN_SCALE = QK_DIM ** -0.5
LOG2_E = math.log2(math.e)
DFT_RADIX = 16

LANES = 128
F32_SUBLANES = 8
BF16_SUBLANES = 16
VMEM_LIMIT_BYTES = 56 * 1024 * 1024
DFT_K1_BATCH = 4
FF_CHUNK = 1024

_C_UF = 0
_C_CQ = _C_UF + FOURIER_DIM
_C_CKV = _C_CQ + Q_RANK
_C_GA = _C_CKV + KV_RANK
_C_GB = _C_GA + D_MODEL
_C_KR = _C_GB + D_MODEL
IN_COLS = _C_KR + LANES


def _rms(x, g):
    return x * lax.rsqrt(jnp.mean(x * x, axis=-1, keepdims=True) + NORM_EPS) * g


def _rope128(x, cos2, sin2):
    lane = lax.broadcasted_iota(jnp.int32, x.shape, 1)
    first_half = (lane % ROPE_DIM) < (ROPE_DIM // 2)
    partner = jnp.where(first_half, pltpu.roll(x, LANES - ROPE_DIM // 2, 1), pltpu.roll(x, ROPE_DIM // 2, 1))
    return x * cos2 + partner * sin2


def _inproj_kernel(x_ref, g1_ref, win_ref, qg_ref, kvg_ref, wuq_ref, wkt_ref, wv_ref, cos_ref, sin_ref,
                   uf_ref, q_ref, kt_ref, v_ref, sa_ref, sb_ref):
    bf = jnp.bfloat16
    f32 = jnp.float32
    h = _rms(x_ref[0], g1_ref[...]).astype(bf)

    def proj(c0, width):
        return jnp.dot(h, win_ref[:, c0:c0 + width], preferred_element_type=f32)

    cos2 = cos_ref[...]
    sin2 = sin_ref[...]

    ckv_raw = proj(_C_CKV, KV_RANK)
    kr_raw = proj(_C_KR, LANES)
    cq_raw = proj(_C_CQ, Q_RANK)

    ckv = _rms(ckv_raw, kvg_ref[...])
    ckv_t = ckv.T.astype(bf)
    k_t = jnp.dot(wkt_ref[...], ckv_t, preferred_element_type=f32)
    vals = jnp.dot(ckv.astype(bf), wv_ref[...], preferred_element_type=f32)
    cq = _rms(cq_raw, qg_ref[...]).astype(bf)
    q = jnp.dot(cq, wuq_ref[...], preferred_element_type=f32) * (ATTN_SCALE * LOG2_E)

    kr_t = _rope128(kr_raw, cos2, sin2).T[:ROPE_DIM].astype(bf)
    for hd in range(N_HEADS):
        kt_ref[0, hd, 0, :NOPE_DIM, :] = k_t[hd * NOPE_DIM:(hd + 1) * NOPE_DIM].astype(bf)
        kt_ref[0, hd, 0, NOPE_DIM:, :] = kr_t
        v_ref[0, hd] = vals[:, hd * V_DIM:(hd + 1) * V_DIM].astype(bf)

    rope0 = N_HEADS * NOPE_DIM
    for pair in range(N_HEADS // 2):
        qr = _rope128(q[:, rope0 + pair * LANES:rope0 + (pair + 1) * LANES], cos2, sin2).astype(bf)
        for sub in range(2):
            hd = 2 * pair + sub
            q_ref[0, hd, :, :NOPE_DIM] = q[:, hd * NOPE_DIM:(hd + 1) * NOPE_DIM].astype(bf)
            q_ref[0, hd, :, NOPE_DIM:] = qr[:, sub * ROPE_DIM:(sub + 1) * ROPE_DIM]

    uf = proj(_C_UF, FOURIER_DIM)
    for g in range(N_GROUPS):
        uf_ref[0, g] = uf[:, g * GROUP_DIM:(g + 1) * GROUP_DIM]
    sa_ref[0] = jax.nn.sigmoid(proj(_C_GA, D_MODEL)).astype(bf)
    sb_ref[0] = jax.nn.sigmoid(proj(_C_GB, D_MODEL)).astype(bf)


def _const_spec(shape):
    return pl.BlockSpec(shape, lambda *_: (0,) * len(shape), pipeline_mode=pl.Buffered(1))


def _inproj(x, g1, win, qg, kvg, wuq, wkt, wv, cos2, sin2, tm):
    b, s, _ = x.shape
    tok = lambda bi, si: (bi, si, 0)
    head = lambda bi, si: (bi, 0, si, 0)
    bf = jnp.bfloat16
    return pl.pallas_call(
        _inproj_kernel,
        grid=(b, s // tm),
        in_specs=[
            pl.BlockSpec((1, tm, D_MODEL), tok),
            _const_spec((1, D_MODEL)),
            _const_spec((D_MODEL, IN_COLS)),
            _const_spec((1, Q_RANK)),
            _const_spec((1, KV_RANK)),
            _const_spec((Q_RANK, N_HEADS * QK_DIM)),
            _const_spec((N_HEADS * NOPE_DIM, KV_RANK)),
            _const_spec((KV_RANK, N_HEADS * V_DIM)),
            pl.BlockSpec((tm, LANES), lambda bi, si: (si, 0)),
            pl.BlockSpec((tm, LANES), lambda bi, si: (si, 0)),
        ],
        out_specs=[
            pl.BlockSpec((1, N_GROUPS, tm, GROUP_DIM), head),
            pl.BlockSpec((1, N_HEADS, tm, QK_DIM), head),
            pl.BlockSpec((1, N_HEADS, 1, QK_DIM, tm), lambda bi, si: (bi, 0, si, 0, 0)),
            pl.BlockSpec((1, N_HEADS, tm, V_DIM), head),
            pl.BlockSpec((1, tm, D_MODEL), tok),
            pl.BlockSpec((1, tm, D_MODEL), tok),
        ],
        out_shape=[
            jax.ShapeDtypeStruct((b, N_GROUPS, s, GROUP_DIM), jnp.float32),
            jax.ShapeDtypeStruct((b, N_HEADS, s, QK_DIM), bf),
            jax.ShapeDtypeStruct((b, N_HEADS, s // tm, QK_DIM, tm), bf),
            jax.ShapeDtypeStruct((b, N_HEADS, s, V_DIM), bf),
            jax.ShapeDtypeStruct((b, s, D_MODEL), bf),
            jax.ShapeDtypeStruct((b, s, D_MODEL), bf),
        ],
        compiler_params=pltpu.CompilerParams(
            dimension_semantics=("parallel", "parallel"), vmem_limit_bytes=VMEM_LIMIT_BYTES),
        name="inproj",
    )(x, g1, win, qg, kvg, wuq, wkt, wv, cos2, sin2)


def _cneg(v):
    return None if v is None else -v


def _cadd(a, b):
    if a is None:
        return b
    if b is None:
        return a
    return a + b


def _csub(a, b):
    if b is None:
        return a
    if a is None:
        return -b
    return a - b


def _fft(xs):
    n = len(xs)
    if n == 1:
        return xs
    ev = _fft(xs[0::2])
    od = _fft(xs[1::2])
    out = [None] * n
    for k in range(n // 2):
        re, im = od[k]
        if k == 0:
            tr, ti = re, im
        elif 4 * k == n:
            tr, ti = im, _cneg(re)
        else:
            c = math.cos(2 * math.pi * k / n)
            s = math.sin(2 * math.pi * k / n)
            tr = _cadd(None if re is None else re * c, None if im is None else im * s)
            ti = _csub(None if im is None else im * c, None if re is None else re * s)
        out[k] = (_cadd(ev[k][0], tr), _cadd(ev[k][1], ti))
        out[k + n // 2] = (_csub(ev[k][0], tr), _csub(ev[k][1], ti))
    return out


def _dft16_real(rows):
    out = _fft([(r, None) for r in rows])
    zero = jnp.zeros_like(rows[0])
    out = [(zero if re is None else re, zero if im is None else im) for re, im in out]
    for k in range(DFT_RADIX // 2 + 1, DFT_RADIX):
        re, im = out[DFT_RADIX - k]
        out[k] = (re, -im)
    return out


def _fourier_kernel(n2, x0_ref, xm0_ref, xn_ref, xmn_ref, twc_ref, tws_ref, dmat_ref, cs_ref, y_ref,
                    xrun_ref, z_ref, z0_ref):
    m = n2 - 1
    bf = jnp.bfloat16
    zw = 2 * GROUP_DIM
    slot = pl.program_id(1) % 2

    rc = BF16_SUBLANES

    def run_start(j1):
        return j1 * n2 - N_META

    def vpu_head(x_ref, xm_ref, dst):
        xrun_ref[0:N_META, :] = xm_ref[0]
        xrun_ref[N_META:N_META + m, :] = x_ref[0, 0, 0:m, :]
        dc = _dft16_real([xrun_ref[0:1, :]] + [x_ref[0, 0, run_start(j1):run_start(j1) + 1, :]
                                              for j1 in range(1, DFT_RADIX)])
        for k1 in range(DFT_RADIX):
            z0_ref[dst, 0:1, k1 * zw:k1 * zw + GROUP_DIM] = dc[k1][0]
            z0_ref[dst, 0:1, k1 * zw + GROUP_DIM:(k1 + 1) * zw] = dc[k1][1]

    def vpu_chunk(x_ref, dst, r0):
        rows = [xrun_ref[r0 + 1:r0 + 1 + rc, :]]
        rows += [x_ref[0, 0, r0 + run_start(j1) + 1:r0 + run_start(j1) + 1 + rc, :] for j1 in range(1, DFT_RADIX)]
        a = _dft16_real(rows)
        for k1 in range(DFT_RADIX):
            ar, ai = a[k1]
            if k1 == 0:
                zr, zi = ar, ai
            else:
                tc = twc_ref[k1, r0:r0 + rc, :]
                ts = tws_ref[k1, r0:r0 + rc, :]
                zr = tc * ar + ts * ai
                zi = tc * ai - ts * ar
            z_ref[dst, r0:r0 + rc, k1 * zw:k1 * zw + GROUP_DIM] = zr.astype(bf)
            z_ref[dst, r0:r0 + rc, k1 * zw + GROUP_DIM:(k1 + 1) * zw] = zi.astype(bf)

    scale = 1.0 / math.sqrt(GROUP_DIM * DFT_RADIX * n2)

    def mxu_batch(k0):
        cols = slice(k0 * zw, (k0 + DFT_K1_BATCH) * zw)
        z = z_ref[slot, :, cols]
        swapped = jnp.concatenate(
            [blk for i in range(DFT_K1_BATCH)
             for blk in (z[:, i * zw + GROUP_DIM:(i + 1) * zw], -z[:, i * zw:i * zw + GROUP_DIM])], axis=1)
        rhs = jnp.concatenate([z, swapped], axis=0)
        p = jnp.dot(dmat_ref[...], rhs, preferred_element_type=jnp.float32)
        p = (p + z0_ref[slot, 0:1, cols]).astype(bf)
        stacked = jnp.concatenate([p[:, i * zw:(i + 1) * zw] for i in range(DFT_K1_BATCH)], axis=0)
        y = jnp.dot(stacked, cs_ref[...], preferred_element_type=jnp.float32) * scale
        for i in range(DFT_K1_BATCH):
            y_ref[0, 0, pl.ds(k0 + i, m, stride=DFT_RADIX), :] = y[i * m:(i + 1) * m]

    @pl.when((pl.program_id(0) == 0) & (pl.program_id(1) == 0))
    def _():
        vpu_head(x0_ref, xm0_ref, 0)
        for r0 in range(0, m, rc):
            vpu_chunk(x0_ref, 0, r0)

    vpu_head(xn_ref, xmn_ref, 1 - slot)
    for r0 in range(0, m, rc):
        vpu_chunk(xn_ref, 1 - slot, r0)
    for k0 in range(0, DFT_RADIX, DFT_K1_BATCH):
        mxu_batch(k0)


def _fourier(uf, uf_meta, twc, tws, dmat, cs):
    b, _, s, _ = uf.shape
    n2 = (s + N_META) // DFT_RADIX
    m = n2 - 1
    assert N_GROUPS % 2 == 0

    def next_step(bi, g):
        wrap = g + 1 == N_GROUPS
        return jnp.where(wrap, jnp.where(bi + 1 == b, 0, bi + 1), bi), jnp.where(wrap, 0, g + 1)

    def next_x(bi, g):
        bn, gn = next_step(bi, g)
        return bn, gn, 0, 0

    def next_meta(bi, g):
        return next_step(bi, g)[1], 0, 0

    return pl.pallas_call(
        functools.partial(_fourier_kernel, n2),
        grid=(b, N_GROUPS),
        in_specs=[
            _const_spec((1, 1, s, GROUP_DIM)),
            _const_spec((1, N_META, GROUP_DIM)),
            pl.BlockSpec((1, 1, s, GROUP_DIM), next_x),
            pl.BlockSpec((1, N_META, GROUP_DIM), next_meta),
            _const_spec((DFT_RADIX, m, GROUP_DIM)),
            _const_spec((DFT_RADIX, m, GROUP_DIM)),
            _const_spec((m, 2 * m)),
            _const_spec((2 * GROUP_DIM, GROUP_DIM)),
        ],
        out_specs=pl.BlockSpec((1, 1, s, GROUP_DIM), lambda bi, g: (bi, g, 0, 0)),
        out_shape=jax.ShapeDtypeStruct((b, N_GROUPS, s, GROUP_DIM), jnp.float32),
        scratch_shapes=[
            pltpu.VMEM((N_META + m, GROUP_DIM), jnp.float32),
            pltpu.VMEM((2, m, DFT_RADIX * 2 * GROUP_DIM), jnp.bfloat16),
            pltpu.VMEM((2, F32_SUBLANES, DFT_RADIX * 2 * GROUP_DIM), jnp.float32),
        ],
        compiler_params=pltpu.CompilerParams(
            dimension_semantics=("arbitrary", "arbitrary"), vmem_limit_bytes=VMEM_LIMIT_BYTES),
        name="fourier",
    )(uf, uf_meta, uf, uf_meta, twc, tws, dmat, cs)


def _fourier_tables(s):
    n2 = (s + N_META) // DFT_RADIX
    length = DFT_RADIX * n2
    j2 = np.arange(1, n2, dtype=np.int64)
    k1 = np.arange(DFT_RADIX, dtype=np.int64)
    phi = 2.0 * np.pi * ((k1[:, None] * j2[None, :]) % length) / length
    twc = jnp.broadcast_to(jnp.asarray(np.cos(phi), jnp.float32)[:, :, None], (DFT_RADIX, n2 - 1, GROUP_DIM))
    tws = jnp.broadcast_to(jnp.asarray(np.sin(phi), jnp.float32)[:, :, None], (DFT_RADIX, n2 - 1, GROUP_DIM))
    theta = 2.0 * np.pi * ((j2[:, None] * j2[None, :]) % n2) / n2
    dmat = jnp.asarray(np.concatenate([np.cos(theta), np.sin(theta)], axis=1), jnp.float32)
    c = np.arange(GROUP_DIM, dtype=np.int64)
    psi = 2.0 * np.pi * ((c[:, None] * c[None, :]) % GROUP_DIM) / GROUP_DIM
    cs = jnp.asarray(np.concatenate([np.cos(psi), np.sin(psi)], axis=0), jnp.float32)
    return twc, tws, dmat.astype(jnp.bfloat16), cs.astype(jnp.bfloat16)


def _attn_kernel(tq, q_ref, qn_ref, kt_ref, ktn_ref, v_ref, kmt_ref, vm_ref, o_ref, s_ref, pm_ref):
    bf = jnp.bfloat16
    f32 = jnp.float32
    tiles = o_ref.shape[1] // tq
    n, _, tk = kt_ref.shape[2:]
    group = s_ref.shape[2] // tk
    n_groups = n // group

    def produce(slot, qt, key_chunk):
        pmax = None
        for j in range(group):
            s = jnp.dot(qt, key_chunk(j), preferred_element_type=f32)
            s_ref[slot, :, j * tk:(j + 1) * tk] = s
            for l in range(tk // LANES):
                blk = s[:, l * LANES:(l + 1) * LANES]
                pmax = blk if pmax is None else jnp.maximum(pmax, blk)
        pm_ref[slot] = pmax

    def with_ones(vals):
        return jnp.concatenate([vals, jnp.ones_like(vals)], axis=1)

    def absorb(slot, g, m_i, acc, qt):
        pmax = pm_ref[slot]
        if g == 0:
            lane = lax.broadcasted_iota(jnp.int32, (tq, LANES), 1)
            s_meta = jnp.dot(qt, kmt_ref[0, 0], preferred_element_type=f32)
            s_meta = jnp.where(lane < N_META, s_meta, -jnp.inf)
            pmax = jnp.maximum(pmax, s_meta)
        m_new = jnp.max(pmax, axis=1, keepdims=True)
        if g > 0:
            m_new = jnp.maximum(m_i, m_new)
        p = jnp.exp2((s_ref[slot] - m_new).astype(bf))
        pv = jnp.dot(p, with_ones(v_ref[0, 0, g * group * tk:(g + 1) * group * tk, :]), preferred_element_type=f32)
        if g == 0:
            p_meta = jnp.exp2((s_meta - m_new).astype(bf))
            return m_new, pv + jnp.dot(p_meta, with_ones(vm_ref[0, 0]), preferred_element_type=f32)
        return m_new, jnp.exp2(m_i - m_new) * acc + pv

    first_step = (pl.program_id(0) == 0) & (pl.program_id(1) == 0) & (pl.program_id(2) == 0)

    @pl.when(first_step)
    def _():
        produce(0, q_ref[0, 0, 0:tq, :], lambda j: kt_ref[0, 0, j])

    for t in range(tiles):
        qt = q_ref[0, 0, t * tq:(t + 1) * tq, :]
        m = acc = None
        for g in range(n_groups):
            if g + 1 < n_groups:
                produce((g + 1) % 2, qt, lambda j, g=g: kt_ref[0, 0, (g + 1) * group + j])
            elif t + 1 < tiles:
                produce(0, q_ref[0, 0, (t + 1) * tq:(t + 2) * tq, :], lambda j: kt_ref[0, 0, j])
            else:
                produce(0, qn_ref[0, 0], lambda j: ktn_ref[0, 0, j])
            m, acc = absorb(g % 2, g, m, acc, qt)
        o_ref[0, t * tq:(t + 1) * tq, :] = (acc[:, :V_DIM] / acc[:, V_DIM:]).astype(bf)


def _attention(q, kt, v, kmt, vm, tq, tiles, tg):
    b, nh, s, _ = q.shape
    n, _, tk = kt.shape[2:]
    nq = s // (tiles * tq)
    assert n * tk == s and tg % tk == 0 and (s // tg) % 2 == 0

    def next_step(bi, h, qi):
        wrap_q = qi + 1 == nq
        wrap_h = wrap_q & (h + 1 == nh)
        bn = jnp.where(wrap_h, jnp.where(bi + 1 == b, 0, bi + 1), bi)
        hn = jnp.where(wrap_q, jnp.where(h + 1 == nh, 0, h + 1), h)
        return bn, hn, jnp.where(wrap_q, 0, qi + 1)

    def next_q(bi, h, qi):
        bn, hn, qn = next_step(bi, h, qi)
        return bn, hn, qn * tiles, 0

    def next_keys(bi, h, qi):
        bn, hn, _ = next_step(bi, h, qi)
        return bn, hn, 0, 0, 0

    return pl.pallas_call(
        functools.partial(_attn_kernel, tq),
        grid=(b, nh, nq),
        in_specs=[
            pl.BlockSpec((1, 1, tiles * tq, QK_DIM), lambda bi, h, qi: (bi, h, qi, 0)),
            pl.BlockSpec((1, 1, tq, QK_DIM), next_q),
            pl.BlockSpec((1, 1, n, QK_DIM, tk), lambda bi, h, qi: (bi, h, 0, 0, 0)),
            pl.BlockSpec((1, 1, tg // tk, QK_DIM, tk), next_keys),
            pl.BlockSpec((1, 1, s, V_DIM), lambda bi, h, qi: (bi, h, 0, 0)),
            pl.BlockSpec((1, 1, QK_DIM, LANES), lambda bi, h, qi: (0, h, 0, 0)),
            pl.BlockSpec((1, 1, LANES, V_DIM), lambda bi, h, qi: (0, h, 0, 0)),
        ],
        out_specs=pl.BlockSpec((1, tiles * tq, V_DIM), lambda bi, h, qi: (bi, qi, h)),
        out_shape=jax.ShapeDtypeStruct((b, s, ATTN_DIM), jnp.bfloat16),
        scratch_shapes=[pltpu.VMEM((2, tq, tg), jnp.float32), pltpu.VMEM((2, tq, LANES), jnp.float32)],
        compiler_params=pltpu.CompilerParams(
            dimension_semantics=("arbitrary", "arbitrary", "arbitrary"), vmem_limit_bytes=VMEM_LIMIT_BYTES),
        name="attention",
    )(q, q, kt, kt, v, kmt, vm)


def _tail_kernel(x_ref, y_ref, o_ref, sa_ref, sb_ref, wfo_ref, wao_ref, wo_ref, g2_ref, wg_ref, wu_ref, wd_ref,
                 gf_ref, out_ref):
    bf = jnp.bfloat16
    f32 = jnp.float32
    yf = jnp.concatenate([y_ref[0, g] for g in range(N_GROUPS)], axis=1).astype(bf)
    ya = jnp.dot(yf, wfo_ref[...], preferred_element_type=f32)
    yb = jnp.dot(o_ref[0], wao_ref[...], preferred_element_type=f32)
    merged = sa_ref[0].astype(f32) * ya + sb_ref[0].astype(f32) * yb
    x1 = x_ref[0] + jnp.dot(merged.astype(bf), wo_ref[...], preferred_element_type=f32)
    h2 = _rms(x1, g2_ref[...]).astype(bf)
    x2 = x1
    for c0 in range(0, D_FF, FF_CHUNK):
        c1 = min(c0 + FF_CHUNK, D_FF)
        gate = jnp.dot(h2, wg_ref[:, c0:c1], preferred_element_type=f32)
        up = jnp.dot(h2, wu_ref[:, c0:c1], preferred_element_type=f32)
        act = (gate * jax.nn.sigmoid(gate) * up).astype(bf)
        x2 = x2 + jnp.dot(act, wd_ref[c0:c1, :], preferred_element_type=f32)
    out_ref[0] = _rms(x2, gf_ref[...])


def _tail(x, y, o, sa, sb, wfo, wao, wo, g2, wg, wu, wd, gf, tm):
    b, s, _ = x.shape
    tok = lambda bi, si: (bi, si, 0)
    const = lambda bi, si: (0, 0)

    def weight(shape):
        return pl.BlockSpec(shape, const, pipeline_mode=pl.Buffered(1))

    return pl.pallas_call(
        _tail_kernel,
        grid=(b, s // tm),
        in_specs=[
            pl.BlockSpec((1, tm, D_MODEL), tok),
            pl.BlockSpec((1, N_GROUPS, tm, GROUP_DIM), lambda bi, si: (bi, 0, si, 0)),
            pl.BlockSpec((1, tm, ATTN_DIM), tok),
            pl.BlockSpec((1, tm, D_MODEL), tok),
            pl.BlockSpec((1, tm, D_MODEL), tok),
            weight((FOURIER_DIM, D_MODEL)),
            weight((ATTN_DIM, D_MODEL)),
            weight((D_MODEL, D_MODEL)),
            pl.BlockSpec((1, D_MODEL), const),
            weight((D_MODEL, D_FF)),
            weight((D_MODEL, D_FF)),
            weight((D_FF, D_MODEL)),
            pl.BlockSpec((1, D_MODEL), const),
        ],
        out_specs=pl.BlockSpec((1, tm, D_MODEL), tok),
        out_shape=jax.ShapeDtypeStruct((b, s, D_MODEL), jnp.float32),
        compiler_params=pltpu.CompilerParams(
            dimension_semantics=("parallel", "parallel"), vmem_limit_bytes=VMEM_LIMIT_BYTES),
        name="tail",
    )(x, y, o, sa, sb, wfo, wao, wo, g2, wg, wu, wd, gf)


def _rope_tables(start, length):
    half = ROPE_DIM // 2
    lane = np.arange(LANES)
    inv = 1.0 / (ROPE_THETA ** (jnp.arange(0, ROPE_DIM, 2, dtype=jnp.float32) / ROPE_DIM))
    inv = inv[lane % half]
    sign = jnp.asarray(np.where((lane // half) % 2 == 0, -1.0, 1.0), jnp.float32)
    ang = jnp.arange(start, start + length, dtype=jnp.float32)[:, None] * inv[None, :]
    return jnp.cos(ang), jnp.sin(ang) * sign


def _prepare_weights(norm1_g, w_in, q_norm_g, kv_norm_g, w_uq, w_ukv, w_fourier_out, w_attn_out, w_o, norm2_g,
                     w_ffn_gate, w_ffn_up, w_ffn_down, final_norm_g):
    bf = jnp.bfloat16
    w = w_in[0]
    s_uf, s_cq, s_ckv, s_kr = FOURIER_DIM, FOURIER_DIM + Q_RANK, FOURIER_DIM + Q_RANK + KV_RANK, \
        FOURIER_DIM + Q_RANK + KV_RANK + ROPE_DIM
    win = jnp.concatenate(
        [w[:, :s_ckv], w[:, s_kr:], w[:, s_ckv:s_kr], jnp.zeros((D_MODEL, LANES - ROPE_DIM), w.dtype)], axis=1).astype(bf)
    wq = w_uq[0].reshape(Q_RANK, N_HEADS, QK_DIM)
    wuq = jnp.concatenate([wq[:, :, :NOPE_DIM].reshape(Q_RANK, -1), wq[:, :, NOPE_DIM:].reshape(Q_RANK, -1)], axis=1).astype(bf)
    wkv = w_ukv[0].reshape(KV_RANK, N_HEADS, NOPE_DIM + V_DIM)
    wkt = wkv[:, :, :NOPE_DIM].reshape(KV_RANK, -1).T.astype(bf)
    wv = wkv[:, :, NOPE_DIM:].reshape(KV_RANK, -1).astype(bf)
    row = lambda g: g.reshape(1, -1)
    return dict(
        g1=row(norm1_g[0]), win=win, qg=row(q_norm_g[0]), kvg=row(kv_norm_g[0]), wuq=wuq, wkt=wkt, wv=wv,
        wfo=w_fourier_out[0].astype(bf), wao=w_attn_out[0].astype(bf), wo=w_o[0].astype(bf), g2=row(norm2_g[0]),
        wg=w_ffn_gate[0].astype(bf), wu=w_ffn_up[0].astype(bf), wd=w_ffn_down[0].astype(bf), gf=row(final_norm_g))


def _project(p, x, cos2, sin2, tm):
    return _inproj(x, p["g1"], p["win"], p["qg"], p["kvg"], p["wuq"], p["wkt"], p["wv"], cos2, sin2, tm)


def _meta_projection(meta, p):
    meta_pad = jnp.zeros((1, LANES, D_MODEL), meta.dtype).at[0, :N_META].set(meta)
    uf_m, _, kt_m, v_m, _, _ = _project(p, meta_pad, *_rope_tables(0, LANES), LANES)
    return uf_m[0, :, :N_META], kt_m[:, :, 0], v_m


def _trunk(x, meta_proj, rope, p, tk, tq, attn_tiles, attn_groups, tm_tail):
    s = x.shape[1]
    tg = s // attn_groups
    uf_m, kt_m, v_m = meta_proj
    cos2, sin2 = rope
    uf, q, kt, v, sa, sb = _project(p, x, cos2, sin2, tk)
    y = _fourier(uf, uf_m, *_fourier_tables(s))
    o = _attention(q, kt, v, kt_m, v_m, tq, attn_tiles, tg)
    return _tail(x, y, o, sa, sb, p["wfo"], p["wao"], p["wo"], p["g2"], p["wg"], p["wu"], p["wd"], p["gf"], tm_tail)


def kernel(x_prompt, x_sample, meta_tokens, norm1_g, w_in, q_norm_g, kv_norm_g, w_uq, w_ukv, w_fourier_out,
           w_attn_out, w_o, norm2_g, w_ffn_gate, w_ffn_up, w_ffn_down, final_norm_g):
    p = _prepare_weights(norm1_g, w_in, q_norm_g, kv_norm_g, w_uq, w_ukv, w_fourier_out, w_attn_out, w_o, norm2_g,
                         w_ffn_gate, w_ffn_up, w_ffn_down, final_norm_g)
    rope = _rope_tables(N_META, max(x_prompt.shape[1], x_sample.shape[1]))
    meta_proj = _meta_projection(meta_tokens, p)
    cfg = dict(tk=512, tq=1024, attn_tiles=2, attn_groups=4, tm_tail=512)
    y_prompt = _trunk(x_prompt, meta_proj, rope, p, **cfg)
    y_sample = _trunk(x_sample, meta_proj, rope, p, **cfg)
    return (y_prompt, y_sample)
```

```python
import functools
import math

import jax
import jax.numpy as jnp
import numpy as np
from jax import lax
from jax.experimental import pallas as pl
from jax.experimental.pallas import tpu as pltpu

D_MODEL = 1024
N_META = 16
N_GROUPS = 4
GROUP_DIM = 128
FOURIER_DIM = N_GROUPS * GROUP_DIM
N_HEADS = 8
NOPE_DIM = 128
ROPE_DIM = 64
QK_DIM = NOPE_DIM + ROPE_DIM
V_DIM = 128
Q_RANK = 512
KV_RANK = 256
ATTN_DIM = N_HEADS * V_DIM
D_FF = 2816
ROPE_THETA = 10000.0
NORM_EPS = 1e-6
ATTN_SCALE = QK_DIM ** -0.5
LOG2_E = math.log2(math.e)
DFT_RADIX = 16

LANES = 128
F32_SUBLANES = 8
BF16_SUBLANES = 16
VMEM_LIMIT_BYTES = 56 * 1024 * 1024
DFT_K1_BATCH = 4
FF_CHUNK = 1024

_C_UF = 0
_C_CQ = _C_UF + FOURIER_DIM
_C_CKV = _C_CQ + Q_RANK
_C_GA = _C_CKV + KV_RANK
_C_GB = _C_GA + D_MODEL
_C_KR = _C_GB + D_MODEL
IN_COLS = _C_KR + LANES


def _rms(x, g):
    return x * lax.rsqrt(jnp.mean(x * x, axis=-1, keepdims=True) + NORM_EPS) * g


def _rope128(x, cos2, sin2):
    lane = lax.broadcasted_iota(jnp.int32, x.shape, 1)
    first_half = (lane % ROPE_DIM) < (ROPE_DIM // 2)
    partner = jnp.where(first_half, pltpu.roll(x, LANES - ROPE_DIM // 2, 1), pltpu.roll(x, ROPE_DIM // 2, 1))
    return x * cos2 + partner * sin2


def _inproj_kernel(x_ref, g1_ref, win_ref, qg_ref, kvg_ref, wuq_ref, wkt_ref, wv_ref, cos_ref, sin_ref,
                   uf_ref, q_ref, kt_ref, v_ref, sa_ref, sb_ref):
    bf = jnp.bfloat16
    f32 = jnp.float32
    h = _rms(x_ref[0], g1_ref[...]).astype(bf)

    def proj(c0, width):
        return jnp.dot(h, win_ref[:, c0:c0 + width], preferred_element_type=f32)

    cos2 = cos_ref[...]
    sin2 = sin_ref[...]

    ckv_raw = proj(_C_CKV, KV_RANK)
    kr_raw = proj(_C_KR, LANES)
    cq_raw = proj(_C_CQ, Q_RANK)

    ckv = _rms(ckv_raw, kvg_ref[...])
    ckv_t = ckv.T.astype(bf)
    k_t = jnp.dot(wkt_ref[...], ckv_t, preferred_element_type=f32)
    vals = jnp.dot(ckv.astype(bf), wv_ref[...], preferred_element_type=f32)
    cq = _rms(cq_raw, qg_ref[...]).astype(bf)
    q = jnp.dot(cq, wuq_ref[...], preferred_element_type=f32) * (ATTN_SCALE * LOG2_E)

    kr_t = _rope128(kr_raw, cos2, sin2).T[:ROPE_DIM].astype(bf)
    for hd in range(N_HEADS):
        kt_ref[0, hd, 0, :NOPE_DIM, :] = k_t[hd * NOPE_DIM:(hd + 1) * NOPE_DIM].astype(bf)
        kt_ref[0, hd, 0, NOPE_DIM:, :] = kr_t
        v_ref[0, hd] = vals[:, hd * V_DIM:(hd + 1) * V_DIM].astype(bf)

    rope0 = N_HEADS * NOPE_DIM
    for pair in range(N_HEADS // 2):
        qr = _rope128(q[:, rope0 + pair * LANES:rope0 + (pair + 1) * LANES], cos2, sin2).astype(bf)
        for sub in range(2):
            hd = 2 * pair + sub
            q_ref[0, hd, :, :NOPE_DIM] = q[:, hd * NOPE_DIM:(hd + 1) * NOPE_DIM].astype(bf)
            q_ref[0, hd, :, NOPE_DIM:] = qr[:, sub * ROPE_DIM:(sub + 1) * ROPE_DIM]

    uf = proj(_C_UF, FOURIER_DIM)
    for g in range(N_GROUPS):
        uf_ref[0, g] = uf[:, g * GROUP_DIM:(g + 1) * GROUP_DIM]
    sa_ref[0] = jax.nn.sigmoid(proj(_C_GA, D_MODEL)).astype(bf)
    sb_ref[0] = jax.nn.sigmoid(proj(_C_GB, D_MODEL)).astype(bf)


def _const_spec(shape):
    return pl.BlockSpec(shape, lambda *_: (0,) * len(shape), pipeline_mode=pl.Buffered(1))


def _inproj(x, g1, win, qg, kvg, wuq, wkt, wv, cos2, sin2, tm):
    b, s, _ = x.shape
    tok = lambda bi, si: (bi, si, 0)
    head = lambda bi, si: (bi, 0, si, 0)
    bf = jnp.bfloat16
    return pl.pallas_call(
        _inproj_kernel,
        grid=(b, s // tm),
        in_specs=[
            pl.BlockSpec((1, tm, D_MODEL), tok),
            _const_spec((1, D_MODEL)),
            _const_spec((D_MODEL, IN_COLS)),
            _const_spec((1, Q_RANK)),
            _const_spec((1, KV_RANK)),
            _const_spec((Q_RANK, N_HEADS * QK_DIM)),
            _const_spec((N_HEADS * NOPE_DIM, KV_RANK)),
            _const_spec((KV_RANK, N_HEADS * V_DIM)),
            pl.BlockSpec((tm, LANES), lambda bi, si: (si, 0)),
            pl.BlockSpec((tm, LANES), lambda bi, si: (si, 0)),
        ],
        out_specs=[
            pl.BlockSpec((1, N_GROUPS, tm, GROUP_DIM), head),
            pl.BlockSpec((1, N_HEADS, tm, QK_DIM), head),
            pl.BlockSpec((1, N_HEADS, 1, QK_DIM, tm), lambda bi, si: (bi, 0, si, 0, 0)),
            pl.BlockSpec((1, N_HEADS, tm, V_DIM), head),
            pl.BlockSpec((1, tm, D_MODEL), tok),
            pl.BlockSpec((1, tm, D_MODEL), tok),
        ],
        out_shape=[
            jax.ShapeDtypeStruct((b, N_GROUPS, s, GROUP_DIM), jnp.float32),
            jax.ShapeDtypeStruct((b, N_HEADS, s, QK_DIM), bf),
            jax.ShapeDtypeStruct((b, N_HEADS, s // tm, QK_DIM, tm), bf),
            jax.ShapeDtypeStruct((b, N_HEADS, s, V_DIM), bf),
            jax.ShapeDtypeStruct((b, s, D_MODEL), bf),
            jax.ShapeDtypeStruct((b, s, D_MODEL), bf),
        ],
        compiler_params=pltpu.CompilerParams(
            dimension_semantics=("parallel", "parallel"), vmem_limit_bytes=VMEM_LIMIT_BYTES),
        name="inproj",
    )(x, g1, win, qg, kvg, wuq, wkt, wv, cos2, sin2)


def _cneg(v):
    return None if v is None else -v


def _cadd(a, b):
    if a is None:
        return b
    if b is None:
        return a
    return a + b


def _csub(a, b):
    if b is None:
        return a
    if a is None:
        return -b
    return a - b


def _fft(xs):
    n = len(xs)
    if n == 1:
        return xs
    ev = _fft(xs[0::2])
    od = _fft(xs[1::2])
    out = [None] * n
    for k in range(n // 2):
        re, im = od[k]
        if k == 0:
            tr, ti = re, im
        elif 4 * k == n:
            tr, ti = im, _cneg(re)
        else:
            c = math.cos(2 * math.pi * k / n)
            s = math.sin(2 * math.pi * k / n)
            tr = _cadd(None if re is None else re * c, None if im is None else im * s)
            ti = _csub(None if im is None else im * c, None if re is None else re * s)
        out[k] = (_cadd(ev[k][0], tr), _cadd(ev[k][1], ti))
        out[k + n // 2] = (_csub(ev[k][0], tr), _csub(ev[k][1], ti))
    return out


def _dft16_real(rows):
    out = _fft([(r, None) for r in rows])
    zero = jnp.zeros_like(rows[0])
    out = [(zero if re is None else re, zero if im is None else im) for re, im in out]
    for k in range(DFT_RADIX // 2 + 1, DFT_RADIX):
        re, im = out[DFT_RADIX - k]
        out[k] = (re, -im)
    return out


def _fourier_kernel(n2, x0_ref, xm0_ref, xn_ref, xmn_ref, twc_ref, tws_ref, dmat_ref, cs_ref, y_ref,
                    xrun_ref, z_ref, z0_ref):
    m = n2 - 1
    bf = jnp.bfloat16
    zw = 2 * GROUP_DIM
    slot = pl.program_id(1) % 2

    rc = BF16_SUBLANES

    def run_start(j1):
        return j1 * n2 - N_META

    def vpu_head(x_ref, xm_ref, dst):
        xrun_ref[0:N_META, :] = xm_ref[0]
        xrun_ref[N_META:N_META + m, :] = x_ref[0, 0, 0:m, :]
        dc = _dft16_real([xrun_ref[0:1, :]] + [x_ref[0, 0, run_start(j1):run_start(j1) + 1, :]
                                              for j1 in range(1, DFT_RADIX)])
        for k1 in range(DFT_RADIX):
            z0_ref[dst, 0:1, k1 * zw:k1 * zw + GROUP_DIM] = dc[k1][0]
            z0_ref[dst, 0:1, k1 * zw + GROUP_DIM:(k1 + 1) * zw] = dc[k1][1]

    def vpu_chunk(x_ref, dst, r0):
        rows = [xrun_ref[r0 + 1:r0 + 1 + rc, :]]
        rows += [x_ref[0, 0, r0 + run_start(j1) + 1:r0 + run_start(j1) + 1 + rc, :] for j1 in range(1, DFT_RADIX)]
        a = _dft16_real(rows)
        for k1 in range(DFT_RADIX):
            ar, ai = a[k1]
            if k1 == 0:
                zr, zi = ar, ai
            else:
                tc = twc_ref[k1, r0:r0 + rc, :]
                ts = tws_ref[k1, r0:r0 + rc, :]
                zr = tc * ar + ts * ai
                zi = tc * ai - ts * ar
            z_ref[dst, r0:r0 + rc, k1 * zw:k1 * zw + GROUP_DIM] = zr.astype(bf)
            z_ref[dst, r0:r0 + rc, k1 * zw + GROUP_DIM:(k1 + 1) * zw] = zi.astype(bf)

    scale = 1.0 / math.sqrt(GROUP_DIM * DFT_RADIX * n2)

    def mxu_batch(k0):
        cols = slice(k0 * zw, (k0 + DFT_K1_BATCH) * zw)
        z = z_ref[slot, :, cols]
        swapped = jnp.concatenate(
            [blk for i in range(DFT_K1_BATCH)
             for blk in (z[:, i * zw + GROUP_DIM:(i + 1) * zw], -z[:, i * zw:i * zw + GROUP_DIM])], axis=1)
        rhs = jnp.concatenate([z, swapped], axis=0)
        p = jnp.dot(dmat_ref[...], rhs, preferred_element_type=jnp.float32)
        p = (p + z0_ref[slot, 0:1, cols]).astype(bf)
        stacked = jnp.concatenate([p[:, i * zw:(i + 1) * zw] for i in range(DFT_K1_BATCH)], axis=0)
        y = jnp.dot(stacked, cs_ref[...], preferred_element_type=jnp.float32) * scale
        for i in range(DFT_K1_BATCH):
            y_ref[0, 0, pl.ds(k0 + i, m, stride=DFT_RADIX), :] = y[i * m:(i + 1) * m]

    @pl.when((pl.program_id(0) == 0) & (pl.program_id(1) == 0))
    def _():
        vpu_head(x0_ref, xm0_ref, 0)
        for r0 in range(0, m, rc):
            vpu_chunk(x0_ref, 0, r0)

    vpu_head(xn_ref, xmn_ref, 1 - slot)
    for r0 in range(0, m, rc):
        vpu_chunk(xn_ref, 1 - slot, r0)
    for k0 in range(0, DFT_RADIX, DFT_K1_BATCH):
        mxu_batch(k0)


def _fourier(uf, uf_meta, twc, tws, dmat, cs):
    b, _, s, _ = uf.shape
    n2 = (s + N_META) // DFT_RADIX
    m = n2 - 1
    assert N_GROUPS % 2 == 0

    def next_step(bi, g):
        wrap = g + 1 == N_GROUPS
        return jnp.where(wrap, jnp.where(bi + 1 == b, 0, bi + 1), bi), jnp.where(wrap, 0, g + 1)

    def next_x(bi, g):
        bn, gn = next_step(bi, g)
        return bn, gn, 0, 0

    def next_meta(bi, g):
        return next_step(bi, g)[1], 0, 0

    return pl.pallas_call(
        functools.partial(_fourier_kernel, n2),
        grid=(b, N_GROUPS),
        in_specs=[
            _const_spec((1, 1, s, GROUP_DIM)),
            _const_spec((1, N_META, GROUP_DIM)),
            pl.BlockSpec((1, 1, s, GROUP_DIM), next_x),
            pl.BlockSpec((1, N_META, GROUP_DIM), next_meta),
            _const_spec((DFT_RADIX, m, GROUP_DIM)),
            _const_spec((DFT_RADIX, m, GROUP_DIM)),
            _const_spec((m, 2 * m)),
            _const_spec((2 * GROUP_DIM, GROUP_DIM)),
        ],
        out_specs=pl.BlockSpec((1, 1, s, GROUP_DIM), lambda bi, g: (bi, g, 0, 0)),
        out_shape=jax.ShapeDtypeStruct((b, N_GROUPS, s, GROUP_DIM), jnp.float32),
        scratch_shapes=[
            pltpu.VMEM((N_META + m, GROUP_DIM), jnp.float32),
            pltpu.VMEM((2, m, DFT_RADIX * 2 * GROUP_DIM), jnp.bfloat16),
            pltpu.VMEM((2, F32_SUBLANES, DFT_RADIX * 2 * GROUP_DIM), jnp.float32),
        ],
        compiler_params=pltpu.CompilerParams(
            dimension_semantics=("arbitrary", "arbitrary"), vmem_limit_bytes=VMEM_LIMIT_BYTES),
        name="fourier",
    )(uf, uf_meta, uf, uf_meta, twc, tws, dmat, cs)


def _fourier_tables(s):
    n2 = (s + N_META) // DFT_RADIX
    length = DFT_RADIX * n2
    j2 = np.arange(1, n2, dtype=np.int64)
    k1 = np.arange(DFT_RADIX, dtype=np.int64)
    phi = 2.0 * np.pi * ((k1[:, None] * j2[None, :]) % length) / length
    twc = jnp.broadcast_to(jnp.asarray(np.cos(phi), jnp.float32)[:, :, None], (DFT_RADIX, n2 - 1, GROUP_DIM))
    tws = jnp.broadcast_to(jnp.asarray(np.sin(phi), jnp.float32)[:, :, None], (DFT_RADIX, n2 - 1, GROUP_DIM))
    theta = 2.0 * np.pi * ((j2[:, None] * j2[None, :]) % n2) / n2
    dmat = jnp.asarray(np.concatenate([np.cos(theta), np.sin(theta)], axis=1), jnp.float32)
    c = np.arange(GROUP_DIM, dtype=np.int64)
    psi = 2.0 * np.pi * ((c[:, None] * c[None, :]) % GROUP_DIM) / GROUP_DIM
    cs = jnp.asarray(np.concatenate([np.cos(psi), np.sin(psi)], axis=0), jnp.float32)
    return twc, tws, dmat.astype(jnp.bfloat16), cs.astype(jnp.bfloat16)


def _attn_kernel(tq, q_ref, qn_ref, kt_ref, ktn_ref, v_ref, kmt_ref, vm_ref, o_ref, s_ref, pm_ref):
    bf = jnp.bfloat16
    f32 = jnp.float32
    tiles = o_ref.shape[1] // tq
    n, _, tk = kt_ref.shape[2:]
    group = s_ref.shape[2] // tk
    n_groups = n // group

    def produce(slot, qt, key_chunk):
        pmax = None
        for j in range(group):
            s = jnp.dot(qt, key_chunk(j), preferred_element_type=f32)
            s_ref[slot, :, j * tk:(j + 1) * tk] = s
            for l in range(tk // LANES):
                blk = s[:, l * LANES:(l + 1) * LANES]
                pmax = blk if pmax is None else jnp.maximum(pmax, blk)
        pm_ref[slot] = pmax

    def with_ones(vals):
        return jnp.concatenate([vals, jnp.ones_like(vals)], axis=1)

    def absorb(slot, g, m_i, acc, qt):
        pmax = pm_ref[slot]
        if g == 0:
            lane = lax.broadcasted_iota(jnp.int32, (tq, LANES), 1)
            s_meta = jnp.dot(qt, kmt_ref[0, 0], preferred_element_type=f32)
            s_meta = jnp.where(lane < N_META, s_meta, -jnp.inf)
            pmax = jnp.maximum(pmax, s_meta)
        m_new = jnp.max(pmax, axis=1, keepdims=True)
        if g > 0:
            m_new = jnp.maximum(m_i, m_new)
        p = jnp.exp2((s_ref[slot] - m_new).astype(bf))
        pv = jnp.dot(p, with_ones(v_ref[0, 0, g * group * tk:(g + 1) * group * tk, :]), preferred_element_type=f32)
        if g == 0:
            p_meta = jnp.exp2((s_meta - m_new).astype(bf))
            return m_new, pv + jnp.dot(p_meta, with_ones(vm_ref[0, 0]), preferred_element_type=f32)
        return m_new, jnp.exp2(m_i - m_new) * acc + pv

    first_step = (pl.program_id(0) == 0) & (pl.program_id(1) == 0) & (pl.program_id(2) == 0)

    @pl.when(first_step)
    def _():
        produce(0, q_ref[0, 0, 0:tq, :], lambda j: kt_ref[0, 0, j])

    for t in range(tiles):
        qt = q_ref[0, 0, t * tq:(t + 1) * tq, :]
        m = acc = None
        for g in range(n_groups):
            if g + 1 < n_groups:
                produce((g + 1) % 2, qt, lambda j, g=g: kt_ref[0, 0, (g + 1) * group + j])
            elif t + 1 < tiles:
                produce(0, q_ref[0, 0, (t + 1) * tq:(t + 2) * tq, :], lambda j: kt_ref[0, 0, j])
            else:
                produce(0, qn_ref[0, 0], lambda j: ktn_ref[0, 0, j])
            m, acc = absorb(g % 2, g, m, acc, qt)
        o_ref[0, t * tq:(t + 1) * tq, :] = (acc[:, :V_DIM] / acc[:, V_DIM:]).astype(bf)


def _attention(q, kt, v, kmt, vm, tq, tiles, tg):
    b, nh, s, _ = q.shape
    n, _, tk = kt.shape[2:]
    nq = s // (tiles * tq)
    assert n * tk == s and tg % tk == 0 and (s // tg) % 2 == 0

    def next_step(bi, h, qi):
        wrap_q = qi + 1 == nq
        wrap_h = wrap_q & (h + 1 == nh)
        bn = jnp.where(wrap_h, jnp.where(bi + 1 == b, 0, bi + 1), bi)
        hn = jnp.where(wrap_q, jnp.where(h + 1 == nh, 0, h + 1), h)
        return bn, hn, jnp.where(wrap_q, 0, qi + 1)

    def next_q(bi, h, qi):
        bn, hn, qn = next_step(bi, h, qi)
        return bn, hn, qn * tiles, 0

    def next_keys(bi, h, qi):
        bn, hn, _ = next_step(bi, h, qi)
        return bn, hn, 0, 0, 0

    return pl.pallas_call(
        functools.partial(_attn_kernel, tq),
        grid=(b, nh, nq),
        in_specs=[
            pl.BlockSpec((1, 1, tiles * tq, QK_DIM), lambda bi, h, qi: (bi, h, qi, 0)),
            pl.BlockSpec((1, 1, tq, QK_DIM), next_q),
            pl.BlockSpec((1, 1, n, QK_DIM, tk), lambda bi, h, qi: (bi, h, 0, 0, 0)),
            pl.BlockSpec((1, 1, tg // tk, QK_DIM, tk), next_keys),
            pl.BlockSpec((1, 1, s, V_DIM), lambda bi, h, qi: (bi, h, 0, 0)),
            pl.BlockSpec((1, 1, QK_DIM, LANES), lambda bi, h, qi: (0, h, 0, 0)),
            pl.BlockSpec((1, 1, LANES, V_DIM), lambda bi, h, qi: (0, h, 0, 0)),
        ],
        out_specs=pl.BlockSpec((1, tiles * tq, V_DIM), lambda bi, h, qi: (bi, qi, h)),
        out_shape=jax.ShapeDtypeStruct((b, s, ATTN_DIM), jnp.bfloat16),
        scratch_shapes=[pltpu.VMEM((2, tq, tg), jnp.float32), pltpu.VMEM((2, tq, LANES), jnp.float32)],
        compiler_params=pltpu.CompilerParams(
            dimension_semantics=("arbitrary", "arbitrary", "arbitrary"), vmem_limit_bytes=VMEM_LIMIT_BYTES),
        name="attention",
    )(q, q, kt, kt, v, kmt, vm)


def _tail_kernel(x_ref, y_ref, o_ref, sa_ref, sb_ref, wfo_ref, wao_ref, wo_ref, g2_ref, wg_ref, wu_ref, wd_ref,
                 gf_ref, out_ref):
    bf = jnp.bfloat16
    f32 = jnp.float32
    yf = jnp.concatenate([y_ref[0, g] for g in range(N_GROUPS)], axis=1).astype(bf)
    ya = jnp.dot(yf, wfo_ref[...], preferred_element_type=f32)
    yb = jnp.dot(o_ref[0], wao_ref[...], preferred_element_type=f32)
    merged = sa_ref[0].astype(f32) * ya + sb_ref[0].astype(f32) * yb
    x1 = x_ref[0] + jnp.dot(merged.astype(bf), wo_ref[...], preferred_element_type=f32)
    h2 = _rms(x1, g2_ref[...]).astype(bf)
    x2 = x1
    for c0 in range(0, D_FF, FF_CHUNK):
        c1 = min(c0 + FF_CHUNK, D_FF)
        gate = jnp.dot(h2, wg_ref[:, c0:c1], preferred_element_type=f32)
        up = jnp.dot(h2, wu_ref[:, c0:c1], preferred_element_type=f32)
        act = (gate * jax.nn.sigmoid(gate) * up).astype(bf)
        x2 = x2 + jnp.dot(act, wd_ref[c0:c1, :], preferred_element_type=f32)
    out_ref[0] = _rms(x2, gf_ref[...])


def _tail(x, y, o, sa, sb, wfo, wao, wo, g2, wg, wu, wd, gf, tm):
    b, s, _ = x.shape
    tok = lambda bi, si: (bi, si, 0)
    const = lambda bi, si: (0, 0)

    def weight(shape):
        return pl.BlockSpec(shape, const, pipeline_mode=pl.Buffered(1))

    return pl.pallas_call(
        _tail_kernel,
        grid=(b, s // tm),
        in_specs=[
            pl.BlockSpec((1, tm, D_MODEL), tok),
            pl.BlockSpec((1, N_GROUPS, tm, GROUP_DIM), lambda bi, si: (bi, 0, si, 0)),
            pl.BlockSpec((1, tm, ATTN_DIM), tok),
            pl.BlockSpec((1, tm, D_MODEL), tok),
            pl.BlockSpec((1, tm, D_MODEL), tok),
            weight((FOURIER_DIM, D_MODEL)),
            weight((ATTN_DIM, D_MODEL)),
            weight((D_MODEL, D_MODEL)),
            pl.BlockSpec((1, D_MODEL), const),
            weight((D_MODEL, D_FF)),
            weight((D_MODEL, D_FF)),
            weight((D_FF, D_MODEL)),
            pl.BlockSpec((1, D_MODEL), const),
        ],
        out_specs=pl.BlockSpec((1, tm, D_MODEL), tok),
        out_shape=jax.ShapeDtypeStruct((b, s, D_MODEL), jnp.float32),
        compiler_params=pltpu.CompilerParams(
            dimension_semantics=("parallel", "parallel"), vmem_limit_bytes=VMEM_LIMIT_BYTES),
        name="tail",
    )(x, y, o, sa, sb, wfo, wao, wo, g2, wg, wu, wd, gf)


def _rope_tables(start, length):
    half = ROPE_DIM // 2
    lane = np.arange(LANES)
    inv = 1.0 / (ROPE_THETA ** (jnp.arange(0, ROPE_DIM, 2, dtype=jnp.float32) / ROPE_DIM))
    inv = inv[lane % half]
    sign = jnp.asarray(np.where((lane // half) % 2 == 0, -1.0, 1.0), jnp.float32)
    ang = jnp.arange(start, start + length, dtype=jnp.float32)[:, None] * inv[None, :]
    return jnp.cos(ang), jnp.sin(ang) * sign


def _prepare_weights(norm1_g, w_in, q_norm_g, kv_norm_g, w_uq, w_ukv, w_fourier_out, w_attn_out, w_o, norm2_g,
                     w_ffn_gate, w_ffn_up, w_ffn_down, final_norm_g):
    bf = jnp.bfloat16
    w = w_in[0]
    s_uf, s_cq, s_ckv, s_kr = FOURIER_DIM, FOURIER_DIM + Q_RANK, FOURIER_DIM + Q_RANK + KV_RANK, \
        FOURIER_DIM + Q_RANK + KV_RANK + ROPE_DIM
    win = jnp.concatenate(
        [w[:, :s_ckv], w[:, s_kr:], w[:, s_ckv:s_kr], jnp.zeros((D_MODEL, LANES - ROPE_DIM), w.dtype)], axis=1).astype(bf)
    wq = w_uq[0].reshape(Q_RANK, N_HEADS, QK_DIM)
    wuq = jnp.concatenate([wq[:, :, :NOPE_DIM].reshape(Q_RANK, -1), wq[:, :, NOPE_DIM:].reshape(Q_RANK, -1)], axis=1).astype(bf)
    wkv = w_ukv[0].reshape(KV_RANK, N_HEADS, NOPE_DIM + V_DIM)
    wkt = wkv[:, :, :NOPE_DIM].reshape(KV_RANK, -1).T.astype(bf)
    wv = wkv[:, :, NOPE_DIM:].reshape(KV_RANK, -1).astype(bf)
    row = lambda g: g.reshape(1, -1)
    return dict(
        g1=row(norm1_g[0]), win=win, qg=row(q_norm_g[0]), kvg=row(kv_norm_g[0]), wuq=wuq, wkt=wkt, wv=wv,
        wfo=w_fourier_out[0].astype(bf), wao=w_attn_out[0].astype(bf), wo=w_o[0].astype(bf), g2=row(norm2_g[0]),
        wg=w_ffn_gate[0].astype(bf), wu=w_ffn_up[0].astype(bf), wd=w_ffn_down[0].astype(bf), gf=row(final_norm_g))


def _project(p, x, cos2, sin2, tm):
    return _inproj(x, p["g1"], p["win"], p["qg"], p["kvg"], p["wuq"], p["wkt"], p["wv"], cos2, sin2, tm)


def _meta_projection(meta, p):
    meta_pad = jnp.zeros((1, LANES, D_MODEL), meta.dtype).at[0, :N_META].set(meta)
    uf_m, _, kt_m, v_m, _, _ = _project(p, meta_pad, *_rope_tables(0, LANES), LANES)
    return uf_m[0, :, :N_META], kt_m[:, :, 0], v_m


def _trunk(x, meta_proj, rope, p, tk, tq, attn_tiles, attn_groups, tm_tail):
    s = x.shape[1]
    tg = s // attn_groups
    uf_m, kt_m, v_m = meta_proj
    cos2, sin2 = rope
    uf, q, kt, v, sa, sb = _project(p, x, cos2, sin2, tk)
    y = _fourier(uf, uf_m, *_fourier_tables(s))
    o = _attention(q, kt, v, kt_m, v_m, tq, attn_tiles, tg)
    return _tail(x, y, o, sa, sb, p["wfo"], p["wao"], p["wo"], p["g2"], p["wg"], p["wu"], p["wd"], p["gf"], tm_tail)


def kernel(x_prompt, x_sample, meta_tokens, norm1_g, w_in, q_norm_g, kv_norm_g, w_uq, w_ukv, w_fourier_out,
           w_attn_out, w_o, norm2_g, w_ffn_gate, w_ffn_up, w_ffn_down, final_norm_g):
    p = _prepare_weights(norm1_g, w_in, q_norm_g, kv_norm_g, w_uq, w_ukv, w_fourier_out, w_attn_out, w_o, norm2_g,
                         w_ffn_gate, w_ffn_up, w_ffn_down, final_norm_g)
    rope = _rope_tables(N_META, max(x_prompt.shape[1], x_sample.shape[1]))
    meta_proj = _meta_projection(meta_tokens, p)
    cfg = dict(tk=512, tq=1024, attn_tiles=4, attn_groups=4, tm_tail=512)
    y_prompt = _trunk(x_prompt, meta_proj, rope, p, **cfg)
    y_sample = _trunk(x_sample, meta_proj, rope, p, **cfg)
    return (y_prompt, y_sample)
```

```python
import functools
import math

import jax
import jax.numpy as jnp
import numpy as np
from jax import lax
from jax.experimental import pallas as pl
from jax.experimental.pallas import tpu as pltpu

D_MODEL = 1024
N_META = 16
N_GROUPS = 4
GROUP_DIM = 128
FOURIER_DIM = N_GROUPS * GROUP_DIM
N_HEADS = 8
NOPE_DIM = 128
ROPE_DIM = 64
QK_DIM = NOPE_DIM + ROPE_DIM
V_DIM = 128
Q_RANK = 512
KV_RANK = 256
ATTN_DIM = N_HEADS * V_DIM
D_FF = 2816
ROPE_THETA = 10000.0
NORM_EPS = 1e-6
ATTN_SCALE = QK_DIM ** -0.5
LOG2_E = math.log2(math.e)
DFT_RADIX = 16

LANES = 128
F32_SUBLANES = 8
BF16_SUBLANES = 16
VMEM_LIMIT_BYTES = 56 * 1024 * 1024
DFT_K1_BATCH = 4
FF_CHUNK = 1024

_C_UF = 0
_C_CQ = _C_UF + FOURIER_DIM
_C_CKV = _C_CQ + Q_RANK
_C_GA = _C_CKV + KV_RANK
_C_GB = _C_GA + D_MODEL
_C_KR = _C_GB + D_MODEL
IN_COLS = _C_KR + LANES


def _rms(x, g):
    return x * lax.rsqrt(jnp.mean(x * x, axis=-1, keepdims=True) + NORM_EPS) * g


def _rope128(x, cos2, sin2):
    lane = lax.broadcasted_iota(jnp.int32, x.shape, 1)
    first_half = (lane % ROPE_DIM) < (ROPE_DIM // 2)
    partner = jnp.where(first_half, pltpu.roll(x, LANES - ROPE_DIM // 2, 1), pltpu.roll(x, ROPE_DIM // 2, 1))
    return x * cos2 + partner * sin2


def _inproj_kernel(x_ref, g1_ref, win_ref, qg_ref, kvg_ref, wuq_ref, wkt_ref, wv_ref, cos_ref, sin_ref,
                   uf_ref, q_ref, kt_ref, v_ref, sa_ref, sb_ref):
    bf = jnp.bfloat16
    f32 = jnp.float32
    h = _rms(x_ref[0], g1_ref[...]).astype(bf)

    def proj(c0, width):
        return jnp.dot(h, win_ref[:, c0:c0 + width], preferred_element_type=f32)

    cos2 = cos_ref[...]
    sin2 = sin_ref[...]

    ckv_raw = proj(_C_CKV, KV_RANK)
    kr_raw = proj(_C_KR, LANES)
    cq_raw = proj(_C_CQ, Q_RANK)

    ckv = _rms(ckv_raw, kvg_ref[...])
    ckv_t = ckv.T.astype(bf)
    k_t = jnp.dot(wkt_ref[...], ckv_t, preferred_element_type=f32)
    vals = jnp.dot(ckv.astype(bf), wv_ref[...], preferred_element_type=f32)
    cq = _rms(cq_raw, qg_ref[...]).astype(bf)
    q = jnp.dot(cq, wuq_ref[...], preferred_element_type=f32) * (ATTN_SCALE * LOG2_E)

    kr_t = _rope128(kr_raw, cos2, sin2).T[:ROPE_DIM].astype(bf)
    for hd in range(N_HEADS):
        kt_ref[0, hd, 0, :NOPE_DIM, :] = k_t[hd * NOPE_DIM:(hd + 1) * NOPE_DIM].astype(bf)
        kt_ref[0, hd, 0, NOPE_DIM:, :] = kr_t
        v_ref[0, hd] = vals[:, hd * V_DIM:(hd + 1) * V_DIM].astype(bf)

    rope0 = N_HEADS * NOPE_DIM
    for pair in range(N_HEADS // 2):
        qr = _rope128(q[:, rope0 + pair * LANES:rope0 + (pair + 1) * LANES], cos2, sin2).astype(bf)
        for sub in range(2):
            hd = 2 * pair + sub
            q_ref[0, hd, :, :NOPE_DIM] = q[:, hd * NOPE_DIM:(hd + 1) * NOPE_DIM].astype(bf)
            q_ref[0, hd, :, NOPE_DIM:] = qr[:, sub * ROPE_DIM:(sub + 1) * ROPE_DIM]

    uf = proj(_C_UF, FOURIER_DIM)
    for g in range(N_GROUPS):
        uf_ref[0, g] = uf[:, g * GROUP_DIM:(g + 1) * GROUP_DIM]
    sa_ref[0] = jax.nn.sigmoid(proj(_C_GA, D_MODEL)).astype(bf)
    sb_ref[0] = jax.nn.sigmoid(proj(_C_GB, D_MODEL)).astype(bf)


def _const_spec(shape):
    return pl.BlockSpec(shape, lambda *_: (0,) * len(shape), pipeline_mode=pl.Buffered(1))


def _inproj(x, g1, win, qg, kvg, wuq, wkt, wv, cos2, sin2, tm):
    b, s, _ = x.shape
    tok = lambda bi, si: (bi, si, 0)
    head = lambda bi, si: (bi, 0, si, 0)
    bf = jnp.bfloat16
    return pl.pallas_call(
        _inproj_kernel,
        grid=(b, s // tm),
        in_specs=[
            pl.BlockSpec((1, tm, D_MODEL), tok),
            _const_spec((1, D_MODEL)),
            _const_spec((D_MODEL, IN_COLS)),
            _const_spec((1, Q_RANK)),
            _const_spec((1, KV_RANK)),
            _const_spec((Q_RANK, N_HEADS * QK_DIM)),
            _const_spec((N_HEADS * NOPE_DIM, KV_RANK)),
            _const_spec((KV_RANK, N_HEADS * V_DIM)),
            pl.BlockSpec((tm, LANES), lambda bi, si: (si, 0)),
            pl.BlockSpec((tm, LANES), lambda bi, si: (si, 0)),
        ],
        out_specs=[
            pl.BlockSpec((1, N_GROUPS, tm, GROUP_DIM), head),
            pl.BlockSpec((1, N_HEADS, tm, QK_DIM), head),
            pl.BlockSpec((1, N_HEADS, 1, QK_DIM, tm), lambda bi, si: (bi, 0, si, 0, 0)),
            pl.BlockSpec((1, N_HEADS, tm, V_DIM), head),
            pl.BlockSpec((1, tm, D_MODEL), tok),
            pl.BlockSpec((1, tm, D_MODEL), tok),
        ],
        out_shape=[
            jax.ShapeDtypeStruct((b, N_GROUPS, s, GROUP_DIM), jnp.float32),
            jax.ShapeDtypeStruct((b, N_HEADS, s, QK_DIM), bf),
            jax.ShapeDtypeStruct((b, N_HEADS, s // tm, QK_DIM, tm), bf),
            jax.ShapeDtypeStruct((b, N_HEADS, s, V_DIM), bf),
            jax.ShapeDtypeStruct((b, s, D_MODEL), bf),
            jax.ShapeDtypeStruct((b, s, D_MODEL), bf),
        ],
        compiler_params=pltpu.CompilerParams(
            dimension_semantics=("parallel", "parallel"), vmem_limit_bytes=VMEM_LIMIT_BYTES),
        name="inproj",
    )(x, g1, win, qg, kvg, wuq, wkt, wv, cos2, sin2)


def _cneg(v):
    return None if v is None else -v


def _cadd(a, b):
    if a is None:
        return b
    if b is None:
        return a
    return a + b


def _csub(a, b):
    if b is None:
        return a
    if a is None:
        return -b
    return a - b


def _fft(xs):
    n = len(xs)
    if n == 1:
        return xs
    ev = _fft(xs[0::2])
    od = _fft(xs[1::2])
    out = [None] * n
    for k in range(n // 2):
        re, im = od[k]
        if k == 0:
            tr, ti = re, im
        elif 4 * k == n:
            tr, ti = im, _cneg(re)
        else:
            c = math.cos(2 * math.pi * k / n)
            s = math.sin(2 * math.pi * k / n)
            tr = _cadd(None if re is None else re * c, None if im is None else im * s)
            ti = _csub(None if im is None else im * c, None if re is None else re * s)
        out[k] = (_cadd(ev[k][0], tr), _cadd(ev[k][1], ti))
        out[k + n // 2] = (_csub(ev[k][0], tr), _csub(ev[k][1], ti))
    return out


def _dft16_real(rows):
    out = _fft([(r, None) for r in rows])
    zero = jnp.zeros_like(rows[0])
    out = [(zero if re is None else re, zero if im is None else im) for re, im in out]
    for k in range(DFT_RADIX // 2 + 1, DFT_RADIX):
        re, im = out[DFT_RADIX - k]
        out[k] = (re, -im)
    return out


def _fourier_kernel(n2, x0_ref, xm0_ref, xn_ref, xmn_ref, twc_ref, tws_ref, dmat_ref, cs_ref, y_ref,
                    xrun_ref, z_ref, z0_ref):
    m = n2 - 1
    bf = jnp.bfloat16
    zw = 2 * GROUP_DIM
    slot = pl.program_id(1) % 2

    rc = BF16_SUBLANES

    def run_start(j1):
        return j1 * n2 - N_META

    def vpu_head(x_ref, xm_ref, dst):
        xrun_ref[0:N_META, :] = xm_ref[0]
        xrun_ref[N_META:N_META + m, :] = x_ref[0, 0, 0:m, :]
        dc = _dft16_real([xrun_ref[0:1, :]] + [x_ref[0, 0, run_start(j1):run_start(j1) + 1, :]
                                              for j1 in range(1, DFT_RADIX)])
        for k1 in range(DFT_RADIX):
            z0_ref[dst, 0:1, k1 * zw:k1 * zw + GROUP_DIM] = dc[k1][0]
            z0_ref[dst, 0:1, k1 * zw + GROUP_DIM:(k1 + 1) * zw] = dc[k1][1]

    def vpu_chunk(x_ref, dst, r0):
        rows = [xrun_ref[r0 + 1:r0 + 1 + rc, :]]
        rows += [x_ref[0, 0, r0 + run_start(j1) + 1:r0 + run_start(j1) + 1 + rc, :] for j1 in range(1, DFT_RADIX)]
        a = _dft16_real(rows)
        for k1 in range(DFT_RADIX):
            ar, ai = a[k1]
            if k1 == 0:
                zr, zi = ar, ai
            else:
                tc = twc_ref[k1, r0:r0 + rc, :]
                ts = tws_ref[k1, r0:r0 + rc, :]
                zr = tc * ar + ts * ai
                zi = tc * ai - ts * ar
            z_ref[dst, r0:r0 + rc, k1 * zw:k1 * zw + GROUP_DIM] = zr.astype(bf)
            z_ref[dst, r0:r0 + rc, k1 * zw + GROUP_DIM:(k1 + 1) * zw] = zi.astype(bf)

    scale = 1.0 / math.sqrt(GROUP_DIM * DFT_RADIX * n2)

    def mxu_batch(k0):
        cols = slice(k0 * zw, (k0 + DFT_K1_BATCH) * zw)
        z = z_ref[slot, :, cols]
        swapped = jnp.concatenate(
            [blk for i in range(DFT_K1_BATCH)
             for blk in (z[:, i * zw + GROUP_DIM:(i + 1) * zw], -z[:, i * zw:i * zw + GROUP_DIM])], axis=1)
        rhs = jnp.concatenate([z, swapped], axis=0)
        p = jnp.dot(dmat_ref[...], rhs, preferred_element_type=jnp.float32)
        p = (p + z0_ref[slot, 0:1, cols]).astype(bf)
        stacked = jnp.concatenate([p[:, i * zw:(i + 1) * zw] for i in range(DFT_K1_BATCH)], axis=0)
        y = jnp.dot(stacked, cs_ref[...], preferred_element_type=jnp.float32) * scale
        for i in range(DFT_K1_BATCH):
            y_ref[0, 0, pl.ds(k0 + i, m, stride=DFT_RADIX), :] = y[i * m:(i + 1) * m]

    @pl.when((pl.program_id(0) == 0) & (pl.program_id(1) == 0))
    def _():
        vpu_head(x0_ref, xm0_ref, 0)
        for r0 in range(0, m, rc):
            vpu_chunk(x0_ref, 0, r0)

    vpu_head(xn_ref, xmn_ref, 1 - slot)
    for r0 in range(0, m, rc):
        vpu_chunk(xn_ref, 1 - slot, r0)
    for k0 in range(0, DFT_RADIX, DFT_K1_BATCH):
        mxu_batch(k0)


def _fourier(uf, uf_meta, twc, tws, dmat, cs):
    b, _, s, _ = uf.shape
    n2 = (s + N_META) // DFT_RADIX
    m = n2 - 1
    assert N_GROUPS % 2 == 0

    def next_step(bi, g):
        wrap = g + 1 == N_GROUPS
        return jnp.where(wrap, jnp.where(bi + 1 == b, 0, bi + 1), bi), jnp.where(wrap, 0, g + 1)

    def next_x(bi, g):
        bn, gn = next_step(bi, g)
        return bn, gn, 0, 0

    def next_meta(bi, g):
        return next_step(bi, g)[1], 0, 0

    return pl.pallas_call(
        functools.partial(_fourier_kernel, n2),
        grid=(b, N_GROUPS),
        in_specs=[
            _const_spec((1, 1, s, GROUP_DIM)),
            _const_spec((1, N_META, GROUP_DIM)),
            pl.BlockSpec((1, 1, s, GROUP_DIM), next_x),
            pl.BlockSpec((1, N_META, GROUP_DIM), next_meta),
            _const_spec((DFT_RADIX, m, GROUP_DIM)),
            _const_spec((DFT_RADIX, m, GROUP_DIM)),
            _const_spec((m, 2 * m)),
            _const_spec((2 * GROUP_DIM, GROUP_DIM)),
        ],
        out_specs=pl.BlockSpec((1, 1, s, GROUP_DIM), lambda bi, g: (bi, g, 0, 0)),
        out_shape=jax.ShapeDtypeStruct((b, N_GROUPS, s, GROUP_DIM), jnp.float32),
        scratch_shapes=[
            pltpu.VMEM((N_META + m, GROUP_DIM), jnp.float32),
            pltpu.VMEM((2, m, DFT_RADIX * 2 * GROUP_DIM), jnp.bfloat16),
            pltpu.VMEM((2, F32_SUBLANES, DFT_RADIX * 2 * GROUP_DIM), jnp.float32),
        ],
        compiler_params=pltpu.CompilerParams(
            dimension_semantics=("arbitrary", "arbitrary"), vmem_limit_bytes=VMEM_LIMIT_BYTES),
        name="fourier",
    )(uf, uf_meta, uf, uf_meta, twc, tws, dmat, cs)


def _fourier_tables(s):
    n2 = (s + N_META) // DFT_RADIX
    length = DFT_RADIX * n2
    j2 = np.arange(1, n2, dtype=np.int64)
    k1 = np.arange(DFT_RADIX, dtype=np.int64)
    phi = 2.0 * np.pi * ((k1[:, None] * j2[None, :]) % length) / length
    twc = jnp.broadcast_to(jnp.asarray(np.cos(phi), jnp.float32)[:, :, None], (DFT_RADIX, n2 - 1, GROUP_DIM))
    tws = jnp.broadcast_to(jnp.asarray(np.sin(phi), jnp.float32)[:, :, None], (DFT_RADIX, n2 - 1, GROUP_DIM))
    theta = 2.0 * np.pi * ((j2[:, None] * j2[None, :]) % n2) / n2
    dmat = jnp.asarray(np.concatenate([np.cos(theta), np.sin(theta)], axis=1), jnp.float32)
    c = np.arange(GROUP_DIM, dtype=np.int64)
    psi = 2.0 * np.pi * ((c[:, None] * c[None, :]) % GROUP_DIM) / GROUP_DIM
    cs = jnp.asarray(np.concatenate([np.cos(psi), np.sin(psi)], axis=0), jnp.float32)
    return twc, tws, dmat.astype(jnp.bfloat16), cs.astype(jnp.bfloat16)


def _attn_kernel(tq, q_ref, qn_ref, kt_ref, ktn_ref, v_ref, kmt_ref, vm_ref, o_ref, s_ref, pm_ref):
    bf = jnp.bfloat16
    f32 = jnp.float32
    tiles = o_ref.shape[1] // tq
    n, _, tk = kt_ref.shape[2:]
    group = s_ref.shape[2] // tk
    n_groups = n // group

    def produce(slot, qt, key_chunk):
        pmax = None
        for j in range(group):
            s = jnp.dot(qt, key_chunk(j), preferred_element_type=f32)
            s_ref[slot, :, j * tk:(j + 1) * tk] = s
            for l in range(tk // LANES):
                blk = s[:, l * LANES:(l + 1) * LANES]
                pmax = blk if pmax is None else jnp.maximum(pmax, blk)
        pm_ref[slot] = pmax

    def with_ones(vals):
        return jnp.concatenate([vals, jnp.ones_like(vals)], axis=1)

    def absorb(slot, g, m_i, acc, qt):
        pmax = pm_ref[slot]
        if g == 0:
            lane = lax.broadcasted_iota(jnp.int32, (tq, LANES), 1)
            s_meta = jnp.dot(qt, kmt_ref[0, 0], preferred_element_type=f32)
            s_meta = jnp.where(lane < N_META, s_meta, -jnp.inf)
            pmax = jnp.maximum(pmax, s_meta)
        m_new = jnp.max(pmax, axis=1, keepdims=True)
        if g > 0:
            m_new = jnp.maximum(m_i, m_new)
        p = jnp.exp2((s_ref[slot] - m_new).astype(bf))
        pv = jnp.dot(p, with_ones(v_ref[0, 0, g * group * tk:(g + 1) * group * tk, :]), preferred_element_type=f32)
        if g == 0:
            p_meta = jnp.exp2((s_meta - m_new).astype(bf))
            return m_new, pv + jnp.dot(p_meta, with_ones(vm_ref[0, 0]), preferred_element_type=f32)
        return m_new, jnp.exp2(m_i - m_new) * acc + pv

    first_step = (pl.program_id(0) == 0) & (pl.program_id(1) == 0) & (pl.program_id(2) == 0)

    @pl.when(first_step)
    def _():
        produce(0, q_ref[0, 0, 0:tq, :], lambda j: kt_ref[0, 0, j])

    for t in range(tiles):
        qt = q_ref[0, 0, t * tq:(t + 1) * tq, :]
        m = acc = None
        for g in range(n_groups):
            if g + 1 < n_groups:
                produce((g + 1) % 2, qt, lambda j, g=g: kt_ref[0, 0, (g + 1) * group + j])
            elif t + 1 < tiles:
                produce(0, q_ref[0, 0, (t + 1) * tq:(t + 2) * tq, :], lambda j: kt_ref[0, 0, j])
            else:
                produce(0, qn_ref[0, 0], lambda j: ktn_ref[0, 0, j])
            m, acc = absorb(g % 2, g, m, acc, qt)
        o_ref[0, t * tq:(t + 1) * tq, :] = (acc[:, :V_DIM] / acc[:, V_DIM:]).astype(bf)


def _attention(q, kt, v, kmt, vm, tq, tiles, tg):
    b, nh, s, _ = q.shape
    n, _, tk = kt.shape[2:]
    nq = s // (tiles * tq)
    assert n * tk == s and tg % tk == 0 and (s // tg) % 2 == 0

    def next_step(bi, h, qi):
        wrap_q = qi + 1 == nq
        wrap_h = wrap_q & (h + 1 == nh)
        bn = jnp.where(wrap_h, jnp.where(bi + 1 == b, 0, bi + 1), bi)
        hn = jnp.where(wrap_q, jnp.where(h + 1 == nh, 0, h + 1), h)
        return bn, hn, jnp.where(wrap_q, 0, qi + 1)

    def next_q(bi, h, qi):
        bn, hn, qn = next_step(bi, h, qi)
        return bn, hn, qn * tiles, 0

    def next_keys(bi, h, qi):
        bn, hn, _ = next_step(bi, h, qi)
        return bn, hn, 0, 0, 0

    return pl.pallas_call(
        functools.partial(_attn_kernel, tq),
        grid=(b, nh, nq),
        in_specs=[
            pl.BlockSpec((1, 1, tiles * tq, QK_DIM), lambda bi, h, qi: (bi, h, qi, 0)),
            pl.BlockSpec((1, 1, tq, QK_DIM), next_q),
            pl.BlockSpec((1, 1, n, QK_DIM, tk), lambda bi, h, qi: (bi, h, 0, 0, 0)),
            pl.BlockSpec((1, 1, tg // tk, QK_DIM, tk), next_keys),
            pl.BlockSpec((1, 1, s, V_DIM), lambda bi, h, qi: (bi, h, 0, 0)),
            pl.BlockSpec((1, 1, QK_DIM, LANES), lambda bi, h, qi: (0, h, 0, 0)),
            pl.BlockSpec((1, 1, LANES, V_DIM), lambda bi, h, qi: (0, h, 0, 0)),
        ],
        out_specs=pl.BlockSpec((1, tiles * tq, V_DIM), lambda bi, h, qi: (bi, qi, h)),
        out_shape=jax.ShapeDtypeStruct((b, s, ATTN_DIM), jnp.bfloat16),
        scratch_shapes=[pltpu.VMEM((2, tq, tg), jnp.float32), pltpu.VMEM((2, tq, LANES), jnp.float32)],
        compiler_params=pltpu.CompilerParams(
            dimension_semantics=("arbitrary", "arbitrary", "arbitrary"), vmem_limit_bytes=VMEM_LIMIT_BYTES),
        name="attention",
    )(q, q, kt, kt, v, kmt, vm)


def _tail_kernel(x_ref, y_ref, o_ref, sa_ref, sb_ref, wfo_ref, wao_ref, wo_ref, g2_ref, wg_ref, wu_ref, wd_ref,
                 gf_ref, out_ref):
    bf = jnp.bfloat16
    f32 = jnp.float32
    yf = jnp.concatenate([y_ref[0, g] for g in range(N_GROUPS)], axis=1).astype(bf)
    ya = jnp.dot(yf, wfo_ref[...], preferred_element_type=f32)
    yb = jnp.dot(o_ref[0], wao_ref[...], preferred_element_type=f32)
    merged = sa_ref[0].astype(f32) * ya + sb_ref[0].astype(f32) * yb
    x1 = x_ref[0] + jnp.dot(merged.astype(bf), wo_ref[...], preferred_element_type=f32)
    h2 = _rms(x1, g2_ref[...]).astype(bf)
    x2 = x1
    for c0 in range(0, D_FF, FF_CHUNK):
        c1 = min(c0 + FF_CHUNK, D_FF)
        gate = jnp.dot(h2, wg_ref[:, c0:c1], preferred_element_type=f32)
        up = jnp.dot(h2, wu_ref[:, c0:c1], preferred_element_type=f32)
        act = (gate * jax.nn.sigmoid(gate) * up).astype(bf)
        x2 = x2 + jnp.dot(act, wd_ref[c0:c1, :], preferred_element_type=f32)
    out_ref[0] = _rms(x2, gf_ref[...])


def _tail(x, y, o, sa, sb, wfo, wao, wo, g2, wg, wu, wd, gf, tm):
    b, s, _ = x.shape
    tok = lambda bi, si: (bi, si, 0)
    const = lambda bi, si: (0, 0)

    def weight(shape):
        return pl.BlockSpec(shape, const, pipeline_mode=pl.Buffered(1))

    return pl.pallas_call(
        _tail_kernel,
        grid=(b, s // tm),
        in_specs=[
            pl.BlockSpec((1, tm, D_MODEL), tok),
            pl.BlockSpec((1, N_GROUPS, tm, GROUP_DIM), lambda bi, si: (bi, 0, si, 0)),
            pl.BlockSpec((1, tm, ATTN_DIM), tok),
            pl.BlockSpec((1, tm, D_MODEL), tok),
            pl.BlockSpec((1, tm, D_MODEL), tok),
            weight((FOURIER_DIM, D_MODEL)),
            weight((ATTN_DIM, D_MODEL)),
            weight((D_MODEL, D_MODEL)),
            pl.BlockSpec((1, D_MODEL), const),
            weight((D_MODEL, D_FF)),
            weight((D_MODEL, D_FF)),
            weight((D_FF, D_MODEL)),
            pl.BlockSpec((1, D_MODEL), const),
        ],
        out_specs=pl.BlockSpec((1, tm, D_MODEL), tok),
        out_shape=jax.ShapeDtypeStruct((b, s, D_MODEL), jnp.float32),
        compiler_params=pltpu.CompilerParams(
            dimension_semantics=("parallel", "parallel"), vmem_limit_bytes=VMEM_LIMIT_BYTES),
        name="tail",
    )(x, y, o, sa, sb, wfo, wao, wo, g2, wg, wu, wd, gf)


def _rope_tables(start, length):
    half = ROPE_DIM // 2
    lane = np.arange(LANES)
    inv = 1.0 / (ROPE_THETA ** (jnp.arange(0, ROPE_DIM, 2, dtype=jnp.float32) / ROPE_DIM))
    inv = inv[lane % half]
    sign = jnp.asarray(np.where((lane // half) % 2 == 0, -1.0, 1.0), jnp.float32)
    ang = jnp.arange(start, start + length, dtype=jnp.float32)[:, None] * inv[None, :]
    return jnp.cos(ang), jnp.sin(ang) * sign


def _prepare_weights(norm1_g, w_in, q_norm_g, kv_norm_g, w_uq, w_ukv, w_fourier_out, w_attn_out, w_o, norm2_g,
                     w_ffn_gate, w_ffn_up, w_ffn_down, final_norm_g):
    bf = jnp.bfloat16
    w = w_in[0]
    s_uf, s_cq, s_ckv, s_kr = FOURIER_DIM, FOURIER_DIM + Q_RANK, FOURIER_DIM + Q_RANK + KV_RANK, \
        FOURIER_DIM + Q_RANK + KV_RANK + ROPE_DIM
    win = jnp.concatenate(
        [w[:, :s_ckv], w[:, s_kr:], w[:, s_ckv:s_kr], jnp.zeros((D_MODEL, LANES - ROPE_DIM), w.dtype)], axis=1).astype(bf)
    wq = w_uq[0].reshape(Q_RANK, N_HEADS, QK_DIM)
    wuq = jnp.concatenate([wq[:, :, :NOPE_DIM].reshape(Q_RANK, -1), wq[:, :, NOPE_DIM:].reshape(Q_RANK, -1)], axis=1).astype(bf)
    wkv = w_ukv[0].reshape(KV_RANK, N_HEADS, NOPE_DIM + V_DIM)
    wkt = wkv[:, :, :NOPE_DIM].reshape(KV_RANK, -1).T.astype(bf)
    wv = wkv[:, :, NOPE_DIM:].reshape(KV_RANK, -1).astype(bf)
    row = lambda g: g.reshape(1, -1)
    return dict(
        g1=row(norm1_g[0]), win=win, qg=row(q_norm_g[0]), kvg=row(kv_norm_g[0]), wuq=wuq, wkt=wkt, wv=wv,
        wfo=w_fourier_out[0].astype(bf), wao=w_attn_out[0].astype(bf), wo=w_o[0].astype(bf), g2=row(norm2_g[0]),
        wg=w_ffn_gate[0].astype(bf), wu=w_ffn_up[0].astype(bf), wd=w_ffn_down[0].astype(bf), gf=row(final_norm_g))


def _project(p, x, cos2, sin2, tm):
    return _inproj(x, p["g1"], p["win"], p["qg"], p["kvg"], p["wuq"], p["wkt"], p["wv"], cos2, sin2, tm)


def _meta_projection(meta, p):
    meta_pad = jnp.zeros((1, LANES, D_MODEL), meta.dtype).at[0, :N_META].set(meta)
    uf_m, _, kt_m, v_m, _, _ = _project(p, meta_pad, *_rope_tables(0, LANES), LANES)
    return uf_m[0, :, :N_META], kt_m[:, :, 0], v_m


def _trunk(x, meta_proj, rope, p, tk, tq, attn_step_scores, attn_groups, tm_tail):
    s = x.shape[1]
    tg = s // attn_groups
    attn_tiles = max(1, attn_step_scores // (tq * s))
    uf_m, kt_m, v_m = meta_proj
    cos2, sin2 = rope
    uf, q, kt, v, sa, sb = _project(p, x, cos2, sin2, tk)
    y = _fourier(uf, uf_m, *_fourier_tables(s))
    o = _attention(q, kt, v, kt_m, v_m, tq, attn_tiles, tg)
    return _tail(x, y, o, sa, sb, p["wfo"], p["wao"], p["wo"], p["g2"], p["wg"], p["wu"], p["wd"], p["gf"], tm_tail)


def kernel(x_prompt, x_sample, meta_tokens, norm1_g, w_in, q_norm_g, kv_norm_g, w_uq, w_ukv, w_fourier_out,
           w_attn_out, w_o, norm2_g, w_ffn_gate, w_ffn_up, w_ffn_down, final_norm_g):
    p = _prepare_weights(norm1_g, w_in, q_norm_g, kv_norm_g, w_uq, w_ukv, w_fourier_out, w_attn_out, w_o, norm2_g,
                         w_ffn_gate, w_ffn_up, w_ffn_down, final_norm_g)
    rope = _rope_tables(N_META, max(x_prompt.shape[1], x_sample.shape[1]))
    meta_proj = _meta_projection(meta_tokens, p)
    cfg = dict(tk=512, tq=1024, attn_step_scores=1 << 24, attn_groups=4, tm_tail=512)
    y_prompt = _trunk(x_prompt, meta_proj, rope, p, **cfg)
    y_sample = _trunk(x_sample, meta_proj, rope, p, **cfg)
    return (y_prompt, y_sample)
```

```python
import functools
import math

import jax
import jax.numpy as jnp
import numpy as np
from jax import lax
from jax.experimental import pallas as pl
from jax.experimental.pallas import tpu as pltpu

D_MODEL = 1024
N_META = 16
N_GROUPS = 4
GROUP_DIM = 128
FOURIER_DIM = N_GROUPS * GROUP_DIM
N_HEADS = 8
NOPE_DIM = 128
ROPE_DIM = 64
QK_DIM = NOPE_DIM + ROPE_DIM
V_DIM = 128
Q_RANK = 512
KV_RANK = 256
ATTN_DIM = N_HEADS * V_DIM
D_FF = 2816
ROPE_THETA = 10000.0
NORM_EPS = 1e-6
ATTN_SCALE = QK_DIM ** -0.5
LOG2_E = math.log2(math.e)
DFT_RADIX = 16

LANES = 128
F32_SUBLANES = 8
BF16_SUBLANES = 16
VMEM_LIMIT_BYTES = 56 * 1024 * 1024
DFT_K1_BATCH = 4
FF_CHUNK = 1024

_C_UF = 0
_C_CQ = _C_UF + FOURIER_DIM
_C_CKV = _C_CQ + Q_RANK
_C_GA = _C_CKV + KV_RANK
_C_GB = _C_GA + D_MODEL
_C_KR = _C_GB + D_MODEL
IN_COLS = _C_KR + LANES


def _rms(x, g):
    return x * lax.rsqrt(jnp.mean(x * x, axis=-1, keepdims=True) + NORM_EPS) * g


def _rope128(x, cos2, sin2):
    lane = lax.broadcasted_iota(jnp.int32, x.shape, 1)
    first_half = (lane % ROPE_DIM) < (ROPE_DIM // 2)
    partner = jnp.where(first_half, pltpu.roll(x, LANES - ROPE_DIM // 2, 1), pltpu.roll(x, ROPE_DIM // 2, 1))
    return x * cos2 + partner * sin2


def _inproj_kernel(x_ref, g1_ref, win_ref, qg_ref, kvg_ref, wuq_ref, wkt_ref, wv_ref, cos_ref, sin_ref,
                   uf_ref, q_ref, kt_ref, v_ref, sa_ref, sb_ref):
    bf = jnp.bfloat16
    f32 = jnp.float32
    h = _rms(x_ref[0], g1_ref[...]).astype(bf)

    def proj(c0, width):
        return jnp.dot(h, win_ref[:, c0:c0 + width], preferred_element_type=f32)

    cos2 = cos_ref[...]
    sin2 = sin_ref[...]

    ckv_raw = proj(_C_CKV, KV_RANK)
    kr_raw = proj(_C_KR, LANES)
    cq_raw = proj(_C_CQ, Q_RANK)

    ckv = _rms(ckv_raw, kvg_ref[...])
    ckv_t = ckv.T.astype(bf)
    k_t = jnp.dot(wkt_ref[...], ckv_t, preferred_element_type=f32)
    vals = jnp.dot(ckv.astype(bf), wv_ref[...], preferred_element_type=f32)
    cq = _rms(cq_raw, qg_ref[...]).astype(bf)
    q = jnp.dot(cq, wuq_ref[...], preferred_element_type=f32) * (ATTN_SCALE * LOG2_E)

    kr_t = _rope128(kr_raw, cos2, sin2).T[:ROPE_DIM].astype(bf)
    for hd in range(N_HEADS):
        kt_ref[0, hd, 0, :NOPE_DIM, :] = k_t[hd * NOPE_DIM:(hd + 1) * NOPE_DIM].astype(bf)
        kt_ref[0, hd, 0, NOPE_DIM:, :] = kr_t
        v_ref[0, hd] = vals[:, hd * V_DIM:(hd + 1) * V_DIM].astype(bf)

    rope0 = N_HEADS * NOPE_DIM
    for pair in range(N_HEADS // 2):
        qr = _rope128(q[:, rope0 + pair * LANES:rope0 + (pair + 1) * LANES], cos2, sin2).astype(bf)
        for sub in range(2):
            hd = 2 * pair + sub
            q_ref[0, hd, :, :NOPE_DIM] = q[:, hd * NOPE_DIM:(hd + 1) * NOPE_DIM].astype(bf)
            q_ref[0, hd, :, NOPE_DIM:] = qr[:, sub * ROPE_DIM:(sub + 1) * ROPE_DIM]

    uf = proj(_C_UF, FOURIER_DIM)
    for g in range(N_GROUPS):
        uf_ref[0, g] = uf[:, g * GROUP_DIM:(g + 1) * GROUP_DIM]
    sa_ref[0] = jax.nn.sigmoid(proj(_C_GA, D_MODEL)).astype(bf)
    sb_ref[0] = jax.nn.sigmoid(proj(_C_GB, D_MODEL)).astype(bf)


def _const_spec(shape):
    return pl.BlockSpec(shape, lambda *_: (0,) * len(shape), pipeline_mode=pl.Buffered(1))


def _inproj(x, g1, win, qg, kvg, wuq, wkt, wv, cos2, sin2, tm):
    b, s, _ = x.shape
    tok = lambda bi, si: (bi, si, 0)
    head = lambda bi, si: (bi, 0, si, 0)
    bf = jnp.bfloat16
    return pl.pallas_call(
        _inproj_kernel,
        grid=(b, s // tm),
        in_specs=[
            pl.BlockSpec((1, tm, D_MODEL), tok),
            _const_spec((1, D_MODEL)),
            _const_spec((D_MODEL, IN_COLS)),
            _const_spec((1, Q_RANK)),
            _const_spec((1, KV_RANK)),
            _const_spec((Q_RANK, N_HEADS * QK_DIM)),
            _const_spec((N_HEADS * NOPE_DIM, KV_RANK)),
            _const_spec((KV_RANK, N_HEADS * V_DIM)),
            pl.BlockSpec((tm, LANES), lambda bi, si: (si, 0)),
            pl.BlockSpec((tm, LANES), lambda bi, si: (si, 0)),
        ],
        out_specs=[
            pl.BlockSpec((1, N_GROUPS, tm, GROUP_DIM), head),
            pl.BlockSpec((1, N_HEADS, tm, QK_DIM), head),
            pl.BlockSpec((1, N_HEADS, 1, QK_DIM, tm), lambda bi, si: (bi, 0, si, 0, 0)),
            pl.BlockSpec((1, N_HEADS, tm, V_DIM), head),
            pl.BlockSpec((1, tm, D_MODEL), tok),
            pl.BlockSpec((1, tm, D_MODEL), tok),
        ],
        out_shape=[
            jax.ShapeDtypeStruct((b, N_GROUPS, s, GROUP_DIM), jnp.float32),
            jax.ShapeDtypeStruct((b, N_HEADS, s, QK_DIM), bf),
            jax.ShapeDtypeStruct((b, N_HEADS, s // tm, QK_DIM, tm), bf),
            jax.ShapeDtypeStruct((b, N_HEADS, s, V_DIM), bf),
            jax.ShapeDtypeStruct((b, s, D_MODEL), bf),
            jax.ShapeDtypeStruct((b, s, D_MODEL), bf),
        ],
        compiler_params=pltpu.CompilerParams(
            dimension_semantics=("parallel", "parallel"), vmem_limit_bytes=VMEM_LIMIT_BYTES),
        name="inproj",
    )(x, g1, win, qg, kvg, wuq, wkt, wv, cos2, sin2)


def _cneg(v):
    return None if v is None else -v


def _cadd(a, b):
    if a is None:
        return b
    if b is None:
        return a
    return a + b


def _csub(a, b):
    if b is None:
        return a
    if a is None:
        return -b
    return a - b


def _fft(xs):
    n = len(xs)
    if n == 1:
        return xs
    ev = _fft(xs[0::2])
    od = _fft(xs[1::2])
    out = [None] * n
    for k in range(n // 2):
        re, im = od[k]
        if k == 0:
            tr, ti = re, im
        elif 4 * k == n:
            tr, ti = im, _cneg(re)
        else:
            c = math.cos(2 * math.pi * k / n)
            s = math.sin(2 * math.pi * k / n)
            tr = _cadd(None if re is None else re * c, None if im is None else im * s)
            ti = _csub(None if im is None else im * c, None if re is None else re * s)
        out[k] = (_cadd(ev[k][0], tr), _cadd(ev[k][1], ti))
        out[k + n // 2] = (_csub(ev[k][0], tr), _csub(ev[k][1], ti))
    return out


def _dft16_real(rows):
    out = _fft([(r, None) for r in rows])
    zero = jnp.zeros_like(rows[0])
    out = [(zero if re is None else re, zero if im is None else im) for re, im in out]
    for k in range(DFT_RADIX // 2 + 1, DFT_RADIX):
        re, im = out[DFT_RADIX - k]
        out[k] = (re, -im)
    return out


def _fourier_kernel(n2, x0_ref, xm0_ref, xn_ref, xmn_ref, twc_ref, tws_ref, dmat_ref, cs_ref, y_ref,
                    xrun_ref, z_ref, z0_ref):
    m = n2 - 1
    bf = jnp.bfloat16
    zw = 2 * GROUP_DIM
    slot = pl.program_id(1) % 2

    rc = BF16_SUBLANES

    def run_start(j1):
        return j1 * n2 - N_META

    def vpu_head(x_ref, xm_ref, dst):
        xrun_ref[0:N_META, :] = xm_ref[0]
        xrun_ref[N_META:N_META + m, :] = x_ref[0, 0, 0:m, :]
        dc = _dft16_real([xrun_ref[0:1, :]] + [x_ref[0, 0, run_start(j1):run_start(j1) + 1, :]
                                              for j1 in range(1, DFT_RADIX)])
        for k1 in range(DFT_RADIX):
            z0_ref[dst, 0:1, k1 * zw:k1 * zw + GROUP_DIM] = dc[k1][0]
            z0_ref[dst, 0:1, k1 * zw + GROUP_DIM:(k1 + 1) * zw] = dc[k1][1]

    def vpu_chunk(x_ref, dst, r0):
        rows = [xrun_ref[r0 + 1:r0 + 1 + rc, :]]
        rows += [x_ref[0, 0, r0 + run_start(j1) + 1:r0 + run_start(j1) + 1 + rc, :] for j1 in range(1, DFT_RADIX)]
        a = _dft16_real(rows)
        for k1 in range(DFT_RADIX):
            ar, ai = a[k1]
            if k1 == 0:
                zr, zi = ar, ai
            else:
                tc = twc_ref[k1, r0:r0 + rc, :]
                ts = tws_ref[k1, r0:r0 + rc, :]
                zr = tc * ar + ts * ai
                zi = tc * ai - ts * ar
            z_ref[dst, r0:r0 + rc, k1 * zw:k1 * zw + GROUP_DIM] = zr.astype(bf)
            z_ref[dst, r0:r0 + rc, k1 * zw + GROUP_DIM:(k1 + 1) * zw] = zi.astype(bf)

    scale = 1.0 / math.sqrt(GROUP_DIM * DFT_RADIX * n2)

    def mxu_batch(k0):
        cols = slice(k0 * zw, (k0 + DFT_K1_BATCH) * zw)
        z = z_ref[slot, :, cols]
        swapped = jnp.concatenate(
            [blk for i in range(DFT_K1_BATCH)
             for blk in (z[:, i * zw + GROUP_DIM:(i + 1) * zw], -z[:, i * zw:i * zw + GROUP_DIM])], axis=1)
        rhs = jnp.concatenate([z, swapped], axis=0)
        p = jnp.dot(dmat_ref[...], rhs, preferred_element_type=jnp.float32)
        p = (p + z0_ref[slot, 0:1, cols]).astype(bf)
        stacked = jnp.concatenate([p[:, i * zw:(i + 1) * zw] for i in range(DFT_K1_BATCH)], axis=0)
        y = jnp.dot(stacked, cs_ref[...], preferred_element_type=jnp.float32) * scale
        for i in range(DFT_K1_BATCH):
            y_ref[0, 0, pl.ds(k0 + i, m, stride=DFT_RADIX), :] = y[i * m:(i + 1) * m]

    @pl.when((pl.program_id(0) == 0) & (pl.program_id(1) == 0))
    def _():
        vpu_head(x0_ref, xm0_ref, 0)
        for r0 in range(0, m, rc):
            vpu_chunk(x0_ref, 0, r0)

    vpu_head(xn_ref, xmn_ref, 1 - slot)
    for r0 in range(0, m, rc):
        vpu_chunk(xn_ref, 1 - slot, r0)
    for k0 in range(0, DFT_RADIX, DFT_K1_BATCH):
        mxu_batch(k0)


def _fourier(uf, uf_meta, twc, tws, dmat, cs):
    b, _, s, _ = uf.shape
    n2 = (s + N_META) // DFT_RADIX
    m = n2 - 1
    assert N_GROUPS % 2 == 0

    def next_step(bi, g):
        wrap = g + 1 == N_GROUPS
        return jnp.where(wrap, jnp.where(bi + 1 == b, 0, bi + 1), bi), jnp.where(wrap, 0, g + 1)

    def next_x(bi, g):
        bn, gn = next_step(bi, g)
        return bn, gn, 0, 0

    def next_meta(bi, g):
        return next_step(bi, g)[1], 0, 0

    return pl.pallas_call(
        functools.partial(_fourier_kernel, n2),
        grid=(b, N_GROUPS),
        in_specs=[
            _const_spec((1, 1, s, GROUP_DIM)),
            _const_spec((1, N_META, GROUP_DIM)),
            pl.BlockSpec((1, 1, s, GROUP_DIM), next_x),
            pl.BlockSpec((1, N_META, GROUP_DIM), next_meta),
            _const_spec((DFT_RADIX, m, GROUP_DIM)),
            _const_spec((DFT_RADIX, m, GROUP_DIM)),
            _const_spec((m, 2 * m)),
            _const_spec((2 * GROUP_DIM, GROUP_DIM)),
        ],
        out_specs=pl.BlockSpec((1, 1, s, GROUP_DIM), lambda bi, g: (bi, g, 0, 0)),
        out_shape=jax.ShapeDtypeStruct((b, N_GROUPS, s, GROUP_DIM), jnp.float32),
        scratch_shapes=[
            pltpu.VMEM((N_META + m, GROUP_DIM), jnp.float32),
            pltpu.VMEM((2, m, DFT_RADIX * 2 * GROUP_DIM), jnp.bfloat16),
            pltpu.VMEM((2, F32_SUBLANES, DFT_RADIX * 2 * GROUP_DIM), jnp.float32),
        ],
        compiler_params=pltpu.CompilerParams(
            dimension_semantics=("arbitrary", "arbitrary"), vmem_limit_bytes=VMEM_LIMIT_BYTES),
        name="fourier",
    )(uf, uf_meta, uf, uf_meta, twc, tws, dmat, cs)


def _fourier_tables(s):
    n2 = (s + N_META) // DFT_RADIX
    length = DFT_RADIX * n2
    j2 = np.arange(1, n2, dtype=np.int64)
    k1 = np.arange(DFT_RADIX, dtype=np.int64)
    phi = 2.0 * np.pi * ((k1[:, None] * j2[None, :]) % length) / length
    twc = jnp.broadcast_to(jnp.asarray(np.cos(phi), jnp.float32)[:, :, None], (DFT_RADIX, n2 - 1, GROUP_DIM))
    tws = jnp.broadcast_to(jnp.asarray(np.sin(phi), jnp.float32)[:, :, None], (DFT_RADIX, n2 - 1, GROUP_DIM))
    theta = 2.0 * np.pi * ((j2[:, None] * j2[None, :]) % n2) / n2
    dmat = jnp.asarray(np.concatenate([np.cos(theta), np.sin(theta)], axis=1), jnp.float32)
    c = np.arange(GROUP_DIM, dtype=np.int64)
    psi = 2.0 * np.pi * ((c[:, None] * c[None, :]) % GROUP_DIM) / GROUP_DIM
    cs = jnp.asarray(np.concatenate([np.cos(psi), np.sin(psi)], axis=0), jnp.float32)
    return twc, tws, dmat.astype(jnp.bfloat16), cs.astype(jnp.bfloat16)


def _attn_kernel(tq, q_ref, qn_ref, kt_ref, ktn_ref, v_ref, kmt_ref, vm_ref, o_ref, s_ref, pm_ref):
    bf = jnp.bfloat16
    f32 = jnp.float32
    tiles = o_ref.shape[1] // tq
    n, _, tk = kt_ref.shape[2:]
    group = s_ref.shape[2] // tk
    n_groups = n // group

    def produce(slot, qt, key_chunk):
        pmax = None
        for j in range(group):
            s = jnp.dot(qt, key_chunk(j), preferred_element_type=f32)
            s_ref[slot, :, j * tk:(j + 1) * tk] = s
            for l in range(tk // LANES):
                blk = s[:, l * LANES:(l + 1) * LANES]
                pmax = blk if pmax is None else jnp.maximum(pmax, blk)
        pm_ref[slot] = pmax

    def with_ones(vals):
        return jnp.concatenate([vals, jnp.ones_like(vals)], axis=1)

    def absorb(slot, g, m_i, acc, qt):
        pmax = pm_ref[slot]
        if g == 0:
            lane = lax.broadcasted_iota(jnp.int32, (tq, LANES), 1)
            s_meta = jnp.dot(qt, kmt_ref[0, 0], preferred_element_type=f32)
            s_meta = jnp.where(lane < N_META, s_meta, -jnp.inf)
            pmax = jnp.maximum(pmax, s_meta)
        m_new = jnp.max(pmax, axis=1, keepdims=True)
        if g > 0:
            m_new = jnp.maximum(m_i, m_new)
        p = jnp.exp2((s_ref[slot] - m_new).astype(bf))
        pv = jnp.dot(p, with_ones(v_ref[0, 0, g * group * tk:(g + 1) * group * tk, :]), preferred_element_type=f32)
        if g == 0:
            p_meta = jnp.exp2((s_meta - m_new).astype(bf))
            return m_new, pv + jnp.dot(p_meta, with_ones(vm_ref[0, 0]), preferred_element_type=f32)
        return m_new, jnp.exp2(m_i - m_new) * acc + pv

    first_step = (pl.program_id(0) == 0) & (pl.program_id(1) == 0) & (pl.program_id(2) == 0)

    @pl.when(first_step)
    def _():
        produce(0, q_ref[0, 0, 0:tq, :], lambda j: kt_ref[0, 0, j])

    for t in range(tiles):
        qt = q_ref[0, 0, t * tq:(t + 1) * tq, :]
        m = acc = None
        for g in range(n_groups):
            if g + 1 < n_groups:
                produce((g + 1) % 2, qt, lambda j, g=g: kt_ref[0, 0, (g + 1) * group + j])
            elif t + 1 < tiles:
                produce(0, q_ref[0, 0, (t + 1) * tq:(t + 2) * tq, :], lambda j: kt_ref[0, 0, j])
            else:
                produce(0, qn_ref[0, 0], lambda j: ktn_ref[0, 0, j])
            m, acc = absorb(g % 2, g, m, acc, qt)
        o_ref[0, t * tq:(t + 1) * tq, :] = (acc[:, :V_DIM] / acc[:, V_DIM:]).astype(bf)


def _attention(q, kt, v, kmt, vm, tq, tiles, tg):
    b, nh, s, _ = q.shape
    n, _, tk = kt.shape[2:]
    nq = s // (tiles * tq)
    assert n * tk == s and tg % tk == 0 and (s // tg) % 2 == 0

    def next_step(bi, h, qi):
        wrap_q = qi + 1 == nq
        wrap_h = wrap_q & (h + 1 == nh)
        bn = jnp.where(wrap_h, jnp.where(bi + 1 == b, 0, bi + 1), bi)
        hn = jnp.where(wrap_q, jnp.where(h + 1 == nh, 0, h + 1), h)
        return bn, hn, jnp.where(wrap_q, 0, qi + 1)

    def next_q(bi, h, qi):
        bn, hn, qn = next_step(bi, h, qi)
        return bn, hn, qn * tiles, 0

    def next_keys(bi, h, qi):
        bn, hn, _ = next_step(bi, h, qi)
        return bn, hn, 0, 0, 0

    return pl.pallas_call(
        functools.partial(_attn_kernel, tq),
        grid=(b, nh, nq),
        in_specs=[
            pl.BlockSpec((1, 1, tiles * tq, QK_DIM), lambda bi, h, qi: (bi, h, qi, 0)),
            pl.BlockSpec((1, 1, tq, QK_DIM), next_q),
            pl.BlockSpec((1, 1, n, QK_DIM, tk), lambda bi, h, qi: (bi, h, 0, 0, 0)),
            pl.BlockSpec((1, 1, tg // tk, QK_DIM, tk), next_keys),
            pl.BlockSpec((1, 1, s, V_DIM), lambda bi, h, qi: (bi, h, 0, 0)),
            pl.BlockSpec((1, 1, QK_DIM, LANES), lambda bi, h, qi: (0, h, 0, 0)),
            pl.BlockSpec((1, 1, LANES, V_DIM), lambda bi, h, qi: (0, h, 0, 0)),
        ],
        out_specs=pl.BlockSpec((1, tiles * tq, V_DIM), lambda bi, h, qi: (bi, qi, h)),
        out_shape=jax.ShapeDtypeStruct((b, s, ATTN_DIM), jnp.bfloat16),
        scratch_shapes=[pltpu.VMEM((2, tq, tg), jnp.float32), pltpu.VMEM((2, tq, LANES), jnp.float32)],
        compiler_params=pltpu.CompilerParams(
            dimension_semantics=("arbitrary", "arbitrary", "arbitrary"), vmem_limit_bytes=VMEM_LIMIT_BYTES),
        name="attention",
    )(q, q, kt, kt, v, kmt, vm)


def _tail_kernel(x_ref, y_ref, o_ref, sa_ref, sb_ref, wfo_ref, wao_ref, wo_ref, g2_ref, wg_ref, wu_ref, wd_ref,
                 gf_ref, out_ref):
    bf = jnp.bfloat16
    f32 = jnp.float32
    yf = jnp.concatenate([y_ref[0, g] for g in range(N_GROUPS)], axis=1).astype(bf)
    ya = jnp.dot(yf, wfo_ref[...], preferred_element_type=f32)
    yb = jnp.dot(o_ref[0], wao_ref[...], preferred_element_type=f32)
    merged = sa_ref[0].astype(f32) * ya + sb_ref[0].astype(f32) * yb
    x1 = x_ref[0] + jnp.dot(merged.astype(bf), wo_ref[...], preferred_element_type=f32)
    h2 = _rms(x1, g2_ref[...]).astype(bf)
    x2 = x1
    for c0 in range(0, D_FF, FF_CHUNK):
        c1 = min(c0 + FF_CHUNK, D_FF)
        gate = jnp.dot(h2, wg_ref[:, c0:c1], preferred_element_type=f32)
        up = jnp.dot(h2, wu_ref[:, c0:c1], preferred_element_type=f32)
        act = (gate * jax.nn.sigmoid(gate) * up).astype(bf)
        x2 = x2 + jnp.dot(act, wd_ref[c0:c1, :], preferred_element_type=f32)
    out_ref[0] = _rms(x2, gf_ref[...])


def _tail(x, y, o, sa, sb, wfo, wao, wo, g2, wg, wu, wd, gf, tm):
    b, s, _ = x.shape
    tok = lambda bi, si: (bi, si, 0)
    const = lambda bi, si: (0, 0)

    def weight(shape):
        return pl.BlockSpec(shape, const, pipeline_mode=pl.Buffered(1))

    return pl.pallas_call(
        _tail_kernel,
        grid=(b, s // tm),
        in_specs=[
            pl.BlockSpec((1, tm, D_MODEL), tok),
            pl.BlockSpec((1, N_GROUPS, tm, GROUP_DIM), lambda bi, si: (bi, 0, si, 0)),
            pl.BlockSpec((1, tm, ATTN_DIM), tok),
            pl.BlockSpec((1, tm, D_MODEL), tok),
            pl.BlockSpec((1, tm, D_MODEL), tok),
            weight((FOURIER_DIM, D_MODEL)),
            weight((ATTN_DIM, D_MODEL)),
            weight((D_MODEL, D_MODEL)),
            pl.BlockSpec((1, D_MODEL), const),
            weight((D_MODEL, D_FF)),
            weight((D_MODEL, D_FF)),
            weight((D_FF, D_MODEL)),
            pl.BlockSpec((1, D_MODEL), const),
        ],
        out_specs=pl.BlockSpec((1, tm, D_MODEL), tok),
        out_shape=jax.ShapeDtypeStruct((b, s, D_MODEL), jnp.float32),
        compiler_params=pltpu.CompilerParams(
            dimension_semantics=("parallel", "parallel"), vmem_limit_bytes=VMEM_LIMIT_BYTES),
        name="tail",
    )(x, y, o, sa, sb, wfo, wao, wo, g2, wg, wu, wd, gf)


def _rope_tables(start, length):
    half = ROPE_DIM // 2
    lane = np.arange(LANES)
    inv = 1.0 / (ROPE_THETA ** (jnp.arange(0, ROPE_DIM, 2, dtype=jnp.float32) / ROPE_DIM))
    inv = inv[lane % half]
    sign = jnp.asarray(np.where((lane // half) % 2 == 0, -1.0, 1.0), jnp.float32)
    ang = jnp.arange(start, start + length, dtype=jnp.float32)[:, None] * inv[None, :]
    return jnp.cos(ang), jnp.sin(ang) * sign


def _prepare_weights(norm1_g, w_in, q_norm_g, kv_norm_g, w_uq, w_ukv, w_fourier_out, w_attn_out, w_o, norm2_g,
                     w_ffn_gate, w_ffn_up, w_ffn_down, final_norm_g):
    bf = jnp.bfloat16
    w = w_in[0]
    s_uf, s_cq, s_ckv, s_kr = FOURIER_DIM, FOURIER_DIM + Q_RANK, FOURIER_DIM + Q_RANK + KV_RANK, \
        FOURIER_DIM + Q_RANK + KV_RANK + ROPE_DIM
    win = jnp.concatenate(
        [w[:, :s_ckv], w[:, s_kr:], w[:, s_ckv:s_kr], jnp.zeros((D_MODEL, LANES - ROPE_DIM), w.dtype)], axis=1).astype(bf)
    wq = w_uq[0].reshape(Q_RANK, N_HEADS, QK_DIM)
    wuq = jnp.concatenate([wq[:, :, :NOPE_DIM].reshape(Q_RANK, -1), wq[:, :, NOPE_DIM:].reshape(Q_RANK, -1)], axis=1).astype(bf)
    wkv = w_ukv[0].reshape(KV_RANK, N_HEADS, NOPE_DIM + V_DIM)
    wkt = wkv[:, :, :NOPE_DIM].reshape(KV_RANK, -1).T.astype(bf)
    wv = wkv[:, :, NOPE_DIM:].reshape(KV_RANK, -1).astype(bf)
    row = lambda g: g.reshape(1, -1)
    return dict(
        g1=row(norm1_g[0]), win=win, qg=row(q_norm_g[0]), kvg=row(kv_norm_g[0]), wuq=wuq, wkt=wkt, wv=wv,
        wfo=w_fourier_out[0].astype(bf), wao=w_attn_out[0].astype(bf), wo=w_o[0].astype(bf), g2=row(norm2_g[0]),
        wg=w_ffn_gate[0].astype(bf), wu=w_ffn_up[0].astype(bf), wd=w_ffn_down[0].astype(bf), gf=row(final_norm_g))


def _project(p, x, cos2, sin2, tm):
    return _inproj(x, p["g1"], p["win"], p["qg"], p["kvg"], p["wuq"], p["wkt"], p["wv"], cos2, sin2, tm)


def _meta_projection(meta, p):
    meta_pad = jnp.zeros((1, LANES, D_MODEL), meta.dtype).at[0, :N_META].set(meta)
    uf_m, _, kt_m, v_m, _, _ = _project(p, meta_pad, *_rope_tables(0, LANES), LANES)
    return uf_m[0, :, :N_META], kt_m[:, :, 0], v_m


def _trunk(x, meta_proj, rope, p, tk, tq, tg, attn_step_scores, tm_tail):
    s = x.shape[1]
    attn_tiles = max(1, attn_step_scores // (tq * s))
    uf_m, kt_m, v_m = meta_proj
    cos2, sin2 = rope
    uf, q, kt, v, sa, sb = _project(p, x, cos2, sin2, tk)
    y = _fourier(uf, uf_m, *_fourier_tables(s))
    o = _attention(q, kt, v, kt_m, v_m, tq, attn_tiles, tg)
    return _tail(x, y, o, sa, sb, p["wfo"], p["wao"], p["wo"], p["g2"], p["wg"], p["wu"], p["wd"], p["gf"], tm_tail)


def kernel(x_prompt, x_sample, meta_tokens, norm1_g, w_in, q_norm_g, kv_norm_g, w_uq, w_ukv, w_fourier_out,
           w_attn_out, w_o, norm2_g, w_ffn_gate, w_ffn_up, w_ffn_down, final_norm_g):
    p = _prepare_weights(norm1_g, w_in, q_norm_g, kv_norm_g, w_uq, w_ukv, w_fourier_out, w_attn_out, w_o, norm2_g,
                         w_ffn_gate, w_ffn_up, w_ffn_down, final_norm_g)
    rope = _rope_tables(N_META, max(x_prompt.shape[1], x_sample.shape[1]))
    meta_proj = _meta_projection(meta_tokens, p)
    cfg = dict(tk=512, tq=1024, tg=2048, attn_step_scores=1 << 24, tm_tail=512)
    y_prompt = _trunk(x_prompt, meta_proj, rope, p, **cfg)
    y_sample = _trunk(x_sample, meta_proj, rope, p, **cfg)
    return (y_prompt, y_sample)
```

```python
import functools
import math

import jax
import jax.numpy as jnp
import numpy as np
from jax import lax
from jax.experimental import pallas as pl
from jax.experimental.pallas import tpu as pltpu

D_MODEL = 1024
N_META = 16
N_GROUPS = 4
GROUP_DIM = 128
FOURIER_DIM = N_GROUPS * GROUP_DIM
N_HEADS = 8
NOPE_DIM = 128
ROPE_DIM = 64
QK_DIM = NOPE_DIM + ROPE_DIM
V_DIM = 128
Q_RANK = 512
KV_RANK = 256
ATTN_DIM = N_HEADS * V_DIM
D_FF = 2816
ROPE_THETA = 10000.0
NORM_EPS = 1e-6
ATTN_SCALE = QK_DIM ** -0.5
LOG2_E = math.log2(math.e)
DFT_RADIX = 16

LANES = 128
F32_SUBLANES = 8
BF16_SUBLANES = 16
VMEM_LIMIT_BYTES = 56 * 1024 * 1024
DFT_K1_BATCH = 4
FF_CHUNK = 1024

_C_UF = 0
_C_CQ = _C_UF + FOURIER_DIM
_C_CKV = _C_CQ + Q_RANK
_C_GA = _C_CKV + KV_RANK
_C_GB = _C_GA + D_MODEL
_C_KR = _C_GB + D_MODEL
IN_COLS = _C_KR + LANES


def _rms(x, g):
    return x * lax.rsqrt(jnp.mean(x * x, axis=-1, keepdims=True) + NORM_EPS) * g


def _sigmoid(x):
    return 0.5 * jnp.tanh(0.5 * x) + 0.5


def _rope128(x, cos2, sin2):
    lane = lax.broadcasted_iota(jnp.int32, x.shape, 1)
    first_half = (lane % ROPE_DIM) < (ROPE_DIM // 2)
    partner = jnp.where(first_half, pltpu.roll(x, LANES - ROPE_DIM // 2, 1), pltpu.roll(x, ROPE_DIM // 2, 1))
    return x * cos2 + partner * sin2


def _inproj_kernel(x_ref, g1_ref, win_ref, qg_ref, kvg_ref, wuq_ref, wkt_ref, wv_ref, cos_ref, sin_ref,
                   uf_ref, q_ref, kt_ref, v_ref, sa_ref, sb_ref):
    bf = jnp.bfloat16
    f32 = jnp.float32
    h = _rms(x_ref[0], g1_ref[...]).astype(bf)

    def proj(c0, width):
        return jnp.dot(h, win_ref[:, c0:c0 + width], preferred_element_type=f32)

    cos2 = cos_ref[...]
    sin2 = sin_ref[...]

    ckv_raw = proj(_C_CKV, KV_RANK)
    kr_raw = proj(_C_KR, LANES)
    cq_raw = proj(_C_CQ, Q_RANK)

    ckv = _rms(ckv_raw, kvg_ref[...])
    ckv_t = ckv.T.astype(bf)
    k_t = jnp.dot(wkt_ref[...], ckv_t, preferred_element_type=f32)
    vals = jnp.dot(ckv.astype(bf), wv_ref[...], preferred_element_type=f32)
    cq = _rms(cq_raw, qg_ref[...]).astype(bf)
    q = jnp.dot(cq, wuq_ref[...], preferred_element_type=f32) * (ATTN_SCALE * LOG2_E)

    kr_t = _rope128(kr_raw, cos2, sin2).T[:ROPE_DIM].astype(bf)
    for hd in range(N_HEADS):
        kt_ref[0, hd, 0, :NOPE_DIM, :] = k_t[hd * NOPE_DIM:(hd + 1) * NOPE_DIM].astype(bf)
        kt_ref[0, hd, 0, NOPE_DIM:, :] = kr_t
        v_ref[0, hd] = vals[:, hd * V_DIM:(hd + 1) * V_DIM].astype(bf)

    rope0 = N_HEADS * NOPE_DIM
    for pair in range(N_HEADS // 2):
        qr = _rope128(q[:, rope0 + pair * LANES:rope0 + (pair + 1) * LANES], cos2, sin2).astype(bf)
        for sub in range(2):
            hd = 2 * pair + sub
            q_ref[0, hd, :, :NOPE_DIM] = q[:, hd * NOPE_DIM:(hd + 1) * NOPE_DIM].astype(bf)
            q_ref[0, hd, :, NOPE_DIM:] = qr[:, sub * ROPE_DIM:(sub + 1) * ROPE_DIM]

    uf = proj(_C_UF, FOURIER_DIM)
    for g in range(N_GROUPS):
        uf_ref[0, g] = uf[:, g * GROUP_DIM:(g + 1) * GROUP_DIM]
    sa_ref[0] = _sigmoid(proj(_C_GA, D_MODEL)).astype(bf)
    sb_ref[0] = _sigmoid(proj(_C_GB, D_MODEL)).astype(bf)


def _const_spec(shape):
    return pl.BlockSpec(shape, lambda *_: (0,) * len(shape), pipeline_mode=pl.Buffered(1))


def _inproj(x, g1, win, qg, kvg, wuq, wkt, wv, cos2, sin2, tm):
    b, s, _ = x.shape
    tok = lambda bi, si: (bi, si, 0)
    head = lambda bi, si: (bi, 0, si, 0)
    bf = jnp.bfloat16
    return pl.pallas_call(
        _inproj_kernel,
        grid=(b, s // tm),
        in_specs=[
            pl.BlockSpec((1, tm, D_MODEL), tok),
            _const_spec((1, D_MODEL)),
            _const_spec((D_MODEL, IN_COLS)),
            _const_spec((1, Q_RANK)),
            _const_spec((1, KV_RANK)),
            _const_spec((Q_RANK, N_HEADS * QK_DIM)),
            _const_spec((N_HEADS * NOPE_DIM, KV_RANK)),
            _const_spec((KV_RANK, N_HEADS * V_DIM)),
            pl.BlockSpec((tm, LANES), lambda bi, si: (si, 0)),
            pl.BlockSpec((tm, LANES), lambda bi, si: (si, 0)),
        ],
        out_specs=[
            pl.BlockSpec((1, N_GROUPS, tm, GROUP_DIM), head),
            pl.BlockSpec((1, N_HEADS, tm, QK_DIM), head),
            pl.BlockSpec((1, N_HEADS, 1, QK_DIM, tm), lambda bi, si: (bi, 0, si, 0, 0)),
            pl.BlockSpec((1, N_HEADS, tm, V_DIM), head),
            pl.BlockSpec((1, tm, D_MODEL), tok),
            pl.BlockSpec((1, tm, D_MODEL), tok),
        ],
        out_shape=[
            jax.ShapeDtypeStruct((b, N_GROUPS, s, GROUP_DIM), jnp.float32),
            jax.ShapeDtypeStruct((b, N_HEADS, s, QK_DIM), bf),
            jax.ShapeDtypeStruct((b, N_HEADS, s // tm, QK_DIM, tm), bf),
            jax.ShapeDtypeStruct((b, N_HEADS, s, V_DIM), bf),
            jax.ShapeDtypeStruct((b, s, D_MODEL), bf),
            jax.ShapeDtypeStruct((b, s, D_MODEL), bf),
        ],
        compiler_params=pltpu.CompilerParams(
            dimension_semantics=("parallel", "parallel"), vmem_limit_bytes=VMEM_LIMIT_BYTES),
        name="inproj",
    )(x, g1, win, qg, kvg, wuq, wkt, wv, cos2, sin2)


def _cneg(v):
    return None if v is None else -v


def _cadd(a, b):
    if a is None:
        return b
    if b is None:
        return a
    return a + b


def _csub(a, b):
    if b is None:
        return a
    if a is None:
        return -b
    return a - b


def _fft(xs):
    n = len(xs)
    if n == 1:
        return xs
    ev = _fft(xs[0::2])
    od = _fft(xs[1::2])
    out = [None] * n
    for k in range(n // 2):
        re, im = od[k]
        if k == 0:
            tr, ti = re, im
        elif 4 * k == n:
            tr, ti = im, _cneg(re)
        else:
            c = math.cos(2 * math.pi * k / n)
            s = math.sin(2 * math.pi * k / n)
            tr = _cadd(None if re is None else re * c, None if im is None else im * s)
            ti = _csub(None if im is None else im * c, None if re is None else re * s)
        out[k] = (_cadd(ev[k][0], tr), _cadd(ev[k][1], ti))
        out[k + n // 2] = (_csub(ev[k][0], tr), _csub(ev[k][1], ti))
    return out


def _dft16_real(rows):
    out = _fft([(r, None) for r in rows])
    zero = jnp.zeros_like(rows[0])
    out = [(zero if re is None else re, zero if im is None else im) for re, im in out]
    for k in range(DFT_RADIX // 2 + 1, DFT_RADIX):
        re, im = out[DFT_RADIX - k]
        out[k] = (re, -im)
    return out


def _fourier_kernel(n2, x0_ref, xm0_ref, xn_ref, xmn_ref, twc_ref, tws_ref, dmat_ref, cs_ref, y_ref,
                    xrun_ref, z_ref, z0_ref):
    m = n2 - 1
    bf = jnp.bfloat16
    zw = 2 * GROUP_DIM
    slot = pl.program_id(1) % 2

    rc = BF16_SUBLANES

    def run_start(j1):
        return j1 * n2 - N_META

    def vpu_head(x_ref, xm_ref, dst):
        xrun_ref[0:N_META, :] = xm_ref[0]
        xrun_ref[N_META:N_META + m, :] = x_ref[0, 0, 0:m, :]
        dc = _dft16_real([xrun_ref[0:1, :]] + [x_ref[0, 0, run_start(j1):run_start(j1) + 1, :]
                                              for j1 in range(1, DFT_RADIX)])
        for k1 in range(DFT_RADIX):
            z0_ref[dst, 0:1, k1 * zw:k1 * zw + GROUP_DIM] = dc[k1][0]
            z0_ref[dst, 0:1, k1 * zw + GROUP_DIM:(k1 + 1) * zw] = dc[k1][1]

    def vpu_chunk(x_ref, dst, r0):
        rows = [xrun_ref[r0 + 1:r0 + 1 + rc, :]]
        rows += [x_ref[0, 0, r0 + run_start(j1) + 1:r0 + run_start(j1) + 1 + rc, :] for j1 in range(1, DFT_RADIX)]
        a = _dft16_real(rows)
        for k1 in range(DFT_RADIX):
            ar, ai = a[k1]
            if k1 == 0:
                zr, zi = ar, ai
            else:
                tc = twc_ref[k1, r0:r0 + rc, :]
                ts = tws_ref[k1, r0:r0 + rc, :]
                zr = tc * ar + ts * ai
                zi = tc * ai - ts * ar
            z_ref[dst, r0:r0 + rc, k1 * zw:k1 * zw + GROUP_DIM] = zr.astype(bf)
            z_ref[dst, r0:r0 + rc, k1 * zw + GROUP_DIM:(k1 + 1) * zw] = zi.astype(bf)

    scale = 1.0 / math.sqrt(GROUP_DIM * DFT_RADIX * n2)

    def mxu_batch(k0):
        cols = slice(k0 * zw, (k0 + DFT_K1_BATCH) * zw)
        z = z_ref[slot, :, cols]
        swapped = jnp.concatenate(
            [blk for i in range(DFT_K1_BATCH)
             for blk in (z[:, i * zw + GROUP_DIM:(i + 1) * zw], -z[:, i * zw:i * zw + GROUP_DIM])], axis=1)
        rhs = jnp.concatenate([z, swapped], axis=0)
        p = jnp.dot(dmat_ref[...], rhs, preferred_element_type=jnp.float32)
        p = (p + z0_ref[slot, 0:1, cols]).astype(bf)
        stacked = jnp.concatenate([p[:, i * zw:(i + 1) * zw] for i in range(DFT_K1_BATCH)], axis=0)
        y = jnp.dot(stacked, cs_ref[...], preferred_element_type=jnp.float32) * scale
        for i in range(DFT_K1_BATCH):
            y_ref[0, 0, pl.ds(k0 + i, m, stride=DFT_RADIX), :] = y[i * m:(i + 1) * m]

    @pl.when((pl.program_id(0) == 0) & (pl.program_id(1) == 0))
    def _():
        vpu_head(x0_ref, xm0_ref, 0)
        for r0 in range(0, m, rc):
            vpu_chunk(x0_ref, 0, r0)

    vpu_head(xn_ref, xmn_ref, 1 - slot)
    for r0 in range(0, m, rc):
        vpu_chunk(xn_ref, 1 - slot, r0)
    for k0 in range(0, DFT_RADIX, DFT_K1_BATCH):
        mxu_batch(k0)


def _fourier(uf, uf_meta, twc, tws, dmat, cs):
    b, _, s, _ = uf.shape
    n2 = (s + N_META) // DFT_RADIX
    m = n2 - 1
    assert N_GROUPS % 2 == 0

    def next_step(bi, g):
        wrap = g + 1 == N_GROUPS
        return jnp.where(wrap, jnp.where(bi + 1 == b, 0, bi + 1), bi), jnp.where(wrap, 0, g + 1)

    def next_x(bi, g):
        bn, gn = next_step(bi, g)
        return bn, gn, 0, 0

    def next_meta(bi, g):
        return next_step(bi, g)[1], 0, 0

    return pl.pallas_call(
        functools.partial(_fourier_kernel, n2),
        grid=(b, N_GROUPS),
        in_specs=[
            _const_spec((1, 1, s, GROUP_DIM)),
            _const_spec((1, N_META, GROUP_DIM)),
            pl.BlockSpec((1, 1, s, GROUP_DIM), next_x),
            pl.BlockSpec((1, N_META, GROUP_DIM), next_meta),
            _const_spec((DFT_RADIX, m, GROUP_DIM)),
            _const_spec((DFT_RADIX, m, GROUP_DIM)),
            _const_spec((m, 2 * m)),
            _const_spec((2 * GROUP_DIM, GROUP_DIM)),
        ],
        out_specs=pl.BlockSpec((1, 1, s, GROUP_DIM), lambda bi, g: (bi, g, 0, 0)),
        out_shape=jax.ShapeDtypeStruct((b, N_GROUPS, s, GROUP_DIM), jnp.float32),
        scratch_shapes=[
            pltpu.VMEM((N_META + m, GROUP_DIM), jnp.float32),
            pltpu.VMEM((2, m, DFT_RADIX * 2 * GROUP_DIM), jnp.bfloat16),
            pltpu.VMEM((2, F32_SUBLANES, DFT_RADIX * 2 * GROUP_DIM), jnp.float32),
        ],
        compiler_params=pltpu.CompilerParams(
            dimension_semantics=("arbitrary", "arbitrary"), vmem_limit_bytes=VMEM_LIMIT_BYTES),
        name="fourier",
    )(uf, uf_meta, uf, uf_meta, twc, tws, dmat, cs)


def _fourier_tables(s):
    n2 = (s + N_META) // DFT_RADIX
    length = DFT_RADIX * n2
    j2 = np.arange(1, n2, dtype=np.int64)
    k1 = np.arange(DFT_RADIX, dtype=np.int64)
    phi = 2.0 * np.pi * ((k1[:, None] * j2[None, :]) % length) / length
    twc = jnp.broadcast_to(jnp.asarray(np.cos(phi), jnp.float32)[:, :, None], (DFT_RADIX, n2 - 1, GROUP_DIM))
    tws = jnp.broadcast_to(jnp.asarray(np.sin(phi), jnp.float32)[:, :, None], (DFT_RADIX, n2 - 1, GROUP_DIM))
    theta = 2.0 * np.pi * ((j2[:, None] * j2[None, :]) % n2) / n2
    dmat = jnp.asarray(np.concatenate([np.cos(theta), np.sin(theta)], axis=1), jnp.float32)
    c = np.arange(GROUP_DIM, dtype=np.int64)
    psi = 2.0 * np.pi * ((c[:, None] * c[None, :]) % GROUP_DIM) / GROUP_DIM
    cs = jnp.asarray(np.concatenate([np.cos(psi), np.sin(psi)], axis=0), jnp.float32)
    return twc, tws, dmat.astype(jnp.bfloat16), cs.astype(jnp.bfloat16)


def _attn_kernel(tq, q_ref, qn_ref, kt_ref, ktn_ref, v_ref, kmt_ref, vm_ref, o_ref, s_ref, pm_ref):
    bf = jnp.bfloat16
    f32 = jnp.float32
    tiles = o_ref.shape[1] // tq
    n, _, tk = kt_ref.shape[2:]
    group = s_ref.shape[2] // tk
    n_groups = n // group

    def produce(slot, qt, key_chunk):
        pmax = None
        for j in range(group):
            s = jnp.dot(qt, key_chunk(j), preferred_element_type=f32)
            s_ref[slot, :, j * tk:(j + 1) * tk] = s
            for l in range(tk // LANES):
                blk = s[:, l * LANES:(l + 1) * LANES]
                pmax = blk if pmax is None else jnp.maximum(pmax, blk)
        pm_ref[slot] = pmax

    def with_ones(vals):
        return jnp.concatenate([vals, jnp.ones_like(vals)], axis=1)

    def absorb(slot, g, m_i, acc, qt):
        pmax = pm_ref[slot]
        if g == 0:
            lane = lax.broadcasted_iota(jnp.int32, (tq, LANES), 1)
            s_meta = jnp.dot(qt, kmt_ref[0, 0], preferred_element_type=f32)
            s_meta = jnp.where(lane < N_META, s_meta, -jnp.inf)
            pmax = jnp.maximum(pmax, s_meta)
        m_new = jnp.max(pmax, axis=1, keepdims=True)
        if g > 0:
            m_new = jnp.maximum(m_i, m_new)
        p = jnp.exp2((s_ref[slot] - m_new).astype(bf))
        pv = jnp.dot(p, with_ones(v_ref[0, 0, g * group * tk:(g + 1) * group * tk, :]), preferred_element_type=f32)
        if g == 0:
            p_meta = jnp.exp2((s_meta - m_new).astype(bf))
            return m_new, pv + jnp.dot(p_meta, with_ones(vm_ref[0, 0]), preferred_element_type=f32)
        return m_new, jnp.exp2(m_i - m_new) * acc + pv

    first_step = (pl.program_id(0) == 0) & (pl.program_id(1) == 0) & (pl.program_id(2) == 0)

    @pl.when(first_step)
    def _():
        produce(0, q_ref[0, 0, 0:tq, :], lambda j: kt_ref[0, 0, j])

    for t in range(tiles):
        qt = q_ref[0, 0, t * tq:(t + 1) * tq, :]
        m = acc = None
        for g in range(n_groups):
            if g + 1 < n_groups:
                produce((g + 1) % 2, qt, lambda j, g=g: kt_ref[0, 0, (g + 1) * group + j])
            elif t + 1 < tiles:
                produce(0, q_ref[0, 0, (t + 1) * tq:(t + 2) * tq, :], lambda j: kt_ref[0, 0, j])
            else:
                produce(0, qn_ref[0, 0], lambda j: ktn_ref[0, 0, j])
            m, acc = absorb(g % 2, g, m, acc, qt)
        o_ref[0, t * tq:(t + 1) * tq, :] = (acc[:, :V_DIM] / acc[:, V_DIM:]).astype(bf)


def _attention(q, kt, v, kmt, vm, tq, tiles, tg):
    b, nh, s, _ = q.shape
    n, _, tk = kt.shape[2:]
    nq = s // (tiles * tq)
    assert n * tk == s and tg % tk == 0 and (s // tg) % 2 == 0

    def next_step(bi, h, qi):
        wrap_q = qi + 1 == nq
        wrap_h = wrap_q & (h + 1 == nh)
        bn = jnp.where(wrap_h, jnp.where(bi + 1 == b, 0, bi + 1), bi)
        hn = jnp.where(wrap_q, jnp.where(h + 1 == nh, 0, h + 1), h)
        return bn, hn, jnp.where(wrap_q, 0, qi + 1)

    def next_q(bi, h, qi):
        bn, hn, qn = next_step(bi, h, qi)
        return bn, hn, qn * tiles, 0

    def next_keys(bi, h, qi):
        bn, hn, _ = next_step(bi, h, qi)
        return bn, hn, 0, 0, 0

    return pl.pallas_call(
        functools.partial(_attn_kernel, tq),
        grid=(b, nh, nq),
        in_specs=[
            pl.BlockSpec((1, 1, tiles * tq, QK_DIM), lambda bi, h, qi: (bi, h, qi, 0)),
            pl.BlockSpec((1, 1, tq, QK_DIM), next_q),
            pl.BlockSpec((1, 1, n, QK_DIM, tk), lambda bi, h, qi: (bi, h, 0, 0, 0)),
            pl.BlockSpec((1, 1, tg // tk, QK_DIM, tk), next_keys),
            pl.BlockSpec((1, 1, s, V_DIM), lambda bi, h, qi: (bi, h, 0, 0)),
            pl.BlockSpec((1, 1, QK_DIM, LANES), lambda bi, h, qi: (0, h, 0, 0)),
            pl.BlockSpec((1, 1, LANES, V_DIM), lambda bi, h, qi: (0, h, 0, 0)),
        ],
        out_specs=pl.BlockSpec((1, tiles * tq, V_DIM), lambda bi, h, qi: (bi, qi, h)),
        out_shape=jax.ShapeDtypeStruct((b, s, ATTN_DIM), jnp.bfloat16),
        scratch_shapes=[pltpu.VMEM((2, tq, tg), jnp.float32), pltpu.VMEM((2, tq, LANES), jnp.float32)],
        compiler_params=pltpu.CompilerParams(
            dimension_semantics=("arbitrary", "arbitrary", "arbitrary"), vmem_limit_bytes=VMEM_LIMIT_BYTES),
        name="attention",
    )(q, q, kt, kt, v, kmt, vm)


def _tail_kernel(x_ref, y_ref, o_ref, sa_ref, sb_ref, wfo_ref, wao_ref, wo_ref, g2_ref, wg_ref, wu_ref, wd_ref,
                 gf_ref, out_ref):
    bf = jnp.bfloat16
    f32 = jnp.float32
    yf = jnp.concatenate([y_ref[0, g] for g in range(N_GROUPS)], axis=1).astype(bf)
    ya = jnp.dot(yf, wfo_ref[...], preferred_element_type=f32)
    yb = jnp.dot(o_ref[0], wao_ref[...], preferred_element_type=f32)
    merged = sa_ref[0].astype(f32) * ya + sb_ref[0].astype(f32) * yb
    x1 = x_ref[0] + jnp.dot(merged.astype(bf), wo_ref[...], preferred_element_type=f32)
    h2 = _rms(x1, g2_ref[...]).astype(bf)
    x2 = x1
    for c0 in range(0, D_FF, FF_CHUNK):
        c1 = min(c0 + FF_CHUNK, D_FF)
        gate = jnp.dot(h2, wg_ref[:, c0:c1], preferred_element_type=f32)
        up = jnp.dot(h2, wu_ref[:, c0:c1], preferred_element_type=f32)
        act = (gate * _sigmoid(gate) * up).astype(bf)
        x2 = x2 + jnp.dot(act, wd_ref[c0:c1, :], preferred_element_type=f32)
    out_ref[0] = _rms(x2, gf_ref[...])


def _tail(x, y, o, sa, sb, wfo, wao, wo, g2, wg, wu, wd, gf, tm):
    b, s, _ = x.shape
    tok = lambda bi, si: (bi, si, 0)
    const = lambda bi, si: (0, 0)

    def weight(shape):
        return pl.BlockSpec(shape, const, pipeline_mode=pl.Buffered(1))

    return pl.pallas_call(
        _tail_kernel,
        grid=(b, s // tm),
        in_specs=[
            pl.BlockSpec((1, tm, D_MODEL), tok),
            pl.BlockSpec((1, N_GROUPS, tm, GROUP_DIM), lambda bi, si: (bi, 0, si, 0)),
            pl.BlockSpec((1, tm, ATTN_DIM), tok),
            pl.BlockSpec((1, tm, D_MODEL), tok),
            pl.BlockSpec((1, tm, D_MODEL), tok),
            weight((FOURIER_DIM, D_MODEL)),
            weight((ATTN_DIM, D_MODEL)),
            weight((D_MODEL, D_MODEL)),
            pl.BlockSpec((1, D_MODEL), const),
            weight((D_MODEL, D_FF)),
            weight((D_MODEL, D_FF)),
            weight((D_FF, D_MODEL)),
            pl.BlockSpec((1, D_MODEL), const),
        ],
        out_specs=pl.BlockSpec((1, tm, D_MODEL), tok),
        out_shape=jax.ShapeDtypeStruct((b, s, D_MODEL), jnp.float32),
        compiler_params=pltpu.CompilerParams(
            dimension_semantics=("parallel", "parallel"), vmem_limit_bytes=VMEM_LIMIT_BYTES),
        name="tail",
    )(x, y, o, sa, sb, wfo, wao, wo, g2, wg, wu, wd, gf)


def _rope_tables(start, length):
    half = ROPE_DIM // 2
    lane = np.arange(LANES)
    inv = 1.0 / (ROPE_THETA ** (jnp.arange(0, ROPE_DIM, 2, dtype=jnp.float32) / ROPE_DIM))
    inv = inv[lane % half]
    sign = jnp.asarray(np.where((lane // half) % 2 == 0, -1.0, 1.0), jnp.float32)
    ang = jnp.arange(start, start + length, dtype=jnp.float32)[:, None] * inv[None, :]
    return jnp.cos(ang), jnp.sin(ang) * sign


def _prepare_weights(norm1_g, w_in, q_norm_g, kv_norm_g, w_uq, w_ukv, w_fourier_out, w_attn_out, w_o, norm2_g,
                     w_ffn_gate, w_ffn_up, w_ffn_down, final_norm_g):
    bf = jnp.bfloat16
    w = w_in[0]
    s_uf, s_cq, s_ckv, s_kr = FOURIER_DIM, FOURIER_DIM + Q_RANK, FOURIER_DIM + Q_RANK + KV_RANK, \
        FOURIER_DIM + Q_RANK + KV_RANK + ROPE_DIM
    win = jnp.concatenate(
        [w[:, :s_ckv], w[:, s_kr:], w[:, s_ckv:s_kr], jnp.zeros((D_MODEL, LANES - ROPE_DIM), w.dtype)], axis=1).astype(bf)
    wq = w_uq[0].reshape(Q_RANK, N_HEADS, QK_DIM)
    wuq = jnp.concatenate([wq[:, :, :NOPE_DIM].reshape(Q_RANK, -1), wq[:, :, NOPE_DIM:].reshape(Q_RANK, -1)], axis=1).astype(bf)
    wkv = w_ukv[0].reshape(KV_RANK, N_HEADS, NOPE_DIM + V_DIM)
    wkt = wkv[:, :, :NOPE_DIM].reshape(KV_RANK, -1).T.astype(bf)
    wv = wkv[:, :, NOPE_DIM:].reshape(KV_RANK, -1).astype(bf)
    row = lambda g: g.reshape(1, -1)
    return dict(
        g1=row(norm1_g[0]), win=win, qg=row(q_norm_g[0]), kvg=row(kv_norm_g[0]), wuq=wuq, wkt=wkt, wv=wv,
        wfo=w_fourier_out[0].astype(bf), wao=w_attn_out[0].astype(bf), wo=w_o[0].astype(bf), g2=row(norm2_g[0]),
        wg=w_ffn_gate[0].astype(bf), wu=w_ffn_up[0].astype(bf), wd=w_ffn_down[0].astype(bf), gf=row(final_norm_g))


def _project(p, x, cos2, sin2, tm):
    return _inproj(x, p["g1"], p["win"], p["qg"], p["kvg"], p["wuq"], p["wkt"], p["wv"], cos2, sin2, tm)


def _meta_projection(meta, p):
    meta_pad = jnp.zeros((1, LANES, D_MODEL), meta.dtype).at[0, :N_META].set(meta)
    uf_m, _, kt_m, v_m, _, _ = _project(p, meta_pad, *_rope_tables(0, LANES), LANES)
    return uf_m[0, :, :N_META], kt_m[:, :, 0], v_m


def _trunk(x, meta_proj, rope, p, tk, tq, attn_step_scores, attn_groups, tm_tail):
    s = x.shape[1]
    tg = s // attn_groups
    attn_tiles = max(1, attn_step_scores // (tq * s))
    uf_m, kt_m, v_m = meta_proj
    cos2, sin2 = rope
    uf, q, kt, v, sa, sb = _project(p, x, cos2, sin2, tk)
    y = _fourier(uf, uf_m, *_fourier_tables(s))
    o = _attention(q, kt, v, kt_m, v_m, tq, attn_tiles, tg)
    return _tail(x, y, o, sa, sb, p["wfo"], p["wao"], p["wo"], p["g2"], p["wg"], p["wu"], p["wd"], p["gf"], tm_tail)


def kernel(x_prompt, x_sample, meta_tokens, norm1_g, w_in, q_norm_g, kv_norm_g, w_uq, w_ukv, w_fourier_out,
           w_attn_out, w_o, norm2_g, w_ffn_gate, w_ffn_up, w_ffn_down, final_norm_g):
    p = _prepare_weights(norm1_g, w_in, q_norm_g, kv_norm_g, w_uq, w_ukv, w_fourier_out, w_attn_out, w_o, norm2_g,
                         w_ffn_gate, w_ffn_up, w_ffn_down, final_norm_g)
    rope = _rope_tables(N_META, max(x_prompt.shape[1], x_sample.shape[1]))
    meta_proj = _meta_projection(meta_tokens, p)
    cfg = dict(tk=512, tq=1024, attn_step_scores=1 << 24, attn_groups=4, tm_tail=512)
    y_prompt = _trunk(x_prompt, meta_proj, rope, p, **cfg)
    y_sample = _trunk(x_sample, meta_proj, rope, p, **cfg)
    return (y_prompt, y_sample)
```

```python
import functools
import math

import jax
import jax.numpy as jnp
import numpy as np
from jax import lax
from jax.experimental import pallas as pl
from jax.experimental.pallas import tpu as pltpu

D_MODEL = 1024
N_META = 16
N_GROUPS = 4
GROUP_DIM = 128
FOURIER_DIM = N_GROUPS * GROUP_DIM
N_HEADS = 8
NOPE_DIM = 128
ROPE_DIM = 64
QK_DIM = NOPE_DIM + ROPE_DIM
V_DIM = 128
Q_RANK = 512
KV_RANK = 256
ATTN_DIM = N_HEADS * V_DIM
D_FF = 2816
ROPE_THETA = 10000.0
NORM_EPS = 1e-6
ATTN_SCALE = QK_DIM ** -0.5
LOG2_E = math.log2(math.e)
DFT_RADIX = 16

LANES = 128
F32_SUBLANES = 8
BF16_SUBLANES = 16
VMEM_LIMIT_BYTES = 56 * 1024 * 1024
DFT_K1_BATCH = 4
FF_CHUNK = 1024

_C_UF = 0
_C_CQ = _C_UF + FOURIER_DIM
_C_CKV = _C_CQ + Q_RANK
_C_GA = _C_CKV + KV_RANK
_C_GB = _C_GA + D_MODEL
_C_KR = _C_GB + D_MODEL
IN_COLS = _C_KR + LANES


def _rms(x, g):
    return x * lax.rsqrt(jnp.mean(x * x, axis=-1, keepdims=True) + NORM_EPS) * g


def _sigmoid(x):
    return 0.5 * jnp.tanh(0.5 * x) + 0.5


def _rope128(x, cos2, sin2):
    lane = lax.broadcasted_iota(jnp.int32, x.shape, 1)
    first_half = (lane % ROPE_DIM) < (ROPE_DIM // 2)
    partner = jnp.where(first_half, pltpu.roll(x, LANES - ROPE_DIM // 2, 1), pltpu.roll(x, ROPE_DIM // 2, 1))
    return x * cos2 + partner * sin2


def _inproj_kernel(x_ref, g1_ref, win_ref, qg_ref, kvg_ref, wuq_ref, wkt_ref, wv_ref, cos_ref, sin_ref,
                   uf_ref, q_ref, kt_ref, v_ref, sa_ref, sb_ref):
    bf = jnp.bfloat16
    f32 = jnp.float32
    h = _rms(x_ref[0], g1_ref[...]).astype(bf)

    def proj(c0, width):
        return jnp.dot(h, win_ref[:, c0:c0 + width], preferred_element_type=f32)

    cos2 = cos_ref[...]
    sin2 = sin_ref[...]

    ckv_raw = proj(_C_CKV, KV_RANK)
    kr_raw = proj(_C_KR, LANES)
    cq_raw = proj(_C_CQ, Q_RANK)

    ckv = _rms(ckv_raw, kvg_ref[...])
    ckv_t = ckv.T.astype(bf)
    k_t = jnp.dot(wkt_ref[...], ckv_t, preferred_element_type=f32)
    vals = jnp.dot(ckv.astype(bf), wv_ref[...], preferred_element_type=f32)
    cq = _rms(cq_raw, qg_ref[...]).astype(bf)
    q = jnp.dot(cq, wuq_ref[...], preferred_element_type=f32) * (ATTN_SCALE * LOG2_E)

    kr_t = _rope128(kr_raw, cos2, sin2).T[:ROPE_DIM].astype(bf)
    for hd in range(N_HEADS):
        kt_ref[0, hd, 0, :NOPE_DIM, :] = k_t[hd * NOPE_DIM:(hd + 1) * NOPE_DIM].astype(bf)
        kt_ref[0, hd, 0, NOPE_DIM:, :] = kr_t
        v_ref[0, hd] = vals[:, hd * V_DIM:(hd + 1) * V_DIM].astype(bf)

    rope0 = N_HEADS * NOPE_DIM
    for pair in range(N_HEADS // 2):
        qr = _rope128(q[:, rope0 + pair * LANES:rope0 + (pair + 1) * LANES], cos2, sin2).astype(bf)
        for sub in range(2):
            hd = 2 * pair + sub
            q_ref[0, hd, :, :NOPE_DIM] = q[:, hd * NOPE_DIM:(hd + 1) * NOPE_DIM].astype(bf)
            q_ref[0, hd, :, NOPE_DIM:] = qr[:, sub * ROPE_DIM:(sub + 1) * ROPE_DIM]

    uf = proj(_C_UF, FOURIER_DIM)
    for g in range(N_GROUPS):
        uf_ref[0, g] = uf[:, g * GROUP_DIM:(g + 1) * GROUP_DIM]
    sa_ref[0] = _sigmoid(proj(_C_GA, D_MODEL).astype(bf))
    sb_ref[0] = _sigmoid(proj(_C_GB, D_MODEL).astype(bf))


def _const_spec(shape):
    return pl.BlockSpec(shape, lambda *_: (0,) * len(shape), pipeline_mode=pl.Buffered(1))


def _inproj(x, g1, win, qg, kvg, wuq, wkt, wv, cos2, sin2, tm):
    b, s, _ = x.shape
    tok = lambda bi, si: (bi, si, 0)
    head = lambda bi, si: (bi, 0, si, 0)
    bf = jnp.bfloat16
    return pl.pallas_call(
        _inproj_kernel,
        grid=(b, s // tm),
        in_specs=[
            pl.BlockSpec((1, tm, D_MODEL), tok),
            _const_spec((1, D_MODEL)),
            _const_spec((D_MODEL, IN_COLS)),
            _const_spec((1, Q_RANK)),
            _const_spec((1, KV_RANK)),
            _const_spec((Q_RANK, N_HEADS * QK_DIM)),
            _const_spec((N_HEADS * NOPE_DIM, KV_RANK)),
            _const_spec((KV_RANK, N_HEADS * V_DIM)),
            pl.BlockSpec((tm, LANES), lambda bi, si: (si, 0)),
            pl.BlockSpec((tm, LANES), lambda bi, si: (si, 0)),
        ],
        out_specs=[
            pl.BlockSpec((1, N_GROUPS, tm, GROUP_DIM), head),
            pl.BlockSpec((1, N_HEADS, tm, QK_DIM), head),
            pl.BlockSpec((1, N_HEADS, 1, QK_DIM, tm), lambda bi, si: (bi, 0, si, 0, 0)),
            pl.BlockSpec((1, N_HEADS, tm, V_DIM), head),
            pl.BlockSpec((1, tm, D_MODEL), tok),
            pl.BlockSpec((1, tm, D_MODEL), tok),
        ],
        out_shape=[
            jax.ShapeDtypeStruct((b, N_GROUPS, s, GROUP_DIM), jnp.float32),
            jax.ShapeDtypeStruct((b, N_HEADS, s, QK_DIM), bf),
            jax.ShapeDtypeStruct((b, N_HEADS, s // tm, QK_DIM, tm), bf),
            jax.ShapeDtypeStruct((b, N_HEADS, s, V_DIM), bf),
            jax.ShapeDtypeStruct((b, s, D_MODEL), bf),
            jax.ShapeDtypeStruct((b, s, D_MODEL), bf),
        ],
        compiler_params=pltpu.CompilerParams(
            dimension_semantics=("parallel", "parallel"), vmem_limit_bytes=VMEM_LIMIT_BYTES),
        name="inproj",
    )(x, g1, win, qg, kvg, wuq, wkt, wv, cos2, sin2)


def _cneg(v):
    return None if v is None else -v


def _cadd(a, b):
    if a is None:
        return b
    if b is None:
        return a
    return a + b


def _csub(a, b):
    if b is None:
        return a
    if a is None:
        return -b
    return a - b


def _fft(xs):
    n = len(xs)
    if n == 1:
        return xs
    ev = _fft(xs[0::2])
    od = _fft(xs[1::2])
    out = [None] * n
    for k in range(n // 2):
        re, im = od[k]
        if k == 0:
            tr, ti = re, im
        elif 4 * k == n:
            tr, ti = im, _cneg(re)
        else:
            c = math.cos(2 * math.pi * k / n)
            s = math.sin(2 * math.pi * k / n)
            tr = _cadd(None if re is None else re * c, None if im is None else im * s)
            ti = _csub(None if im is None else im * c, None if re is None else re * s)
        out[k] = (_cadd(ev[k][0], tr), _cadd(ev[k][1], ti))
        out[k + n // 2] = (_csub(ev[k][0], tr), _csub(ev[k][1], ti))
    return out


def _dft16_real(rows):
    out = _fft([(r, None) for r in rows])
    zero = jnp.zeros_like(rows[0])
    out = [(zero if re is None else re, zero if im is None else im) for re, im in out]
    for k in range(DFT_RADIX // 2 + 1, DFT_RADIX):
        re, im = out[DFT_RADIX - k]
        out[k] = (re, -im)
    return out


def _fourier_kernel(n2, x0_ref, xm0_ref, xn_ref, xmn_ref, twc_ref, tws_ref, dmat_ref, cs_ref, y_ref,
                    xrun_ref, z_ref, z0_ref):
    m = n2 - 1
    bf = jnp.bfloat16
    zw = 2 * GROUP_DIM
    slot = pl.program_id(1) % 2

    rc = BF16_SUBLANES

    def run_start(j1):
        return j1 * n2 - N_META

    def vpu_head(x_ref, xm_ref, dst):
        xrun_ref[0:N_META, :] = xm_ref[0]
        xrun_ref[N_META:N_META + m, :] = x_ref[0, 0, 0:m, :]
        dc = _dft16_real([xrun_ref[0:1, :]] + [x_ref[0, 0, run_start(j1):run_start(j1) + 1, :]
                                              for j1 in range(1, DFT_RADIX)])
        for k1 in range(DFT_RADIX):
            z0_ref[dst, 0:1, k1 * zw:k1 * zw + GROUP_DIM] = dc[k1][0]
            z0_ref[dst, 0:1, k1 * zw + GROUP_DIM:(k1 + 1) * zw] = dc[k1][1]

    def vpu_chunk(x_ref, dst, r0):
        rows = [xrun_ref[r0 + 1:r0 + 1 + rc, :]]
        rows += [x_ref[0, 0, r0 + run_start(j1) + 1:r0 + run_start(j1) + 1 + rc, :] for j1 in range(1, DFT_RADIX)]
        a = _dft16_real(rows)
        for k1 in range(DFT_RADIX):
            ar, ai = a[k1]
            if k1 == 0:
                zr, zi = ar, ai
            else:
                tc = twc_ref[k1, r0:r0 + rc, :]
                ts = tws_ref[k1, r0:r0 + rc, :]
                zr = tc * ar + ts * ai
                zi = tc * ai - ts * ar
            z_ref[dst, r0:r0 + rc, k1 * zw:k1 * zw + GROUP_DIM] = zr.astype(bf)
            z_ref[dst, r0:r0 + rc, k1 * zw + GROUP_DIM:(k1 + 1) * zw] = zi.astype(bf)

    scale = 1.0 / math.sqrt(GROUP_DIM * DFT_RADIX * n2)

    def mxu_batch(k0):
        cols = slice(k0 * zw, (k0 + DFT_K1_BATCH) * zw)
        z = z_ref[slot, :, cols]
        swapped = jnp.concatenate(
            [blk for i in range(DFT_K1_BATCH)
             for blk in (z[:, i * zw + GROUP_DIM:(i + 1) * zw], -z[:, i * zw:i * zw + GROUP_DIM])], axis=1)
        rhs = jnp.concatenate([z, swapped], axis=0)
        p = jnp.dot(dmat_ref[...], rhs, preferred_element_type=jnp.float32)
        p = (p + z0_ref[slot, 0:1, cols]).astype(bf)
        stacked = jnp.concatenate([p[:, i * zw:(i + 1) * zw] for i in range(DFT_K1_BATCH)], axis=0)
        y = jnp.dot(stacked, cs_ref[...], preferred_element_type=jnp.float32) * scale
        for i in range(DFT_K1_BATCH):
            y_ref[0, 0, pl.ds(k0 + i, m, stride=DFT_RADIX), :] = y[i * m:(i + 1) * m]

    @pl.when((pl.program_id(0) == 0) & (pl.program_id(1) == 0))
    def _():
        vpu_head(x0_ref, xm0_ref, 0)
        for r0 in range(0, m, rc):
            vpu_chunk(x0_ref, 0, r0)

    vpu_head(xn_ref, xmn_ref, 1 - slot)
    for r0 in range(0, m, rc):
        vpu_chunk(xn_ref, 1 - slot, r0)
    for k0 in range(0, DFT_RADIX, DFT_K1_BATCH):
        mxu_batch(k0)


def _fourier(uf, uf_meta, twc, tws, dmat, cs):
    b, _, s, _ = uf.shape
    n2 = (s + N_META) // DFT_RADIX
    m = n2 - 1
    assert N_GROUPS % 2 == 0

    def next_step(bi, g):
        wrap = g + 1 == N_GROUPS
        return jnp.where(wrap, jnp.where(bi + 1 == b, 0, bi + 1), bi), jnp.where(wrap, 0, g + 1)

    def next_x(bi, g):
        bn, gn = next_step(bi, g)
        return bn, gn, 0, 0

    def next_meta(bi, g):
        return next_step(bi, g)[1], 0, 0

    return pl.pallas_call(
        functools.partial(_fourier_kernel, n2),
        grid=(b, N_GROUPS),
        in_specs=[
            _const_spec((1, 1, s, GROUP_DIM)),
            _const_spec((1, N_META, GROUP_DIM)),
            pl.BlockSpec((1, 1, s, GROUP_DIM), next_x),
            pl.BlockSpec((1, N_META, GROUP_DIM), next_meta),
            _const_spec((DFT_RADIX, m, GROUP_DIM)),
            _const_spec((DFT_RADIX, m, GROUP_DIM)),
            _const_spec((m, 2 * m)),
            _const_spec((2 * GROUP_DIM, GROUP_DIM)),
        ],
        out_specs=pl.BlockSpec((1, 1, s, GROUP_DIM), lambda bi, g: (bi, g, 0, 0)),
        out_shape=jax.ShapeDtypeStruct((b, N_GROUPS, s, GROUP_DIM), jnp.float32),
        scratch_shapes=[
            pltpu.VMEM((N_META + m, GROUP_DIM), jnp.float32),
            pltpu.VMEM((2, m, DFT_RADIX * 2 * GROUP_DIM), jnp.bfloat16),
            pltpu.VMEM((2, F32_SUBLANES, DFT_RADIX * 2 * GROUP_DIM), jnp.float32),
        ],
        compiler_params=pltpu.CompilerParams(
            dimension_semantics=("arbitrary", "arbitrary"), vmem_limit_bytes=VMEM_LIMIT_BYTES),
        name="fourier",
    )(uf, uf_meta, uf, uf_meta, twc, tws, dmat, cs)


def _fourier_tables(s):
    n2 = (s + N_META) // DFT_RADIX
    length = DFT_RADIX * n2
    j2 = np.arange(1, n2, dtype=np.int64)
    k1 = np.arange(DFT_RADIX, dtype=np.int64)
    phi = 2.0 * np.pi * ((k1[:, None] * j2[None, :]) % length) / length
    twc = jnp.broadcast_to(jnp.asarray(np.cos(phi), jnp.float32)[:, :, None], (DFT_RADIX, n2 - 1, GROUP_DIM))
    tws = jnp.broadcast_to(jnp.asarray(np.sin(phi), jnp.float32)[:, :, None], (DFT_RADIX, n2 - 1, GROUP_DIM))
    theta = 2.0 * np.pi * ((j2[:, None] * j2[None, :]) % n2) / n2
    dmat = jnp.asarray(np.concatenate([np.cos(theta), np.sin(theta)], axis=1), jnp.float32)
    c = np.arange(GROUP_DIM, dtype=np.int64)
    psi = 2.0 * np.pi * ((c[:, None] * c[None, :]) % GROUP_DIM) / GROUP_DIM
    cs = jnp.asarray(np.concatenate([np.cos(psi), np.sin(psi)], axis=0), jnp.float32)
    return twc, tws, dmat.astype(jnp.bfloat16), cs.astype(jnp.bfloat16)


def _attn_kernel(tq, q_ref, qn_ref, kt_ref, ktn_ref, v_ref, kmt_ref, vm_ref, o_ref, s_ref, pm_ref):
    bf = jnp.bfloat16
    f32 = jnp.float32
    tiles = o_ref.shape[1] // tq
    n, _, tk = kt_ref.shape[2:]
    group = s_ref.shape[2] // tk
    n_groups = n // group

    def produce(slot, qt, key_chunk):
        pmax = None
        for j in range(group):
            s = jnp.dot(qt, key_chunk(j), preferred_element_type=f32)
            s_ref[slot, :, j * tk:(j + 1) * tk] = s
            for l in range(tk // LANES):
                blk = s[:, l * LANES:(l + 1) * LANES]
                pmax = blk if pmax is None else jnp.maximum(pmax, blk)
        pm_ref[slot] = pmax

    def with_ones(vals):
        return jnp.concatenate([vals, jnp.ones_like(vals)], axis=1)

    def absorb(slot, g, m_i, acc, qt):
        pmax = pm_ref[slot]
        if g == 0:
            lane = lax.broadcasted_iota(jnp.int32, (tq, LANES), 1)
            s_meta = jnp.dot(qt, kmt_ref[0, 0], preferred_element_type=f32)
            s_meta = jnp.where(lane < N_META, s_meta, -jnp.inf)
            pmax = jnp.maximum(pmax, s_meta)
        m_new = jnp.max(pmax, axis=1, keepdims=True)
        if g > 0:
            m_new = jnp.maximum(m_i, m_new)
        p = jnp.exp2((s_ref[slot] - m_new).astype(bf))
        pv = jnp.dot(p, with_ones(v_ref[0, 0, g * group * tk:(g + 1) * group * tk, :]), preferred_element_type=f32)
        if g == 0:
            p_meta = jnp.exp2((s_meta - m_new).astype(bf))
            return m_new, pv + jnp.dot(p_meta, with_ones(vm_ref[0, 0]), preferred_element_type=f32)
        return m_new, jnp.exp2(m_i - m_new) * acc + pv

    first_step = (pl.program_id(0) == 0) & (pl.program_id(1) == 0) & (pl.program_id(2) == 0)

    @pl.when(first_step)
    def _():
        produce(0, q_ref[0, 0, 0:tq, :], lambda j: kt_ref[0, 0, j])

    for t in range(tiles):
        qt = q_ref[0, 0, t * tq:(t + 1) * tq, :]
        m = acc = None
        for g in range(n_groups):
            if g + 1 < n_groups:
                produce((g + 1) % 2, qt, lambda j, g=g: kt_ref[0, 0, (g + 1) * group + j])
            elif t + 1 < tiles:
                produce(0, q_ref[0, 0, (t + 1) * tq:(t + 2) * tq, :], lambda j: kt_ref[0, 0, j])
            else:
                produce(0, qn_ref[0, 0], lambda j: ktn_ref[0, 0, j])
            m, acc = absorb(g % 2, g, m, acc, qt)
        o_ref[0, t * tq:(t + 1) * tq, :] = (acc[:, :V_DIM] / acc[:, V_DIM:]).astype(bf)


def _attention(q, kt, v, kmt, vm, tq, tiles, tg):
    b, nh, s, _ = q.shape
    n, _, tk = kt.shape[2:]
    nq = s // (tiles * tq)
    assert n * tk == s and tg % tk == 0 and (s // tg) % 2 == 0

    def next_step(bi, h, qi):
        wrap_q = qi + 1 == nq
        wrap_h = wrap_q & (h + 1 == nh)
        bn = jnp.where(wrap_h, jnp.where(bi + 1 == b, 0, bi + 1), bi)
        hn = jnp.where(wrap_q, jnp.where(h + 1 == nh, 0, h + 1), h)
        return bn, hn, jnp.where(wrap_q, 0, qi + 1)

    def next_q(bi, h, qi):
        bn, hn, qn = next_step(bi, h, qi)
        return bn, hn, qn * tiles, 0

    def next_keys(bi, h, qi):
        bn, hn, _ = next_step(bi, h, qi)
        return bn, hn, 0, 0, 0

    return pl.pallas_call(
        functools.partial(_attn_kernel, tq),
        grid=(b, nh, nq),
        in_specs=[
            pl.BlockSpec((1, 1, tiles * tq, QK_DIM), lambda bi, h, qi: (bi, h, qi, 0)),
            pl.BlockSpec((1, 1, tq, QK_DIM), next_q),
            pl.BlockSpec((1, 1, n, QK_DIM, tk), lambda bi, h, qi: (bi, h, 0, 0, 0)),
            pl.BlockSpec((1, 1, tg // tk, QK_DIM, tk), next_keys),
            pl.BlockSpec((1, 1, s, V_DIM), lambda bi, h, qi: (bi, h, 0, 0)),
            pl.BlockSpec((1, 1, QK_DIM, LANES), lambda bi, h, qi: (0, h, 0, 0)),
            pl.BlockSpec((1, 1, LANES, V_DIM), lambda bi, h, qi: (0, h, 0, 0)),
        ],
        out_specs=pl.BlockSpec((1, tiles * tq, V_DIM), lambda bi, h, qi: (bi, qi, h)),
        out_shape=jax.ShapeDtypeStruct((b, s, ATTN_DIM), jnp.bfloat16),
        scratch_shapes=[pltpu.VMEM((2, tq, tg), jnp.float32), pltpu.VMEM((2, tq, LANES), jnp.float32)],
        compiler_params=pltpu.CompilerParams(
            dimension_semantics=("arbitrary", "arbitrary", "arbitrary"), vmem_limit_bytes=VMEM_LIMIT_BYTES),
        name="attention",
    )(q, q, kt, kt, v, kmt, vm)


def _tail_kernel(x_ref, y_ref, o_ref, sa_ref, sb_ref, wfo_ref, wao_ref, wo_ref, g2_ref, wg_ref, wu_ref, wd_ref,
                 gf_ref, out_ref):
    bf = jnp.bfloat16
    f32 = jnp.float32
    yf = jnp.concatenate([y_ref[0, g] for g in range(N_GROUPS)], axis=1).astype(bf)
    ya = jnp.dot(yf, wfo_ref[...], preferred_element_type=f32)
    yb = jnp.dot(o_ref[0], wao_ref[...], preferred_element_type=f32)
    merged = sa_ref[0].astype(f32) * ya + sb_ref[0].astype(f32) * yb
    x1 = x_ref[0] + jnp.dot(merged.astype(bf), wo_ref[...], preferred_element_type=f32)
    h2 = _rms(x1, g2_ref[...]).astype(bf)
    x2 = x1
    for c0 in range(0, D_FF, FF_CHUNK):
        c1 = min(c0 + FF_CHUNK, D_FF)
        gate = jnp.dot(h2, wg_ref[:, c0:c1], preferred_element_type=f32)
        up = jnp.dot(h2, wu_ref[:, c0:c1], preferred_element_type=f32)
        act = (gate * _sigmoid(gate) * up).astype(bf)
        x2 = x2 + jnp.dot(act, wd_ref[c0:c1, :], preferred_element_type=f32)
    out_ref[0] = _rms(x2, gf_ref[...])


def _tail(x, y, o, sa, sb, wfo, wao, wo, g2, wg, wu, wd, gf, tm):
    b, s, _ = x.shape
    tok = lambda bi, si: (bi, si, 0)
    const = lambda bi, si: (0, 0)

    def weight(shape):
        return pl.BlockSpec(shape, const, pipeline_mode=pl.Buffered(1))

    return pl.pallas_call(
        _tail_kernel,
        grid=(b, s // tm),
        in_specs=[
            pl.BlockSpec((1, tm, D_MODEL), tok),
            pl.BlockSpec((1, N_GROUPS, tm, GROUP_DIM), lambda bi, si: (bi, 0, si, 0)),
            pl.BlockSpec((1, tm, ATTN_DIM), tok),
            pl.BlockSpec((1, tm, D_MODEL), tok),
            pl.BlockSpec((1, tm, D_MODEL), tok),
            weight((FOURIER_DIM, D_MODEL)),
            weight((ATTN_DIM, D_MODEL)),
            weight((D_MODEL, D_MODEL)),
            pl.BlockSpec((1, D_MODEL), const),
            weight((D_MODEL, D_FF)),
            weight((D_MODEL, D_FF)),
            weight((D_FF, D_MODEL)),
            pl.BlockSpec((1, D_MODEL), const),
        ],
        out_specs=pl.BlockSpec((1, tm, D_MODEL), tok),
        out_shape=jax.ShapeDtypeStruct((b, s, D_MODEL), jnp.float32),
        compiler_params=pltpu.CompilerParams(
            dimension_semantics=("parallel", "parallel"), vmem_limit_bytes=VMEM_LIMIT_BYTES),
        name="tail",
    )(x, y, o, sa, sb, wfo, wao, wo, g2, wg, wu, wd, gf)


def _rope_tables(start, length):
    half = ROPE_DIM // 2
    lane = np.arange(LANES)
    inv = 1.0 / (ROPE_THETA ** (jnp.arange(0, ROPE_DIM, 2, dtype=jnp.float32) / ROPE_DIM))
    inv = inv[lane % half]
    sign = jnp.asarray(np.where((lane // half) % 2 == 0, -1.0, 1.0), jnp.float32)
    ang = jnp.arange(start, start + length, dtype=jnp.float32)[:, None] * inv[None, :]
    return jnp.cos(ang), jnp.sin(ang) * sign


def _prepare_weights(norm1_g, w_in, q_norm_g, kv_norm_g, w_uq, w_ukv, w_fourier_out, w_attn_out, w_o, norm2_g,
                     w_ffn_gate, w_ffn_up, w_ffn_down, final_norm_g):
    bf = jnp.bfloat16
    w = w_in[0]
    s_uf, s_cq, s_ckv, s_kr = FOURIER_DIM, FOURIER_DIM + Q_RANK, FOURIER_DIM + Q_RANK + KV_RANK, \
        FOURIER_DIM + Q_RANK + KV_RANK + ROPE_DIM
    win = jnp.concatenate(
        [w[:, :s_ckv], w[:, s_kr:], w[:, s_ckv:s_kr], jnp.zeros((D_MODEL, LANES - ROPE_DIM), w.dtype)], axis=1).astype(bf)
    wq = w_uq[0].reshape(Q_RANK, N_HEADS, QK_DIM)
    wuq = jnp.concatenate([wq[:, :, :NOPE_DIM].reshape(Q_RANK, -1), wq[:, :, NOPE_DIM:].reshape(Q_RANK, -1)], axis=1).astype(bf)
    wkv = w_ukv[0].reshape(KV_RANK, N_HEADS, NOPE_DIM + V_DIM)
    wkt = wkv[:, :, :NOPE_DIM].reshape(KV_RANK, -1).T.astype(bf)
    wv = wkv[:, :, NOPE_DIM:].reshape(KV_RANK, -1).astype(bf)
    row = lambda g: g.reshape(1, -1)
    return dict(
        g1=row(norm1_g[0]), win=win, qg=row(q_norm_g[0]), kvg=row(kv_norm_g[0]), wuq=wuq, wkt=wkt, wv=wv,
        wfo=w_fourier_out[0].astype(bf), wao=w_attn_out[0].astype(bf), wo=w_o[0].astype(bf), g2=row(norm2_g[0]),
        wg=w_ffn_gate[0].astype(bf), wu=w_ffn_up[0].astype(bf), wd=w_ffn_down[0].astype(bf), gf=row(final_norm_g))


def _project(p, x, cos2, sin2, tm):
    return _inproj(x, p["g1"], p["win"], p["qg"], p["kvg"], p["wuq"], p["wkt"], p["wv"], cos2, sin2, tm)


def _meta_projection(meta, p):
    meta_pad = jnp.zeros((1, LANES, D_MODEL), meta.dtype).at[0, :N_META].set(meta)
    uf_m, _, kt_m, v_m, _, _ = _project(p, meta_pad, *_rope_tables(0, LANES), LANES)
    return uf_m[0, :, :N_META], kt_m[:, :, 0], v_m


def _trunk(x, meta_proj, rope, p, tk, tq, attn_step_scores, attn_groups, tm_tail):
    s = x.shape[1]
    tg = s // attn_groups
    attn_tiles = max(1, attn_step_scores // (tq * s))
    uf_m, kt_m, v_m = meta_proj
    cos2, sin2 = rope
    uf, q, kt, v, sa, sb = _project(p, x, cos2, sin2, tk)
    y = _fourier(uf, uf_m, *_fourier_tables(s))
    o = _attention(q, kt, v, kt_m, v_m, tq, attn_tiles, tg)
    return _tail(x, y, o, sa, sb, p["wfo"], p["wao"], p["wo"], p["g2"], p["wg"], p["wu"], p["wd"], p["gf"], tm_tail)


def kernel(x_prompt, x_sample, meta_tokens, norm1_g, w_in, q_norm_g, kv_norm_g, w_uq, w_ukv, w_fourier_out,
           w_attn_out, w_o, norm2_g, w_ffn_gate, w_ffn_up, w_ffn_down, final_norm_g):
    p = _prepare_weights(norm1_g, w_in, q_norm_g, kv_norm_g, w_uq, w_ukv, w_fourier_out, w_attn_out, w_o, norm2_g,
                         w_ffn_gate, w_ffn_up, w_ffn_down, final_norm_g)
    rope = _rope_tables(N_META, max(x_prompt.shape[1], x_sample.shape[1]))
    meta_proj = _meta_projection(meta_tokens, p)
    cfg = dict(tk=512, tq=1024, attn_step_scores=1 << 24, attn_groups=4, tm_tail=512)
    y_prompt = _trunk(x_prompt, meta_proj, rope, p, **cfg)
    y_sample = _trunk(x_sample, meta_proj, rope, p, **cfg)
    return (y_prompt, y_sample)
```

```python
import functools
import math

import jax
import jax.numpy as jnp
import numpy as np
from jax import lax
from jax.experimental import pallas as pl
from jax.experimental.pallas import tpu as pltpu

D_MODEL = 1024
N_META = 16
N_GROUPS = 4
GROUP_DIM = 128
FOURIER_DIM = N_GROUPS * GROUP_DIM
N_HEADS = 8
NOPE_DIM = 128
ROPE_DIM = 64
QK_DIM = NOPE_DIM + ROPE_DIM
V_DIM = 128
Q_RANK = 512
KV_RANK = 256
ATTN_DIM = N_HEADS * V_DIM
D_FF = 2816
ROPE_THETA = 10000.0
NORM_EPS = 1e-6
ATTN_SCALE = QK_DIM ** -0.5
LOG2_E = math.log2(math.e)
DFT_RADIX = 16

LANES = 128
F32_SUBLANES = 8
BF16_SUBLANES = 16
VMEM_LIMIT_BYTES = 60 * 1024 * 1024
DFT_K1_BATCH = 4
Y_PITCH = DFT_RADIX + 1
FF_CHUNK = 1024

_C_UF = 0
_C_CQ = _C_UF + FOURIER_DIM
_C_CKV = _C_CQ + Q_RANK
_C_GA = _C_CKV + KV_RANK
_C_GB = _C_GA + D_MODEL
_C_KR = _C_GB + D_MODEL
IN_COLS = _C_KR + LANES


def _rms(x, g):
    return x * lax.rsqrt(jnp.mean(x * x, axis=-1, keepdims=True) + NORM_EPS) * g


def _sigmoid(x):
    return 0.5 * jnp.tanh(0.5 * x) + 0.5


def _rope128(x, cos2, sin2):
    lane = lax.broadcasted_iota(jnp.int32, x.shape, 1)
    first_half = (lane % ROPE_DIM) < (ROPE_DIM // 2)
    partner = jnp.where(first_half, pltpu.roll(x, LANES - ROPE_DIM // 2, 1), pltpu.roll(x, ROPE_DIM // 2, 1))
    return x * cos2 + partner * sin2


def _inproj_kernel(x_ref, g1_ref, win_ref, qg_ref, kvg_ref, wuq_ref, wkt_ref, wv_ref, cos_ref, sin_ref,
                   uf_ref, q_ref, kt_ref, v_ref, sa_ref, sb_ref):
    bf = jnp.bfloat16
    f32 = jnp.float32
    h = _rms(x_ref[0], g1_ref[...]).astype(bf)

    def proj(c0, width):
        return jnp.dot(h, win_ref[:, c0:c0 + width], preferred_element_type=f32)

    cos2 = cos_ref[...]
    sin2 = sin_ref[...]

    ckv_raw = proj(_C_CKV, KV_RANK)
    kr_raw = proj(_C_KR, LANES)
    cq_raw = proj(_C_CQ, Q_RANK)

    ckv = _rms(ckv_raw, kvg_ref[...])
    ckv_t = ckv.T.astype(bf)
    k_t = jnp.dot(wkt_ref[...], ckv_t, preferred_element_type=f32)
    vals = jnp.dot(ckv.astype(bf), wv_ref[...], preferred_element_type=f32)
    cq = _rms(cq_raw, qg_ref[...]).astype(bf)
    q = jnp.dot(cq, wuq_ref[...], preferred_element_type=f32) * (ATTN_SCALE * LOG2_E)

    kr_t = _rope128(kr_raw, cos2, sin2).T[:ROPE_DIM].astype(bf)
    for hd in range(N_HEADS):
        kt_ref[0, hd, 0, :NOPE_DIM, :] = k_t[hd * NOPE_DIM:(hd + 1) * NOPE_DIM].astype(bf)
        kt_ref[0, hd, 0, NOPE_DIM:, :] = kr_t
        v_ref[0, hd] = vals[:, hd * V_DIM:(hd + 1) * V_DIM].astype(bf)

    rope0 = N_HEADS * NOPE_DIM
    for pair in range(N_HEADS // 2):
        qr = _rope128(q[:, rope0 + pair * LANES:rope0 + (pair + 1) * LANES], cos2, sin2).astype(bf)
        for sub in range(2):
            hd = 2 * pair + sub
            q_ref[0, hd, :, :NOPE_DIM] = q[:, hd * NOPE_DIM:(hd + 1) * NOPE_DIM].astype(bf)
            q_ref[0, hd, :, NOPE_DIM:] = qr[:, sub * ROPE_DIM:(sub + 1) * ROPE_DIM]

    uf = proj(_C_UF, FOURIER_DIM)
    for g in range(N_GROUPS):
        uf_ref[0, g] = uf[:, g * GROUP_DIM:(g + 1) * GROUP_DIM]
    sa_ref[0] = _sigmoid(proj(_C_GA, D_MODEL)).astype(bf)
    sb_ref[0] = _sigmoid(proj(_C_GB, D_MODEL)).astype(bf)


def _const_spec(shape):
    return pl.BlockSpec(shape, lambda *_: (0,) * len(shape), pipeline_mode=pl.Buffered(1))


def _inproj(x, g1, win, qg, kvg, wuq, wkt, wv, cos2, sin2, tm):
    b, s, _ = x.shape
    tok = lambda bi, si: (bi, si, 0)
    head = lambda bi, si: (bi, 0, si, 0)
    bf = jnp.bfloat16
    return pl.pallas_call(
        _inproj_kernel,
        grid=(b, s // tm),
        in_specs=[
            pl.BlockSpec((1, tm, D_MODEL), tok),
            _const_spec((1, D_MODEL)),
            _const_spec((D_MODEL, IN_COLS)),
            _const_spec((1, Q_RANK)),
            _const_spec((1, KV_RANK)),
            _const_spec((Q_RANK, N_HEADS * QK_DIM)),
            _const_spec((N_HEADS * NOPE_DIM, KV_RANK)),
            _const_spec((KV_RANK, N_HEADS * V_DIM)),
            pl.BlockSpec((tm, LANES), lambda bi, si: (si, 0)),
            pl.BlockSpec((tm, LANES), lambda bi, si: (si, 0)),
        ],
        out_specs=[
            pl.BlockSpec((1, N_GROUPS, tm, GROUP_DIM), head),
            pl.BlockSpec((1, N_HEADS, tm, QK_DIM), head),
            pl.BlockSpec((1, N_HEADS, 1, QK_DIM, tm), lambda bi, si: (bi, 0, si, 0, 0)),
            pl.BlockSpec((1, N_HEADS, tm, V_DIM), head),
            pl.BlockSpec((1, tm, D_MODEL), tok),
            pl.BlockSpec((1, tm, D_MODEL), tok),
        ],
        out_shape=[
            jax.ShapeDtypeStruct((b, N_GROUPS, s, GROUP_DIM), jnp.float32),
            jax.ShapeDtypeStruct((b, N_HEADS, s, QK_DIM), bf),
            jax.ShapeDtypeStruct((b, N_HEADS, s // tm, QK_DIM, tm), bf),
            jax.ShapeDtypeStruct((b, N_HEADS, s, V_DIM), bf),
            jax.ShapeDtypeStruct((b, s, D_MODEL), bf),
            jax.ShapeDtypeStruct((b, s, D_MODEL), bf),
        ],
        compiler_params=pltpu.CompilerParams(
            dimension_semantics=("parallel", "parallel"), vmem_limit_bytes=VMEM_LIMIT_BYTES),
        name="inproj",
    )(x, g1, win, qg, kvg, wuq, wkt, wv, cos2, sin2)


def _cneg(v):
    return None if v is None else -v


def _cadd(a, b):
    if a is None:
        return b
    if b is None:
        return a
    return a + b


def _csub(a, b):
    if b is None:
        return a
    if a is None:
        return -b
    return a - b


def _fft(xs):
    n = len(xs)
    if n == 1:
        return xs
    ev = _fft(xs[0::2])
    od = _fft(xs[1::2])
    out = [None] * n
    for k in range(n // 2):
        re, im = od[k]
        if k == 0:
            tr, ti = re, im
        elif 4 * k == n:
            tr, ti = im, _cneg(re)
        else:
            c = math.cos(2 * math.pi * k / n)
            s = math.sin(2 * math.pi * k / n)
            tr = _cadd(None if re is None else re * c, None if im is None else im * s)
            ti = _csub(None if im is None else im * c, None if re is None else re * s)
        out[k] = (_cadd(ev[k][0], tr), _cadd(ev[k][1], ti))
        out[k + n // 2] = (_csub(ev[k][0], tr), _csub(ev[k][1], ti))
    return out


def _dft16_real(rows):
    out = _fft([(r, None) for r in rows])
    zero = jnp.zeros_like(rows[0])
    out = [(zero if re is None else re, zero if im is None else im) for re, im in out]
    for k in range(DFT_RADIX // 2 + 1, DFT_RADIX):
        re, im = out[DFT_RADIX - k]
        out[k] = (re, -im)
    return out


def _fourier_kernel(n2, x0_ref, xm0_ref, xn_ref, xmn_ref, twc_ref, tws_ref, dmat_ref, cs_ref, y_ref,
                    xrun_ref, z_ref, z0_ref, ystage_ref):
    m = n2 - 1
    bf = jnp.bfloat16
    zw = 2 * GROUP_DIM
    slot = pl.program_id(1) % 2

    rc = BF16_SUBLANES

    def run_start(j1):
        return j1 * n2 - N_META

    def vpu_head(x_ref, xm_ref, dst):
        xrun_ref[0:N_META, :] = xm_ref[0]
        xrun_ref[N_META:N_META + m, :] = x_ref[0, 0, 0:m, :]
        dc = _dft16_real([xrun_ref[0:1, :]] + [x_ref[0, 0, run_start(j1):run_start(j1) + 1, :]
                                              for j1 in range(1, DFT_RADIX)])
        for k1 in range(DFT_RADIX):
            z0_ref[dst, 0:1, k1 * zw:k1 * zw + GROUP_DIM] = dc[k1][0]
            z0_ref[dst, 0:1, k1 * zw + GROUP_DIM:(k1 + 1) * zw] = dc[k1][1]

    def vpu_chunk(x_ref, dst, r0):
        rows = [xrun_ref[r0 + 1:r0 + 1 + rc, :]]
        rows += [x_ref[0, 0, r0 + run_start(j1) + 1:r0 + run_start(j1) + 1 + rc, :] for j1 in range(1, DFT_RADIX)]
        a = _dft16_real(rows)
        for k1 in range(DFT_RADIX):
            ar, ai = a[k1]
            if k1 == 0:
                zr, zi = ar, ai
            else:
                tc = twc_ref[k1, r0:r0 + rc, :]
                ts = tws_ref[k1, r0:r0 + rc, :]
                zr = tc * ar + ts * ai
                zi = tc * ai - ts * ar
            z_ref[dst, r0:r0 + rc, k1 * zw:k1 * zw + GROUP_DIM] = zr.astype(bf)
            z_ref[dst, r0:r0 + rc, k1 * zw + GROUP_DIM:(k1 + 1) * zw] = zi.astype(bf)

    scale = 1.0 / math.sqrt(GROUP_DIM * DFT_RADIX * n2)

    def mxu_batch(k0):
        cols = slice(k0 * zw, (k0 + DFT_K1_BATCH) * zw)
        z = z_ref[slot, :, cols]
        swapped = jnp.concatenate(
            [blk for i in range(DFT_K1_BATCH)
             for blk in (z[:, i * zw + GROUP_DIM:(i + 1) * zw], -z[:, i * zw:i * zw + GROUP_DIM])], axis=1)
        rhs = jnp.concatenate([z, swapped], axis=0)
        p = jnp.dot(dmat_ref[...], rhs, preferred_element_type=jnp.float32)
        p = (p + z0_ref[slot, 0:1, cols]).astype(bf)
        stacked = jnp.concatenate([p[:, i * zw:(i + 1) * zw] for i in range(DFT_K1_BATCH)], axis=0)
        y = jnp.dot(stacked, cs_ref[...], preferred_element_type=jnp.float32) * scale
        for i in range(DFT_K1_BATCH):
            ystage_ref[pl.ds(k0 + i, m, stride=Y_PITCH), :] = y[i * m:(i + 1) * m]

    @pl.when((pl.program_id(0) == 0) & (pl.program_id(1) == 0))
    def _():
        vpu_head(x0_ref, xm0_ref, 0)
        for r0 in range(0, m, rc):
            vpu_chunk(x0_ref, 0, r0)

    vpu_head(xn_ref, xmn_ref, 1 - slot)
    for r0 in range(0, m, rc):
        vpu_chunk(xn_ref, 1 - slot, r0)
    for k0 in range(0, DFT_RADIX, DFT_K1_BATCH):
        mxu_batch(k0)
    for r in range(m):
        y_ref[0, 0, r * DFT_RADIX:(r + 1) * DFT_RADIX, :] = ystage_ref[r * Y_PITCH:r * Y_PITCH + DFT_RADIX, :]


def _fourier(uf, uf_meta, twc, tws, dmat, cs):
    b, _, s, _ = uf.shape
    n2 = (s + N_META) // DFT_RADIX
    m = n2 - 1
    assert N_GROUPS % 2 == 0

    def next_step(bi, g):
        wrap = g + 1 == N_GROUPS
        return jnp.where(wrap, jnp.where(bi + 1 == b, 0, bi + 1), bi), jnp.where(wrap, 0, g + 1)

    def next_x(bi, g):
        bn, gn = next_step(bi, g)
        return bn, gn, 0, 0

    def next_meta(bi, g):
        return next_step(bi, g)[1], 0, 0

    return pl.pallas_call(
        functools.partial(_fourier_kernel, n2),
        grid=(b, N_GROUPS),
        in_specs=[
            _const_spec((1, 1, s, GROUP_DIM)),
            _const_spec((1, N_META, GROUP_DIM)),
            pl.BlockSpec((1, 1, s, GROUP_DIM), next_x),
            pl.BlockSpec((1, N_META, GROUP_DIM), next_meta),
            _const_spec((DFT_RADIX, m, GROUP_DIM)),
            _const_spec((DFT_RADIX, m, GROUP_DIM)),
            _const_spec((m, 2 * m)),
            _const_spec((2 * GROUP_DIM, GROUP_DIM)),
        ],
        out_specs=pl.BlockSpec((1, 1, s, GROUP_DIM), lambda bi, g: (bi, g, 0, 0)),
        out_shape=jax.ShapeDtypeStruct((b, N_GROUPS, s, GROUP_DIM), jnp.float32),
        scratch_shapes=[
            pltpu.VMEM((N_META + m, GROUP_DIM), jnp.float32),
            pltpu.VMEM((2, m, DFT_RADIX * 2 * GROUP_DIM), jnp.bfloat16),
            pltpu.VMEM((2, F32_SUBLANES, DFT_RADIX * 2 * GROUP_DIM), jnp.float32),
            pltpu.VMEM((m * Y_PITCH, GROUP_DIM), jnp.float32),
        ],
        compiler_params=pltpu.CompilerParams(
            dimension_semantics=("arbitrary", "arbitrary"), vmem_limit_bytes=VMEM_LIMIT_BYTES),
        name="fourier",
    )(uf, uf_meta, uf, uf_meta, twc, tws, dmat, cs)


def _fourier_tables(s):
    n2 = (s + N_META) // DFT_RADIX
    length = DFT_RADIX * n2
    j2 = np.arange(1, n2, dtype=np.int64)
    k1 = np.arange(DFT_RADIX, dtype=np.int64)
    phi = 2.0 * np.pi * ((k1[:, None] * j2[None, :]) % length) / length
    twc = jnp.broadcast_to(jnp.asarray(np.cos(phi), jnp.float32)[:, :, None], (DFT_RADIX, n2 - 1, GROUP_DIM))
    tws = jnp.broadcast_to(jnp.asarray(np.sin(phi), jnp.float32)[:, :, None], (DFT_RADIX, n2 - 1, GROUP_DIM))
    theta = 2.0 * np.pi * ((j2[:, None] * j2[None, :]) % n2) / n2
    dmat = jnp.asarray(np.concatenate([np.cos(theta), np.sin(theta)], axis=1), jnp.float32)
    c = np.arange(GROUP_DIM, dtype=np.int64)
    psi = 2.0 * np.pi * ((c[:, None] * c[None, :]) % GROUP_DIM) / GROUP_DIM
    cs = jnp.asarray(np.concatenate([np.cos(psi), np.sin(psi)], axis=0), jnp.float32)
    return twc, tws, dmat.astype(jnp.bfloat16), cs.astype(jnp.bfloat16)


def _attn_kernel(tq, q_ref, qn_ref, kt_ref, ktn_ref, v_ref, kmt_ref, vm_ref, o_ref, s_ref, pm_ref):
    bf = jnp.bfloat16
    f32 = jnp.float32
    tiles = o_ref.shape[1] // tq
    n, _, tk = kt_ref.shape[2:]
    group = s_ref.shape[2] // tk
    n_groups = n // group

    def produce(slot, qt, key_chunk):
        pmax = None
        for j in range(group):
            s = jnp.dot(qt, key_chunk(j), preferred_element_type=f32)
            s_ref[slot, :, j * tk:(j + 1) * tk] = s
            for l in range(tk // LANES):
                blk = s[:, l * LANES:(l + 1) * LANES]
                pmax = blk if pmax is None else jnp.maximum(pmax, blk)
        pm_ref[slot] = pmax

    def with_ones(vals):
        return jnp.concatenate([vals, jnp.ones_like(vals)], axis=1)

    def absorb(slot, g, m_i, acc, qt):
        pmax = pm_ref[slot]
        if g == 0:
            lane = lax.broadcasted_iota(jnp.int32, (tq, LANES), 1)
            s_meta = jnp.dot(qt, kmt_ref[0, 0], preferred_element_type=f32)
            s_meta = jnp.where(lane < N_META, s_meta, -jnp.inf)
            pmax = jnp.maximum(pmax, s_meta)
        m_new = jnp.max(pmax, axis=1, keepdims=True)
        if g > 0:
            m_new = jnp.maximum(m_i, m_new)
        p = jnp.exp2((s_ref[slot] - m_new).astype(bf))
        pv = jnp.dot(p, with_ones(v_ref[0, 0, g * group * tk:(g + 1) * group * tk, :]), preferred_element_type=f32)
        if g == 0:
            p_meta = jnp.exp2((s_meta - m_new).astype(bf))
            return m_new, pv + jnp.dot(p_meta, with_ones(vm_ref[0, 0]), preferred_element_type=f32)
        return m_new, jnp.exp2(m_i - m_new) * acc + pv

    first_step = (pl.program_id(0) == 0) & (pl.program_id(1) == 0) & (pl.program_id(2) == 0)

    @pl.when(first_step)
    def _():
        produce(0, q_ref[0, 0, 0:tq, :], lambda j: kt_ref[0, 0, j])

    for t in range(tiles):
        qt = q_ref[0, 0, t * tq:(t + 1) * tq, :]
        m = acc = None
        for g in range(n_groups):
            if g + 1 < n_groups:
                produce((g + 1) % 2, qt, lambda j, g=g: kt_ref[0, 0, (g + 1) * group + j])
            elif t + 1 < tiles:
                produce(0, q_ref[0, 0, (t + 1) * tq:(t + 2) * tq, :], lambda j: kt_ref[0, 0, j])
            else:
                produce(0, qn_ref[0, 0], lambda j: ktn_ref[0, 0, j])
            m, acc = absorb(g % 2, g, m, acc, qt)
        o_ref[0, t * tq:(t + 1) * tq, :] = (acc[:, :V_DIM] / acc[:, V_DIM:]).astype(bf)


def _attention(q, kt, v, kmt, vm, tq, tiles, tg):
    b, nh, s, _ = q.shape
    n, _, tk = kt.shape[2:]
    nq = s // (tiles * tq)
    assert n * tk == s and tg % tk == 0 and (s // tg) % 2 == 0

    def next_step(bi, h, qi):
        wrap_q = qi + 1 == nq
        wrap_h = wrap_q & (h + 1 == nh)
        bn = jnp.where(wrap_h, jnp.where(bi + 1 == b, 0, bi + 1), bi)
        hn = jnp.where(wrap_q, jnp.where(h + 1 == nh, 0, h + 1), h)
        return bn, hn, jnp.where(wrap_q, 0, qi + 1)

    def next_q(bi, h, qi):
        bn, hn, qn = next_step(bi, h, qi)
        return bn, hn, qn * tiles, 0

    def next_keys(bi, h, qi):
        bn, hn, _ = next_step(bi, h, qi)
        return bn, hn, 0, 0, 0

    return pl.pallas_call(
        functools.partial(_attn_kernel, tq),
        grid=(b, nh, nq),
        in_specs=[
            pl.BlockSpec((1, 1, tiles * tq, QK_DIM), lambda bi, h, qi: (bi, h, qi, 0)),
            pl.BlockSpec((1, 1, tq, QK_DIM), next_q),
            pl.BlockSpec((1, 1, n, QK_DIM, tk), lambda bi, h, qi: (bi, h, 0, 0, 0)),
            pl.BlockSpec((1, 1, tg // tk, QK_DIM, tk), next_keys),
            pl.BlockSpec((1, 1, s, V_DIM), lambda bi, h, qi: (bi, h, 0, 0)),
            pl.BlockSpec((1, 1, QK_DIM, LANES), lambda bi, h, qi: (0, h, 0, 0)),
            pl.BlockSpec((1, 1, LANES, V_DIM), lambda bi, h, qi: (0, h, 0, 0)),
        ],
        out_specs=pl.BlockSpec((1, tiles * tq, V_DIM), lambda bi, h, qi: (bi, qi, h)),
        out_shape=jax.ShapeDtypeStruct((b, s, ATTN_DIM), jnp.bfloat16),
        scratch_shapes=[pltpu.VMEM((2, tq, tg), jnp.float32), pltpu.VMEM((2, tq, LANES), jnp.float32)],
        compiler_params=pltpu.CompilerParams(
            dimension_semantics=("arbitrary", "arbitrary", "arbitrary"), vmem_limit_bytes=VMEM_LIMIT_BYTES),
        name="attention",
    )(q, q, kt, kt, v, kmt, vm)


def _tail_kernel(x_ref, y_ref, o_ref, sa_ref, sb_ref, wfo_ref, wao_ref, wo_ref, g2_ref, wg_ref, wu_ref, wd_ref,
                 gf_ref, out_ref):
    bf = jnp.bfloat16
    f32 = jnp.float32
    yf = jnp.concatenate([y_ref[0, g] for g in range(N_GROUPS)], axis=1).astype(bf)
    ya = jnp.dot(yf, wfo_ref[...], preferred_element_type=f32)
    yb = jnp.dot(o_ref[0], wao_ref[...], preferred_element_type=f32)
    merged = sa_ref[0].astype(f32) * ya + sb_ref[0].astype(f32) * yb
    x1 = x_ref[0] + jnp.dot(merged.astype(bf), wo_ref[...], preferred_element_type=f32)
    h2 = _rms(x1, g2_ref[...]).astype(bf)
    x2 = x1
    for c0 in range(0, D_FF, FF_CHUNK):
        c1 = min(c0 + FF_CHUNK, D_FF)
        gate = jnp.dot(h2, wg_ref[:, c0:c1], preferred_element_type=f32)
        up = jnp.dot(h2, wu_ref[:, c0:c1], preferred_element_type=f32)
        act = (gate * _sigmoid(gate) * up).astype(bf)
        x2 = x2 + jnp.dot(act, wd_ref[c0:c1, :], preferred_element_type=f32)
    out_ref[0] = _rms(x2, gf_ref[...])


def _tail(x, y, o, sa, sb, wfo, wao, wo, g2, wg, wu, wd, gf, tm):
    b, s, _ = x.shape
    tok = lambda bi, si: (bi, si, 0)
    const = lambda bi, si: (0, 0)

    def weight(shape):
        return pl.BlockSpec(shape, const, pipeline_mode=pl.Buffered(1))

    return pl.pallas_call(
        _tail_kernel,
        grid=(b, s // tm),
        in_specs=[
            pl.BlockSpec((1, tm, D_MODEL), tok),
            pl.BlockSpec((1, N_GROUPS, tm, GROUP_DIM), lambda bi, si: (bi, 0, si, 0)),
            pl.BlockSpec((1, tm, ATTN_DIM), tok),
            pl.BlockSpec((1, tm, D_MODEL), tok),
            pl.BlockSpec((1, tm, D_MODEL), tok),
            weight((FOURIER_DIM, D_MODEL)),
            weight((ATTN_DIM, D_MODEL)),
            weight((D_MODEL, D_MODEL)),
            pl.BlockSpec((1, D_MODEL), const),
            weight((D_MODEL, D_FF)),
            weight((D_MODEL, D_FF)),
            weight((D_FF, D_MODEL)),
            pl.BlockSpec((1, D_MODEL), const),
        ],
        out_specs=pl.BlockSpec((1, tm, D_MODEL), tok),
        out_shape=jax.ShapeDtypeStruct((b, s, D_MODEL), jnp.float32),
        compiler_params=pltpu.CompilerParams(
            dimension_semantics=("parallel", "parallel"), vmem_limit_bytes=VMEM_LIMIT_BYTES),
        name="tail",
    )(x, y, o, sa, sb, wfo, wao, wo, g2, wg, wu, wd, gf)


def _rope_tables(start, length):
    half = ROPE_DIM // 2
    lane = np.arange(LANES)
    inv = 1.0 / (ROPE_THETA ** (jnp.arange(0, ROPE_DIM, 2, dtype=jnp.float32) / ROPE_DIM))
    inv = inv[lane % half]
    sign = jnp.asarray(np.where((lane // half) % 2 == 0, -1.0, 1.0), jnp.float32)
    ang = jnp.arange(start, start + length, dtype=jnp.float32)[:, None] * inv[None, :]
    return jnp.cos(ang), jnp.sin(ang) * sign


def _prepare_weights(norm1_g, w_in, q_norm_g, kv_norm_g, w_uq, w_ukv, w_fourier_out, w_attn_out, w_o, norm2_g,
                     w_ffn_gate, w_ffn_up, w_ffn_down, final_norm_g):
    bf = jnp.bfloat16
    w = w_in[0]
    s_uf, s_cq, s_ckv, s_kr = FOURIER_DIM, FOURIER_DIM + Q_RANK, FOURIER_DIM + Q_RANK + KV_RANK, \
        FOURIER_DIM + Q_RANK + KV_RANK + ROPE_DIM
    win = jnp.concatenate(
        [w[:, :s_ckv], w[:, s_kr:], w[:, s_ckv:s_kr], jnp.zeros((D_MODEL, LANES - ROPE_DIM), w.dtype)], axis=1).astype(bf)
    wq = w_uq[0].reshape(Q_RANK, N_HEADS, QK_DIM)
    wuq = jnp.concatenate([wq[:, :, :NOPE_DIM].reshape(Q_RANK, -1), wq[:, :, NOPE_DIM:].reshape(Q_RANK, -1)], axis=1).astype(bf)
    wkv = w_ukv[0].reshape(KV_RANK, N_HEADS, NOPE_DIM + V_DIM)
    wkt = wkv[:, :, :NOPE_DIM].reshape(KV_RANK, -1).T.astype(bf)
    wv = wkv[:, :, NOPE_DIM:].reshape(KV_RANK, -1).astype(bf)
    row = lambda g: g.reshape(1, -1)
    return dict(
        g1=row(norm1_g[0]), win=win, qg=row(q_norm_g[0]), kvg=row(kv_norm_g[0]), wuq=wuq, wkt=wkt, wv=wv,
        wfo=w_fourier_out[0].astype(bf), wao=w_attn_out[0].astype(bf), wo=w_o[0].astype(bf), g2=row(norm2_g[0]),
        wg=w_ffn_gate[0].astype(bf), wu=w_ffn_up[0].astype(bf), wd=w_ffn_down[0].astype(bf), gf=row(final_norm_g))


def _project(p, x, cos2, sin2, tm):
    return _inproj(x, p["g1"], p["win"], p["qg"], p["kvg"], p["wuq"], p["wkt"], p["wv"], cos2, sin2, tm)


def _meta_projection(meta, p):
    meta_pad = jnp.zeros((1, LANES, D_MODEL), meta.dtype).at[0, :N_META].set(meta)
    uf_m, _, kt_m, v_m, _, _ = _project(p, meta_pad, *_rope_tables(0, LANES), LANES)
    return uf_m[0, :, :N_META], kt_m[:, :, 0], v_m


def _trunk(x, meta_proj, rope, p, tk, tq, attn_step_scores, attn_groups, tm_tail):
    s = x.shape[1]
    tg = s // attn_groups
    attn_tiles = max(1, attn_step_scores // (tq * s))
    uf_m, kt_m, v_m = meta_proj
    cos2, sin2 = rope
    uf, q, kt, v, sa, sb = _project(p, x, cos2, sin2, tk)
    y = _fourier(uf, uf_m, *_fourier_tables(s))
    o = _attention(q, kt, v, kt_m, v_m, tq, attn_tiles, tg)
    return _tail(x, y, o, sa, sb, p["wfo"], p["wao"], p["wo"], p["g2"], p["wg"], p["wu"], p["wd"], p["gf"], tm_tail)


def kernel(x_prompt, x_sample, meta_tokens, norm1_g, w_in, q_norm_g, kv_norm_g, w_uq, w_ukv, w_fourier_out,
           w_attn_out, w_o, norm2_g, w_ffn_gate, w_ffn_up, w_ffn_down, final_norm_g):
    p = _prepare_weights(norm1_g, w_in, q_norm_g, kv_norm_g, w_uq, w_ukv, w_fourier_out, w_attn_out, w_o, norm2_g,
                         w_ffn_gate, w_ffn_up, w_ffn_down, final_norm_g)
    rope = _rope_tables(N_META, max(x_prompt.shape[1], x_sample.shape[1]))
    meta_proj = _meta_projection(meta_tokens, p)
    cfg = dict(tk=512, tq=1024, attn_step_scores=1 << 24, attn_groups=4, tm_tail=512)
    y_prompt = _trunk(x_prompt, meta_proj, rope, p, **cfg)
    y_sample = _trunk(x_sample, meta_proj, rope, p, **cfg)
    return (y_prompt, y_sample)
```

```python
import functools
import math

import jax
import jax.numpy as jnp
import numpy as np
from jax import lax
from jax.experimental import pallas as pl
from jax.experimental.pallas import tpu as pltpu

D_MODEL = 1024
N_META = 16
N_GROUPS = 4
GROUP_DIM = 128
FOURIER_DIM = N_GROUPS * GROUP_DIM
N_HEADS = 8
NOPE_DIM = 128
ROPE_DIM = 64
QK_DIM = NOPE_DIM + ROPE_DIM
V_DIM = 128
Q_RANK = 512
KV_RANK = 256
ATTN_DIM = N_HEADS * V_DIM
D_FF = 2816
ROPE_THETA = 10000.0
NORM_EPS = 1e-6
ATTN_SCALE = QK_DIM ** -0.5
LOG2_E = math.log2(math.e)
DFT_RADIX = 16

LANES = 128
F32_SUBLANES = 8
BF16_SUBLANES = 16
VMEM_LIMIT_BYTES = 60 * 1024 * 1024
DFT_K1_BATCH = 4
Y_PITCH = DFT_RADIX + 1
MXU_TILE = 256
FF_CHUNK = 4 * MXU_TILE

_C_UF = 0
_C_CQ = _C_UF + FOURIER_DIM
_C_CKV = _C_CQ + Q_RANK
_C_GA = _C_CKV + KV_RANK
_C_GB = _C_GA + D_MODEL
_C_KR = _C_GB + D_MODEL
IN_COLS = _C_KR + LANES


def _rms(x, g):
    return x * lax.rsqrt(jnp.mean(x * x, axis=-1, keepdims=True) + NORM_EPS) * g


def _sigmoid(x):
    return 0.5 * jnp.tanh(0.5 * x) + 0.5


def _rope128(x, cos2, sin2):
    lane = lax.broadcasted_iota(jnp.int32, x.shape, 1)
    first_half = (lane % ROPE_DIM) < (ROPE_DIM // 2)
    partner = jnp.where(first_half, pltpu.roll(x, LANES - ROPE_DIM // 2, 1), pltpu.roll(x, ROPE_DIM // 2, 1))
    return x * cos2 + partner * sin2


def _inproj_kernel(x_ref, g1_ref, win_ref, qg_ref, kvg_ref, wuq_ref, wkt_ref, wv_ref, cos_ref, sin_ref,
                   uf_ref, q_ref, kt_ref, v_ref, sa_ref, sb_ref):
    bf = jnp.bfloat16
    f32 = jnp.float32
    h = _rms(x_ref[0], g1_ref[...]).astype(bf)

    def proj(c0, width):
        return jnp.dot(h, win_ref[:, c0:c0 + width], preferred_element_type=f32)

    cos2 = cos_ref[...]
    sin2 = sin_ref[...]

    ckv_raw = proj(_C_CKV, KV_RANK)
    kr_raw = proj(_C_KR, LANES)
    cq_raw = proj(_C_CQ, Q_RANK)

    ckv = _rms(ckv_raw, kvg_ref[...])
    ckv_t = ckv.T.astype(bf)
    k_t = jnp.dot(wkt_ref[...], ckv_t, preferred_element_type=f32)
    vals = jnp.dot(ckv.astype(bf), wv_ref[...], preferred_element_type=f32)
    cq = _rms(cq_raw, qg_ref[...]).astype(bf)
    q = jnp.dot(cq, wuq_ref[...], preferred_element_type=f32) * (ATTN_SCALE * LOG2_E)

    kr_t = _rope128(kr_raw, cos2, sin2).T[:ROPE_DIM].astype(bf)
    for hd in range(N_HEADS):
        kt_ref[0, hd, 0, :NOPE_DIM, :] = k_t[hd * NOPE_DIM:(hd + 1) * NOPE_DIM].astype(bf)
        kt_ref[0, hd, 0, NOPE_DIM:, :] = kr_t
        v_ref[0, hd] = vals[:, hd * V_DIM:(hd + 1) * V_DIM].astype(bf)

    rope0 = N_HEADS * NOPE_DIM
    for pair in range(N_HEADS // 2):
        qr = _rope128(q[:, rope0 + pair * LANES:rope0 + (pair + 1) * LANES], cos2, sin2).astype(bf)
        for sub in range(2):
            hd = 2 * pair + sub
            q_ref[0, hd, :, :NOPE_DIM] = q[:, hd * NOPE_DIM:(hd + 1) * NOPE_DIM].astype(bf)
            q_ref[0, hd, :, NOPE_DIM:] = qr[:, sub * ROPE_DIM:(sub + 1) * ROPE_DIM]

    uf = proj(_C_UF, FOURIER_DIM)
    for g in range(N_GROUPS):
        uf_ref[0, g] = uf[:, g * GROUP_DIM:(g + 1) * GROUP_DIM]
    sa_ref[0] = _sigmoid(proj(_C_GA, D_MODEL)).astype(bf)
    sb_ref[0] = _sigmoid(proj(_C_GB, D_MODEL)).astype(bf)


def _const_spec(shape):
    return pl.BlockSpec(shape, lambda *_: (0,) * len(shape), pipeline_mode=pl.Buffered(1))


def _inproj(x, g1, win, qg, kvg, wuq, wkt, wv, cos2, sin2, tm):
    b, s, _ = x.shape
    tok = lambda bi, si: (bi, si, 0)
    head = lambda bi, si: (bi, 0, si, 0)
    bf = jnp.bfloat16
    return pl.pallas_call(
        _inproj_kernel,
        grid=(b, s // tm),
        in_specs=[
            pl.BlockSpec((1, tm, D_MODEL), tok),
            _const_spec((1, D_MODEL)),
            _const_spec((D_MODEL, IN_COLS)),
            _const_spec((1, Q_RANK)),
            _const_spec((1, KV_RANK)),
            _const_spec((Q_RANK, N_HEADS * QK_DIM)),
            _const_spec((N_HEADS * NOPE_DIM, KV_RANK)),
            _const_spec((KV_RANK, N_HEADS * V_DIM)),
            pl.BlockSpec((tm, LANES), lambda bi, si: (si, 0)),
            pl.BlockSpec((tm, LANES), lambda bi, si: (si, 0)),
        ],
        out_specs=[
            pl.BlockSpec((1, N_GROUPS, tm, GROUP_DIM), head),
            pl.BlockSpec((1, N_HEADS, tm, QK_DIM), head),
            pl.BlockSpec((1, N_HEADS, 1, QK_DIM, tm), lambda bi, si: (bi, 0, si, 0, 0)),
            pl.BlockSpec((1, N_HEADS, tm, V_DIM), head),
            pl.BlockSpec((1, tm, D_MODEL), tok),
            pl.BlockSpec((1, tm, D_MODEL), tok),
        ],
        out_shape=[
            jax.ShapeDtypeStruct((b, N_GROUPS, s, GROUP_DIM), jnp.float32),
            jax.ShapeDtypeStruct((b, N_HEADS, s, QK_DIM), bf),
            jax.ShapeDtypeStruct((b, N_HEADS, s // tm, QK_DIM, tm), bf),
            jax.ShapeDtypeStruct((b, N_HEADS, s, V_DIM), bf),
            jax.ShapeDtypeStruct((b, s, D_MODEL), bf),
            jax.ShapeDtypeStruct((b, s, D_MODEL), bf),
        ],
        compiler_params=pltpu.CompilerParams(
            dimension_semantics=("parallel", "parallel"), vmem_limit_bytes=VMEM_LIMIT_BYTES),
        name="inproj",
    )(x, g1, win, qg, kvg, wuq, wkt, wv, cos2, sin2)


def _cneg(v):
    return None if v is None else -v


def _cadd(a, b):
    if a is None:
        return b
    if b is None:
        return a
    return a + b


def _csub(a, b):
    if b is None:
        return a
    if a is None:
        return -b
    return a - b


def _fft(xs):
    n = len(xs)
    if n == 1:
        return xs
    ev = _fft(xs[0::2])
    od = _fft(xs[1::2])
    out = [None] * n
    for k in range(n // 2):
        re, im = od[k]
        if k == 0:
            tr, ti = re, im
        elif 4 * k == n:
            tr, ti = im, _cneg(re)
        else:
            c = math.cos(2 * math.pi * k / n)
            s = math.sin(2 * math.pi * k / n)
            tr = _cadd(None if re is None else re * c, None if im is None else im * s)
            ti = _csub(None if im is None else im * c, None if re is None else re * s)
        out[k] = (_cadd(ev[k][0], tr), _cadd(ev[k][1], ti))
        out[k + n // 2] = (_csub(ev[k][0], tr), _csub(ev[k][1], ti))
    return out


def _dft16_real(rows):
    out = _fft([(r, None) for r in rows])
    zero = jnp.zeros_like(rows[0])
    out = [(zero if re is None else re, zero if im is None else im) for re, im in out]
    for k in range(DFT_RADIX // 2 + 1, DFT_RADIX):
        re, im = out[DFT_RADIX - k]
        out[k] = (re, -im)
    return out


def _fourier_kernel(n2, x0_ref, xm0_ref, xn_ref, xmn_ref, twc_ref, tws_ref, dmat_ref, cs_ref, y_ref,
                    xrun_ref, z_ref, z0_ref, ystage_ref):
    m = n2 - 1
    bf = jnp.bfloat16
    zw = 2 * GROUP_DIM
    slot = pl.program_id(1) % 2

    rc = BF16_SUBLANES

    def run_start(j1):
        return j1 * n2 - N_META

    def vpu_head(x_ref, xm_ref, dst):
        xrun_ref[0:N_META, :] = xm_ref[0]
        xrun_ref[N_META:N_META + m, :] = x_ref[0, 0, 0:m, :]
        dc = _dft16_real([xrun_ref[0:1, :]] + [x_ref[0, 0, run_start(j1):run_start(j1) + 1, :]
                                              for j1 in range(1, DFT_RADIX)])
        for k1 in range(DFT_RADIX):
            z0_ref[dst, 0:1, k1 * zw:k1 * zw + GROUP_DIM] = dc[k1][0]
            z0_ref[dst, 0:1, k1 * zw + GROUP_DIM:(k1 + 1) * zw] = dc[k1][1]

    def vpu_chunk(x_ref, dst, r0):
        rows = [xrun_ref[r0 + 1:r0 + 1 + rc, :]]
        rows += [x_ref[0, 0, r0 + run_start(j1) + 1:r0 + run_start(j1) + 1 + rc, :] for j1 in range(1, DFT_RADIX)]
        a = _dft16_real(rows)
        for k1 in range(DFT_RADIX):
            ar, ai = a[k1]
            if k1 == 0:
                zr, zi = ar, ai
            else:
                tc = twc_ref[k1, r0:r0 + rc, :]
                ts = tws_ref[k1, r0:r0 + rc, :]
                zr = tc * ar + ts * ai
                zi = tc * ai - ts * ar
            z_ref[dst, r0:r0 + rc, k1 * zw:k1 * zw + GROUP_DIM] = zr.astype(bf)
            z_ref[dst, r0:r0 + rc, k1 * zw + GROUP_DIM:(k1 + 1) * zw] = zi.astype(bf)

    scale = 1.0 / math.sqrt(GROUP_DIM * DFT_RADIX * n2)

    def mxu_batch(k0):
        cols = slice(k0 * zw, (k0 + DFT_K1_BATCH) * zw)
        z = z_ref[slot, :, cols]
        swapped = jnp.concatenate(
            [blk for i in range(DFT_K1_BATCH)
             for blk in (z[:, i * zw + GROUP_DIM:(i + 1) * zw], -z[:, i * zw:i * zw + GROUP_DIM])], axis=1)
        rhs = jnp.concatenate([z, swapped], axis=0)
        p = jnp.dot(dmat_ref[...], rhs, preferred_element_type=jnp.float32)
        p = (p + z0_ref[slot, 0:1, cols]).astype(bf)
        stacked = jnp.concatenate([p[:, i * zw:(i + 1) * zw] for i in range(DFT_K1_BATCH)], axis=0)
        y = jnp.dot(stacked, cs_ref[...], preferred_element_type=jnp.float32) * scale
        for i in range(DFT_K1_BATCH):
            ystage_ref[pl.ds(k0 + i, m, stride=Y_PITCH), :] = y[i * m:(i + 1) * m]

    @pl.when((pl.program_id(0) == 0) & (pl.program_id(1) == 0))
    def _():
        vpu_head(x0_ref, xm0_ref, 0)
        for r0 in range(0, m, rc):
            vpu_chunk(x0_ref, 0, r0)

    vpu_head(xn_ref, xmn_ref, 1 - slot)
    for r0 in range(0, m, rc):
        vpu_chunk(xn_ref, 1 - slot, r0)
    for k0 in range(0, DFT_RADIX, DFT_K1_BATCH):
        mxu_batch(k0)
    for r in range(m):
        y_ref[0, 0, r * DFT_RADIX:(r + 1) * DFT_RADIX, :] = ystage_ref[r * Y_PITCH:r * Y_PITCH + DFT_RADIX, :]


def _fourier(uf, uf_meta, twc, tws, dmat, cs):
    b, _, s, _ = uf.shape
    n2 = (s + N_META) // DFT_RADIX
    m = n2 - 1
    assert N_GROUPS % 2 == 0

    def next_step(bi, g):
        wrap = g + 1 == N_GROUPS
        return jnp.where(wrap, jnp.where(bi + 1 == b, 0, bi + 1), bi), jnp.where(wrap, 0, g + 1)

    def next_x(bi, g):
        bn, gn = next_step(bi, g)
        return bn, gn, 0, 0

    def next_meta(bi, g):
        return next_step(bi, g)[1], 0, 0

    return pl.pallas_call(
        functools.partial(_fourier_kernel, n2),
        grid=(b, N_GROUPS),
        in_specs=[
            _const_spec((1, 1, s, GROUP_DIM)),
            _const_spec((1, N_META, GROUP_DIM)),
            pl.BlockSpec((1, 1, s, GROUP_DIM), next_x),
            pl.BlockSpec((1, N_META, GROUP_DIM), next_meta),
            _const_spec((DFT_RADIX, m, GROUP_DIM)),
            _const_spec((DFT_RADIX, m, GROUP_DIM)),
            _const_spec((m, 2 * m)),
            _const_spec((2 * GROUP_DIM, GROUP_DIM)),
        ],
        out_specs=pl.BlockSpec((1, 1, s, GROUP_DIM), lambda bi, g: (bi, g, 0, 0)),
        out_shape=jax.ShapeDtypeStruct((b, N_GROUPS, s, GROUP_DIM), jnp.float32),
        scratch_shapes=[
            pltpu.VMEM((N_META + m, GROUP_DIM), jnp.float32),
            pltpu.VMEM((2, m, DFT_RADIX * 2 * GROUP_DIM), jnp.bfloat16),
            pltpu.VMEM((2, F32_SUBLANES, DFT_RADIX * 2 * GROUP_DIM), jnp.float32),
            pltpu.VMEM((m * Y_PITCH, GROUP_DIM), jnp.float32),
        ],
        compiler_params=pltpu.CompilerParams(
            dimension_semantics=("arbitrary", "arbitrary"), vmem_limit_bytes=VMEM_LIMIT_BYTES),
        name="fourier",
    )(uf, uf_meta, uf, uf_meta, twc, tws, dmat, cs)


def _fourier_tables(s):
    n2 = (s + N_META) // DFT_RADIX
    length = DFT_RADIX * n2
    j2 = np.arange(1, n2, dtype=np.int64)
    k1 = np.arange(DFT_RADIX, dtype=np.int64)
    phi = 2.0 * np.pi * ((k1[:, None] * j2[None, :]) % length) / length
    twc = jnp.broadcast_to(jnp.asarray(np.cos(phi), jnp.float32)[:, :, None], (DFT_RADIX, n2 - 1, GROUP_DIM))
    tws = jnp.broadcast_to(jnp.asarray(np.sin(phi), jnp.float32)[:, :, None], (DFT_RADIX, n2 - 1, GROUP_DIM))
    theta = 2.0 * np.pi * ((j2[:, None] * j2[None, :]) % n2) / n2
    dmat = jnp.asarray(np.concatenate([np.cos(theta), np.sin(theta)], axis=1), jnp.float32)
    c = np.arange(GROUP_DIM, dtype=np.int64)
    psi = 2.0 * np.pi * ((c[:, None] * c[None, :]) % GROUP_DIM) / GROUP_DIM
    cs = jnp.asarray(np.concatenate([np.cos(psi), np.sin(psi)], axis=0), jnp.float32)
    return twc, tws, dmat.astype(jnp.bfloat16), cs.astype(jnp.bfloat16)


def _attn_kernel(tq, q_ref, qn_ref, kt_ref, ktn_ref, v_ref, kmt_ref, vm_ref, o_ref, s_ref, pm_ref):
    bf = jnp.bfloat16
    f32 = jnp.float32
    tiles = o_ref.shape[1] // tq
    n, _, tk = kt_ref.shape[2:]
    group = s_ref.shape[2] // tk
    n_groups = n // group

    def produce(slot, qt, key_chunk):
        pmax = None
        for j in range(group):
            s = jnp.dot(qt, key_chunk(j), preferred_element_type=f32)
            s_ref[slot, :, j * tk:(j + 1) * tk] = s
            for l in range(tk // LANES):
                blk = s[:, l * LANES:(l + 1) * LANES]
                pmax = blk if pmax is None else jnp.maximum(pmax, blk)
        pm_ref[slot] = pmax

    def with_ones(vals):
        return jnp.concatenate([vals, jnp.ones_like(vals)], axis=1)

    def absorb(slot, g, m_i, acc, qt):
        pmax = pm_ref[slot]
        if g == 0:
            lane = lax.broadcasted_iota(jnp.int32, (tq, LANES), 1)
            s_meta = jnp.dot(qt, kmt_ref[0, 0], preferred_element_type=f32)
            s_meta = jnp.where(lane < N_META, s_meta, -jnp.inf)
            pmax = jnp.maximum(pmax, s_meta)
        m_new = jnp.max(pmax, axis=1, keepdims=True)
        if g > 0:
            m_new = jnp.maximum(m_i, m_new)
        pv = None
        for c0 in range(0, group * tk, MXU_TILE):
            p = jnp.exp2((s_ref[slot, :, c0:c0 + MXU_TILE] - m_new).astype(bf))
            k0 = g * group * tk + c0
            part = jnp.dot(p, with_ones(v_ref[0, 0, k0:k0 + MXU_TILE, :]), preferred_element_type=f32)
            pv = part if pv is None else pv + part
        if g == 0:
            p_meta = jnp.exp2((s_meta - m_new).astype(bf))
            return m_new, pv + jnp.dot(p_meta, with_ones(vm_ref[0, 0]), preferred_element_type=f32)
        return m_new, jnp.exp2(m_i - m_new) * acc + pv

    first_step = (pl.program_id(0) == 0) & (pl.program_id(1) == 0) & (pl.program_id(2) == 0)

    @pl.when(first_step)
    def _():
        produce(0, q_ref[0, 0, 0:tq, :], lambda j: kt_ref[0, 0, j])

    for t in range(tiles):
        qt = q_ref[0, 0, t * tq:(t + 1) * tq, :]
        m = acc = None
        for g in range(n_groups):
            if g + 1 < n_groups:
                produce((g + 1) % 2, qt, lambda j, g=g: kt_ref[0, 0, (g + 1) * group + j])
            elif t + 1 < tiles:
                produce(0, q_ref[0, 0, (t + 1) * tq:(t + 2) * tq, :], lambda j: kt_ref[0, 0, j])
            else:
                produce(0, qn_ref[0, 0], lambda j: ktn_ref[0, 0, j])
            m, acc = absorb(g % 2, g, m, acc, qt)
        o_ref[0, t * tq:(t + 1) * tq, :] = (acc[:, :V_DIM] / acc[:, V_DIM:]).astype(bf)


def _attention(q, kt, v, kmt, vm, tq, tiles, tg):
    b, nh, s, _ = q.shape
    n, _, tk = kt.shape[2:]
    nq = s // (tiles * tq)
    assert n * tk == s and tg % tk == 0 and (s // tg) % 2 == 0

    def next_step(bi, h, qi):
        wrap_q = qi + 1 == nq
        wrap_h = wrap_q & (h + 1 == nh)
        bn = jnp.where(wrap_h, jnp.where(bi + 1 == b, 0, bi + 1), bi)
        hn = jnp.where(wrap_q, jnp.where(h + 1 == nh, 0, h + 1), h)
        return bn, hn, jnp.where(wrap_q, 0, qi + 1)

    def next_q(bi, h, qi):
        bn, hn, qn = next_step(bi, h, qi)
        return bn, hn, qn * tiles, 0

    def next_keys(bi, h, qi):
        bn, hn, _ = next_step(bi, h, qi)
        return bn, hn, 0, 0, 0

    return pl.pallas_call(
        functools.partial(_attn_kernel, tq),
        grid=(b, nh, nq),
        in_specs=[
            pl.BlockSpec((1, 1, tiles * tq, QK_DIM), lambda bi, h, qi: (bi, h, qi, 0)),
            pl.BlockSpec((1, 1, tq, QK_DIM), next_q),
            pl.BlockSpec((1, 1, n, QK_DIM, tk), lambda bi, h, qi: (bi, h, 0, 0, 0)),
            pl.BlockSpec((1, 1, tg // tk, QK_DIM, tk), next_keys),
            pl.BlockSpec((1, 1, s, V_DIM), lambda bi, h, qi: (bi, h, 0, 0)),
            pl.BlockSpec((1, 1, QK_DIM, LANES), lambda bi, h, qi: (0, h, 0, 0)),
            pl.BlockSpec((1, 1, LANES, V_DIM), lambda bi, h, qi: (0, h, 0, 0)),
        ],
        out_specs=pl.BlockSpec((1, tiles * tq, V_DIM), lambda bi, h, qi: (bi, qi, h)),
        out_shape=jax.ShapeDtypeStruct((b, s, ATTN_DIM), jnp.bfloat16),
        scratch_shapes=[pltpu.VMEM((2, tq, tg), jnp.float32), pltpu.VMEM((2, tq, LANES), jnp.float32)],
        compiler_params=pltpu.CompilerParams(
            dimension_semantics=("arbitrary", "arbitrary", "arbitrary"), vmem_limit_bytes=VMEM_LIMIT_BYTES),
        name="attention",
    )(q, q, kt, kt, v, kmt, vm)


def _tail_kernel(x_ref, y_ref, o_ref, sa_ref, sb_ref, wfo_ref, wao_ref, wo_ref, g2_ref, wg_ref, wu_ref, wd_ref,
                 gf_ref, out_ref):
    bf = jnp.bfloat16
    f32 = jnp.float32
    yf = jnp.concatenate([y_ref[0, g] for g in range(N_GROUPS)], axis=1).astype(bf)
    ya = jnp.dot(yf, wfo_ref[...], preferred_element_type=f32)
    yb = jnp.dot(o_ref[0], wao_ref[...], preferred_element_type=f32)
    merged = sa_ref[0].astype(f32) * ya + sb_ref[0].astype(f32) * yb
    x1 = x_ref[0] + jnp.dot(merged.astype(bf), wo_ref[...], preferred_element_type=f32)
    h2 = _rms(x1, g2_ref[...]).astype(bf)
    x2 = x1
    for c0 in range(0, D_FF, FF_CHUNK):
        c1 = min(c0 + FF_CHUNK, D_FF)
        gate = jnp.dot(h2, wg_ref[:, c0:c1], preferred_element_type=f32)
        up = jnp.dot(h2, wu_ref[:, c0:c1], preferred_element_type=f32)
        act = (gate * _sigmoid(gate) * up).astype(bf)
        x2 = x2 + jnp.dot(act, wd_ref[c0:c1, :], preferred_element_type=f32)
    out_ref[0] = _rms(x2, gf_ref[...])


def _tail(x, y, o, sa, sb, wfo, wao, wo, g2, wg, wu, wd, gf, tm):
    b, s, _ = x.shape
    tok = lambda bi, si: (bi, si, 0)
    const = lambda bi, si: (0, 0)

    def weight(shape):
        return pl.BlockSpec(shape, const, pipeline_mode=pl.Buffered(1))

    return pl.pallas_call(
        _tail_kernel,
        grid=(b, s // tm),
        in_specs=[
            pl.BlockSpec((1, tm, D_MODEL), tok),
            pl.BlockSpec((1, N_GROUPS, tm, GROUP_DIM), lambda bi, si: (bi, 0, si, 0)),
            pl.BlockSpec((1, tm, ATTN_DIM), tok),
            pl.BlockSpec((1, tm, D_MODEL), tok),
            pl.BlockSpec((1, tm, D_MODEL), tok),
            weight((FOURIER_DIM, D_MODEL)),
            weight((ATTN_DIM, D_MODEL)),
            weight((D_MODEL, D_MODEL)),
            pl.BlockSpec((1, D_MODEL), const),
            weight((D_MODEL, D_FF)),
            weight((D_MODEL, D_FF)),
            weight((D_FF, D_MODEL)),
            pl.BlockSpec((1, D_MODEL), const),
        ],
        out_specs=pl.BlockSpec((1, tm, D_MODEL), tok),
        out_shape=jax.ShapeDtypeStruct((b, s, D_MODEL), jnp.float32),
        compiler_params=pltpu.CompilerParams(
            dimension_semantics=("parallel", "parallel"), vmem_limit_bytes=VMEM_LIMIT_BYTES),
        name="tail",
    )(x, y, o, sa, sb, wfo, wao, wo, g2, wg, wu, wd, gf)


def _rope_tables(start, length):
    half = ROPE_DIM // 2
    lane = np.arange(LANES)
    inv = 1.0 / (ROPE_THETA ** (jnp.arange(0, ROPE_DIM, 2, dtype=jnp.float32) / ROPE_DIM))
    inv = inv[lane % half]
    sign = jnp.asarray(np.where((lane // half) % 2 == 0, -1.0, 1.0), jnp.float32)
    ang = jnp.arange(start, start + length, dtype=jnp.float32)[:, None] * inv[None, :]
    return jnp.cos(ang), jnp.sin(ang) * sign


def _prepare_weights(norm1_g, w_in, q_norm_g, kv_norm_g, w_uq, w_ukv, w_fourier_out, w_attn_out, w_o, norm2_g,
                     w_ffn_gate, w_ffn_up, w_ffn_down, final_norm_g):
    bf = jnp.bfloat16
    w = w_in[0]
    s_uf, s_cq, s_ckv, s_kr = FOURIER_DIM, FOURIER_DIM + Q_RANK, FOURIER_DIM + Q_RANK + KV_RANK, \
        FOURIER_DIM + Q_RANK + KV_RANK + ROPE_DIM
    win = jnp.concatenate(
        [w[:, :s_ckv], w[:, s_kr:], w[:, s_ckv:s_kr], jnp.zeros((D_MODEL, LANES - ROPE_DIM), w.dtype)], axis=1).astype(bf)
    wq = w_uq[0].reshape(Q_RANK, N_HEADS, QK_DIM)
    wuq = jnp.concatenate([wq[:, :, :NOPE_DIM].reshape(Q_RANK, -1), wq[:, :, NOPE_DIM:].reshape(Q_RANK, -1)], axis=1).astype(bf)
    wkv = w_ukv[0].reshape(KV_RANK, N_HEADS, NOPE_DIM + V_DIM)
    wkt = wkv[:, :, :NOPE_DIM].reshape(KV_RANK, -1).T.astype(bf)
    wv = wkv[:, :, NOPE_DIM:].reshape(KV_RANK, -1).astype(bf)
    row = lambda g: g.reshape(1, -1)
    return dict(
        g1=row(norm1_g[0]), win=win, qg=row(q_norm_g[0]), kvg=row(kv_norm_g[0]), wuq=wuq, wkt=wkt, wv=wv,
        wfo=w_fourier_out[0].astype(bf), wao=w_attn_out[0].astype(bf), wo=w_o[0].astype(bf), g2=row(norm2_g[0]),
        wg=w_ffn_gate[0].astype(bf), wu=w_ffn_up[0].astype(bf), wd=w_ffn_down[0].astype(bf), gf=row(final_norm_g))


def _project(p, x, cos2, sin2, tm):
    return _inproj(x, p["g1"], p["win"], p["qg"], p["kvg"], p["wuq"], p["wkt"], p["wv"], cos2, sin2, tm)


def _meta_projection(meta, p):
    meta_pad = jnp.zeros((1, LANES, D_MODEL), meta.dtype).at[0, :N_META].set(meta)
    uf_m, _, kt_m, v_m, _, _ = _project(p, meta_pad, *_rope_tables(0, LANES), LANES)
    return uf_m[0, :, :N_META], kt_m[:, :, 0], v_m


def _trunk(x, meta_proj, rope, p, tk, tq, attn_step_scores, attn_groups, tm_tail):
    s = x.shape[1]
    tg = s // attn_groups
    attn_tiles = max(1, attn_step_scores // (tq * s))
    uf_m, kt_m, v_m = meta_proj
    cos2, sin2 = rope
    uf, q, kt, v, sa, sb = _project(p, x, cos2, sin2, tk)
    y = _fourier(uf, uf_m, *_fourier_tables(s))
    o = _attention(q, kt, v, kt_m, v_m, tq, attn_tiles, tg)
    return _tail(x, y, o, sa, sb, p["wfo"], p["wao"], p["wo"], p["g2"], p["wg"], p["wu"], p["wd"], p["gf"], tm_tail)


def kernel(x_prompt, x_sample, meta_tokens, norm1_g, w_in, q_norm_g, kv_norm_g, w_uq, w_ukv, w_fourier_out,
           w_attn_out, w_o, norm2_g, w_ffn_gate, w_ffn_up, w_ffn_down, final_norm_g):
    p = _prepare_weights(norm1_g, w_in, q_norm_g, kv_norm_g, w_uq, w_ukv, w_fourier_out, w_attn_out, w_o, norm2_g,
                         w_ffn_gate, w_ffn_up, w_ffn_down, final_norm_g)
    rope = _rope_tables(N_META, max(x_prompt.shape[1], x_sample.shape[1]))
    meta_proj = _meta_projection(meta_tokens, p)
    cfg = dict(tk=512, tq=1024, attn_step_scores=1 << 24, attn_groups=4, tm_tail=512)
    y_prompt = _trunk(x_prompt, meta_proj, rope, p, **cfg)
    y_sample = _trunk(x_sample, meta_proj, rope, p, **cfg)
    return (y_prompt, y_sample)
```
